```python
import math
import jax, jax.numpy as jnp
from jax import lax
import numpy as np

D_MODEL = 2048
BATCH = 2
SEQ = 4096
DEPTH = 4
DEC_BATCH = 128
DEC_SEQ = 4
PAST_LEN = 8192
PAGE_SIZE = 128

N_BRANCH = 4
BRANCH_WIDTH = 512
ADA_CHUNKS = 6
EPS = 1e-6
GLA_HEADS = 4
GLA_DK = 64
GLA_DV = 128
GLA_RANK = 16
GLA_TAU = 16.0
GLA_CHUNK = 16
MLA_HEADS = 8
MLA_Q_RANK = 384
MLA_KV_RANK = 128
MLA_NOPE = 64
MLA_ROPE = 32
MLA_V = 64
MLA_SCALE = (MLA_NOPE + MLA_ROPE) ** -0.5
ROPE_THETA = 10000.0
Q_BLOCK = 128
ML_HEADS = 4
ML_DK = 64
ML_DV = 128
ML_CHUNK = 64
SG_GROUPS = 4
SG_CHUNK = 128
SG_WIDTH = 512
PEER_HEADS = 8
PEER_NKEYS = 128
PEER_DKEY = 128
PEER_TOPK = 16
PEER_EXPERTS = PEER_NKEYS * PEER_NKEYS
PEER_BLOCK = 128
IN_SIZES = (N_BRANCH * D_MODEL,
            GLA_HEADS * GLA_DK, GLA_HEADS * GLA_DK, GLA_HEADS * GLA_DV, GLA_HEADS * GLA_DV, GLA_RANK,
            MLA_Q_RANK, MLA_KV_RANK, MLA_ROPE,
            ML_HEADS * ML_DK, ML_HEADS * ML_DK, ML_HEADS * ML_DV, ML_HEADS * ML_DV, ML_HEADS, ML_HEADS,
            SG_WIDTH, SG_WIDTH)
D_IN = sum(IN_SIZES)

kernel_name = 'hybrid_gla_mla_mlstm_sgu_peer_step'


def rmsnorm(x, g):
    xf = x.astype(jnp.float32)
    y = xf * lax.rsqrt(jnp.mean(xf * xf, axis=-1, keepdims=True) + EPS)
    return (y * g).astype(x.dtype)


def rope(x, pos):
    half = x.shape[-1] // 2
    inv = ROPE_THETA ** (-jnp.arange(half, dtype=jnp.float32) / half)
    ang = pos.astype(jnp.float32)[:, None] * inv
    shape = (ang.shape[0],) + (1,) * (x.ndim - 3) + (half,)
    cos = jnp.cos(ang).reshape(shape).astype(x.dtype)
    sin = jnp.sin(ang).reshape(shape).astype(x.dtype)
    x1, x2 = x[..., :half], x[..., half:]
    return jnp.concatenate([x1 * cos - x2 * sin, x1 * sin + x2 * cos], axis=-1)


def _chunks(a, nc, L):
    a = a.reshape((a.shape[0], nc, L) + a.shape[2:])
    return jnp.moveaxis(jnp.moveaxis(a, 1, 0), 2, 3)


def _unchunks(a, B, T):
    a = jnp.moveaxis(jnp.moveaxis(a, 3, 2), 0, 1)
    return a.reshape((B, T) + a.shape[3:])


def gla_scan(q, k, v, log_a, S0):
    B, T = q.shape[:2]
    L = math.gcd(T, GLA_CHUNK)
    nc = T // L
    qc, kc, vc, ac = (_chunks(a.astype(jnp.float32), nc, L) for a in (q, k, v, log_a))
    causal = jnp.tril(jnp.ones((L, L), dtype=bool))[:, :, None]

    def step(S, inp):
        qi, ki, vi, ai = inp
        b = jnp.cumsum(ai, axis=-2)
        decay = jnp.exp(jnp.where(causal, b[..., :, None, :] - b[..., None, :, :], -jnp.inf))
        att = jnp.einsum('bhtk,bhsk,bhtsk->bhts', qi, ki, decay)
        o = jnp.einsum('bhts,bhsv->bhtv', att, vi) + jnp.einsum('bhtk,bhkv->bhtv', qi * jnp.exp(b), S)
        b_end = b[..., -1:, :]
        S = jnp.exp(b_end)[..., 0, :, None] * S + jnp.einsum('bhsk,bhsv->bhkv', ki * jnp.exp(b_end - b), vi)
        return S, o

    S, o = lax.scan(step, S0.astype(jnp.float32), (qc, kc, vc, ac))
    return _unchunks(o, B, T), S


def mlstm_scan(q, k, v, i_pre, f_pre, C0, n0, m0):
    B, T = q.shape[:2]
    L = math.gcd(T, ML_CHUNK)
    nc = T // L
    qc, kc, vc, ic, fc = (_chunks(a.astype(jnp.float32), nc, L) for a in (q, k, v, i_pre, f_pre))
    causal = jnp.tril(jnp.ones((L, L), dtype=bool))

    def step(carry, inp):
        C, n, m = carry
        qi, ki, vi, ii, fi = inp
        F = jnp.cumsum(jax.nn.log_sigmoid(fi), axis=-1)
        Dm = jnp.where(causal, F[..., :, None] - F[..., None, :] + ii[..., None, :], -jnp.inf)
        prev = F + m[..., None]
        m_t = jnp.maximum(prev, jnp.max(Dm, axis=-1))
        A = jnp.exp(Dm - m_t[..., None]) * jnp.einsum('bhtk,bhsk->bhts', qi, ki)
        w_prev = jnp.exp(prev - m_t)
        num = w_prev[..., None] * jnp.einsum('bhtk,bhkv->bhtv', qi, C) + jnp.einsum('bhts,bhsv->bhtv', A, vi)
        nq = w_prev * jnp.einsum('bhtk,bhk->bht', qi, n) + jnp.sum(A, axis=-1)
        h = num / jnp.maximum(jnp.abs(nq), jnp.exp(-m_t))[..., None]
        m_new = m_t[..., -1]
        w_src = jnp.exp(F[..., -1:] - F + ii - m_new[..., None])
        w_old = jnp.exp(F[..., -1] + m - m_new)
        C = w_old[..., None, None] * C + jnp.einsum('bhsk,bhsv->bhkv', ki * w_src[..., None], vi)
        n = w_old[..., None] * n + jnp.einsum('bhs,bhsk->bhk', w_src, ki)
        return (C, n, m_new), h

    carry0 = (C0.astype(jnp.float32), n0.astype(jnp.float32), m0.astype(jnp.float32))
    carry, h = lax.scan(step, carry0, (qc, kc, vc, ic, fc))
    return _unchunks(h, B, T), carry


def mla(dq, dkv, kr, pos, p, past_ckv, past_kr):
    B, T, _ = dq.shape
    q = (rmsnorm(dq, p['mla_gq']) @ p['mla_w_uq']).reshape(B, T, MLA_HEADS, MLA_NOPE + MLA_ROPE)
    q_lat = jnp.einsum('bthn,rhn->bthr', q[..., :MLA_NOPE], p['mla_w_uk'])
    q_rope = rope(q[..., MLA_NOPE:], pos)
    ckv = rmsnorm(dkv, p['mla_gkv'])
    krope = rope(kr, pos)
    if past_ckv is None:
        keys_c, keys_r, k_pos = ckv, krope, pos
    else:
        keys_c = jnp.concatenate([past_ckv, ckv], axis=1)
        keys_r = jnp.concatenate([past_kr, krope], axis=1)
        k_pos = jnp.concatenate([jnp.arange(past_ckv.shape[1], dtype=jnp.int32), pos])

    def attend(args):
        ql, qr, qp = args
        s = (jnp.einsum('bthr,bsr->bhts', ql, keys_c)
             + jnp.einsum('bthe,bse->bhts', qr, keys_r)).astype(jnp.float32) * MLA_SCALE
        s = jnp.where(k_pos[None, :] <= qp[:, None], s, -jnp.inf)
        pr = jax.nn.softmax(s, axis=-1).astype(keys_c.dtype)
        return jnp.einsum('bhts,bsr->bthr', pr, keys_c)

    if T > Q_BLOCK and T % Q_BLOCK == 0:
        nb = T // Q_BLOCK
        blk = lambda a: jnp.moveaxis(a.reshape((B, nb, Q_BLOCK) + a.shape[2:]), 1, 0)
        o_lat = lax.map(attend, (blk(q_lat), blk(q_rope), pos.reshape(nb, Q_BLOCK)))
        o_lat = jnp.moveaxis(o_lat, 0, 1).reshape(B, T, MLA_HEADS, MLA_KV_RANK)
    else:
        o_lat = attend((q_lat, q_rope, pos))
    y = jnp.einsum('bthr,rhv->bthv', o_lat, p['mla_w_uv']).reshape(B, T, MLA_HEADS * MLA_V)
    return y, ckv, krope


def sgu(u, v_raw, g_v, w_s, b_s):
    B, T, W = u.shape
    v = rmsnorm(jax.nn.gelu(v_raw), g_v)
    Tp = -(-T // SG_CHUNK) * SG_CHUNK
    vp = jnp.pad(v, ((0, 0), (0, Tp - T), (0, 0))).reshape(B, Tp // SG_CHUNK, SG_CHUNK, SG_GROUPS, W // SG_GROUPS)
    w_causal = w_s * jnp.tril(jnp.ones((SG_CHUNK, SG_CHUNK), w_s.dtype))
    mixed = jnp.einsum('gts,bnsgc->bntgc', w_causal, vp) + b_s.T[:, :, None]
    mixed = mixed.reshape(B, Tp, W)[:, :T]
    return jax.nn.gelu(u) * mixed, v


def peer(h, w_q, k1, k2, u_tab, v_tab):
    shape = h.shape
    x = h.reshape(-1, shape[-1])
    N = x.shape[0]
    Np = -(-N // PEER_BLOCK) * PEER_BLOCK
    xb = jnp.pad(x, ((0, Np - N), (0, 0))).reshape(Np // PEER_BLOCK, PEER_BLOCK, shape[-1])
    half = PEER_DKEY // 2

    def block(xt):
        q = (xt @ w_q).reshape(PEER_BLOCK, PEER_HEADS, PEER_DKEY)
        s1 = jnp.einsum('thd,hnd->thn', q[..., :half], k1)
        s2 = jnp.einsum('thd,hnd->thn', q[..., half:], k2)
        v1, i1 = lax.top_k(s1, PEER_TOPK)
        v2, i2 = lax.top_k(s2, PEER_TOPK)
        cand = (v1[..., :, None] + v2[..., None, :]).reshape(PEER_BLOCK, PEER_HEADS, PEER_TOPK * PEER_TOPK)
        cidx = (i1[..., :, None] * PEER_NKEYS + i2[..., None, :]).reshape(PEER_BLOCK, PEER_HEADS, PEER_TOPK * PEER_TOPK)
        top, sel = lax.top_k(cand, PEER_TOPK)
        eidx = jnp.take_along_axis(cidx, sel, axis=-1)
        g = jax.nn.softmax(top.astype(jnp.float32), axis=-1).astype(xt.dtype)
        a = jnp.einsum('td,thkd->thk', xt, u_tab[eidx])
        return jnp.einsum('thk,thkd->td', g * jax.nn.gelu(a), v_tab[eidx])

    return lax.map(block, xb).reshape(Np, shape[-1])[:N].reshape(shape)


def mixer(h, pos, p, past):
    B, T, _ = h.shape
    dt = h.dtype
    splits = np.cumsum(IN_SIZES)[:-1].tolist()
    (gate, a_q, a_k, a_v, a_r, a_g, b_dq, b_dkv, b_kr,
     c_q, c_k, c_v, c_o, c_i, c_f, d_u, d_v) = jnp.split(h @ p['w_in'], splits, axis=-1)
    if past is None:
        S0 = jnp.zeros((B, GLA_HEADS, GLA_DK, GLA_DV), jnp.float32)
        C0 = jnp.zeros((B, ML_HEADS, ML_DK, ML_DV), jnp.float32)
        n0 = jnp.zeros((B, ML_HEADS, ML_DK), jnp.float32)
        m0 = jnp.zeros((B, ML_HEADS), jnp.float32)
        past_ckv = None
        past_kr = None
    else:
        S0, C0, n0, m0 = past['S'], past['C'], past['n'], past['m']
        past_ckv, past_kr = past['ckv'], past['kr']
    log_a = jax.nn.log_sigmoid((a_g @ p['gla_wa2'] + p['gla_ba']).astype(jnp.float32)) / GLA_TAU
    o_a, S = gla_scan(a_q.reshape(B, T, GLA_HEADS, GLA_DK) * GLA_DK ** -0.5,
                      a_k.reshape(B, T, GLA_HEADS, GLA_DK),
                      a_v.reshape(B, T, GLA_HEADS, GLA_DV),
                      log_a.reshape(B, T, GLA_HEADS, GLA_DK), S0)
    y_a = rmsnorm(o_a.astype(dt), p['gla_norm'].reshape(GLA_HEADS, GLA_DV)).reshape(B, T, -1) * jax.nn.silu(a_r)
    y_b, ckv, krope = mla(b_dq, b_dkv, b_kr, pos, p, past_ckv, past_kr)
    o_c, (C, n, m) = mlstm_scan(c_q.reshape(B, T, ML_HEADS, ML_DK),
                                c_k.reshape(B, T, ML_HEADS, ML_DK) * ML_DK ** -0.5,
                                c_v.reshape(B, T, ML_HEADS, ML_DV),
                                c_i + p['ml_bi'], c_f + p['ml_bf'], C0, n0, m0)
    y_c = jax.nn.sigmoid(c_o) * rmsnorm(o_c.astype(dt), p['ml_norm'].reshape(ML_HEADS, ML_DV)).reshape(B, T, -1)
    y_d, v_rows = sgu(d_u, d_v, p['sg_gv'], p['sg_ws'], p['sg_bs'])
    gates = jax.nn.sigmoid(gate).reshape(B, T, N_BRANCH, D_MODEL)
    merged = gates[:, :, 0] * (y_a @ p['w_br'][0])
    for i, y in enumerate((y_b, y_c, y_d), start=1):
        merged = merged + gates[:, :, i] * (y @ p['w_br'][i])
    new = {'ckv': ckv, 'kr': krope, 'S': S.astype(dt), 'C': C.astype(dt), 'n': n.astype(dt),
           'm': m.astype(dt), 'v': v_rows}
    return merged @ p['w_o'], new


def layer(x, c, pos, p, past):
    mod = jax.nn.silu(c) @ p['w_ada'] + p['b_ada']
    sh1, sc1, g1, sh2, sc2, g2 = jnp.split(mod[:, None, :], ADA_CHUNKS, axis=-1)
    h = rmsnorm(x, p['norm1']) * (1 + sc1) + sh1
    mix, new = mixer(h, pos, p, past)
    x = x + g1 * mix
    h = rmsnorm(x, p['norm2']) * (1 + sc2) + sh2
    x = x + g2 * peer(h, p['peer_wq'], p['peer_k1'], p['peer_k2'], p['peer_u'], p['peer_v'])
    return x, new


def setup_inputs(seed: int = 0) -> dict:
    key = jax.random.key(seed)
    ks = iter(jax.random.split(key, 64))

    def nrm(shape, scale):
        return jax.random.normal(next(ks), shape, jnp.float32) * scale

    def gain(shape):
        return 1.0 + nrm(shape, 0.05)

    n_pages = PAST_LEN // PAGE_SIZE
    n_used = DEC_BATCH * n_pages
    n_pool = n_used + max(1, n_used // 4)
    page_table = jax.random.permutation(next(ks), n_pool)[:n_used].reshape(DEC_BATCH, n_pages).astype(jnp.int32)
    D = D_MODEL
    return {
        'x_prompt': nrm((BATCH, SEQ, D), 1.0),
        'x_sample': nrm((DEC_BATCH, DEC_SEQ, D), 1.0),
        'cache_mla_ckv': nrm((DEPTH, n_pool, PAGE_SIZE, MLA_KV_RANK), 1.0),
        'cache_mla_kr': nrm((DEPTH, n_pool, PAGE_SIZE, MLA_ROPE), 1.0),
        'state_gla': nrm((DEPTH, DEC_BATCH, GLA_HEADS, GLA_DK, GLA_DV), 0.1),
        'state_mlstm_C': nrm((DEPTH, DEC_BATCH, ML_HEADS, ML_DK, ML_DV), 0.1),
        'state_mlstm_n': nrm((DEPTH, DEC_BATCH, ML_HEADS, ML_DK), 0.1),
        'state_mlstm_m': nrm((DEPTH, DEC_BATCH, ML_HEADS), 1.0),
        'page_table': page_table,
        'c_prompt': nrm((BATCH, D), 1.0),
        'c_sample': nrm((DEC_BATCH, D), 1.0),
        'w_ada': nrm((DEPTH, D, ADA_CHUNKS * D), 0.5 * D ** -0.5),
        'b_ada': nrm((DEPTH, ADA_CHUNKS * D), 0.02),
        'norm1': gain((DEPTH, D)),
        'norm2': gain((DEPTH, D)),
        'w_in': nrm((DEPTH, D, D_IN), D ** -0.5),
        'gla_wa2': nrm((DEPTH, GLA_RANK, GLA_HEADS * GLA_DK), GLA_RANK ** -0.5),
        'gla_ba': nrm((DEPTH, GLA_HEADS * GLA_DK), 0.1),
        'gla_norm': gain((DEPTH, GLA_HEADS * GLA_DV)),
        'mla_w_uq': nrm((DEPTH, MLA_Q_RANK, MLA_HEADS * (MLA_NOPE + MLA_ROPE)), MLA_Q_RANK ** -0.5),
        'mla_gq': gain((DEPTH, MLA_Q_RANK)),
        'mla_gkv': gain((DEPTH, MLA_KV_RANK)),
        'mla_w_uk': nrm((DEPTH, MLA_KV_RANK, MLA_HEADS, MLA_NOPE), MLA_KV_RANK ** -0.5),
        'mla_w_uv': nrm((DEPTH, MLA_KV_RANK, MLA_HEADS, MLA_V), MLA_KV_RANK ** -0.5),
        'ml_bi': nrm((DEPTH, ML_HEADS), 0.1),
        'ml_bf': 3.0 + nrm((DEPTH, ML_HEADS), 0.5),
        'ml_norm': gain((DEPTH, ML_HEADS * ML_DV)),
        'sg_gv': gain((DEPTH, SG_WIDTH)),
        'sg_ws': nrm((DEPTH, SG_GROUPS, SG_CHUNK, SG_CHUNK), SG_CHUNK ** -0.5),
        'sg_bs': 1.0 + nrm((DEPTH, SG_GROUPS, SG_CHUNK), 0.1),
        'w_br': nrm((DEPTH, N_BRANCH, BRANCH_WIDTH, D), BRANCH_WIDTH ** -0.5),
        'w_o': nrm((DEPTH, D, D), D ** -0.5),
        'peer_wq': nrm((DEPTH, D, PEER_HEADS * PEER_DKEY), D ** -0.5),
        'peer_k1': nrm((DEPTH, PEER_HEADS, PEER_NKEYS, PEER_DKEY // 2), (PEER_DKEY // 2) ** -0.5),
        'peer_k2': nrm((DEPTH, PEER_HEADS, PEER_NKEYS, PEER_DKEY // 2), (PEER_DKEY // 2) ** -0.5),
        'peer_u': nrm((DEPTH, PEER_EXPERTS, D), D ** -0.5),
        'peer_v': nrm((DEPTH, PEER_EXPERTS, D), 0.5),
        'final_norm': gain((D,)),
    }


def reference(x_prompt, x_sample, cache_mla_ckv, cache_mla_kr, state_gla, state_mlstm_C, state_mlstm_n,
              state_mlstm_m, page_table, c_prompt, c_sample, w_ada, b_ada, norm1, norm2, w_in, gla_wa2, gla_ba,
              gla_norm, mla_w_uq, mla_gq, mla_gkv, mla_w_uk, mla_w_uv, ml_bi, ml_bf, ml_norm, sg_gv, sg_ws, sg_bs,
              w_br, w_o, peer_wq, peer_k1, peer_k2, peer_u, peer_v, final_norm):
    db = x_sample.shape[0]
    past_len = page_table.shape[1] * cache_mla_ckv.shape[2]
    pos_p = jnp.arange(x_prompt.shape[1], dtype=jnp.int32)
    pos_s = past_len + jnp.arange(x_sample.shape[1], dtype=jnp.int32)
    xp, xs = x_prompt, x_sample
    new_p, new_s = [], []
    for l in range(DEPTH):
        p = {'w_ada': w_ada[l], 'b_ada': b_ada[l], 'norm1': norm1[l], 'norm2': norm2[l], 'w_in': w_in[l],
             'gla_wa2': gla_wa2[l], 'gla_ba': gla_ba[l], 'gla_norm': gla_norm[l],
             'mla_w_uq': mla_w_uq[l], 'mla_gq': mla_gq[l], 'mla_gkv': mla_gkv[l],
             'mla_w_uk': mla_w_uk[l], 'mla_w_uv': mla_w_uv[l],
             'ml_bi': ml_bi[l], 'ml_bf': ml_bf[l], 'ml_norm': ml_norm[l],
             'sg_gv': sg_gv[l], 'sg_ws': sg_ws[l], 'sg_bs': sg_bs[l], 'w_br': w_br[l], 'w_o': w_o[l],
             'peer_wq': peer_wq[l], 'peer_k1': peer_k1[l], 'peer_k2': peer_k2[l],
             'peer_u': peer_u[l], 'peer_v': peer_v[l]}
        xp, st_p = layer(xp, c_prompt, pos_p, p, None)
        past = {'ckv': cache_mla_ckv[l][page_table].reshape(db, past_len, MLA_KV_RANK),
                'kr': cache_mla_kr[l][page_table].reshape(db, past_len, MLA_ROPE),
                'S': state_gla[l], 'C': state_mlstm_C[l], 'n': state_mlstm_n[l], 'm': state_mlstm_m[l]}
        xs, st_s = layer(xs, c_sample, pos_s, p, past)
        new_p.append(st_p)
        new_s.append(st_s)
    stk = lambda outs, name: jnp.stack([o[name] for o in outs], axis=0)
    y_prompt = rmsnorm(xp, final_norm)
    y_sample = rmsnorm(xs, final_norm)
    return (y_prompt, y_sample,
            stk(new_p, 'ckv'), stk(new_p, 'kr'), stk(new_p, 'S'), stk(new_p, 'C'), stk(new_p, 'n'), stk(new_p, 'm'),
            stk(new_s, 'ckv'), stk(new_s, 'kr'), stk(new_s, 'S'), stk(new_s, 'C'), stk(new_s, 'n'), stk(new_s, 'm'),
            stk(new_s, 'v'))
```

```python
import functools
import math

import numpy as np
import jax
import jax.numpy as jnp
from jax import lax
from jax.experimental import pallas as pl
from jax.experimental.pallas import tpu as pltpu

f32 = jnp.float32
bf16 = jnp.bfloat16

D_MODEL = 2048
EPS = 1e-6
ADA_CHUNKS = 6
GLA_HEADS, GLA_DK, GLA_DV, GLA_RANK, GLA_TAU, GLA_CHUNK = 4, 64, 128, 16, 16.0, 16
MLA_HEADS, MLA_Q_RANK, MLA_KV_RANK, MLA_NOPE, MLA_ROPE, MLA_V = 8, 384, 128, 64, 32, 64
MLA_SCALE = (MLA_NOPE + MLA_ROPE) ** -0.5
ROPE_THETA = 10000.0
ML_HEADS, ML_DK, ML_DV, ML_CHUNK = 4, 64, 128, 64
SG_GROUPS, SG_CHUNK, SG_WIDTH = 4, 128, 512
PEER_HEADS, PEER_NKEYS, PEER_DKEY, PEER_TOPK = 8, 128, 128, 16
PEER_EXPERTS = PEER_NKEYS * PEER_NKEYS
BRANCH_WIDTH = 512
N_BRANCH = 4

W_A = 1664
W_B = 640
W_C = 1792
W_D = 1024

NEG = -1e30
VMEM_LIMIT_V7X = 56 * 1024 * 1024
PEER_PITCH_PAD = 8


def _cparams(*sem):
    return pltpu.CompilerParams(dimension_semantics=sem, vmem_limit_bytes=VMEM_LIMIT_V7X)


def _bf(x):
    return x.astype(bf16)


def _dot(a, b):
    return jnp.dot(a, b, preferred_element_type=f32)


def _dot_nt(a, b):
    return lax.dot_general(a, b, (((1,), (1,)), ((), ())), preferred_element_type=f32)


def _split2(x):
    hi = _bf(x)
    lo = _bf(x - hi.astype(f32))
    return hi, lo


def _dot3(a, b):
    ah, al = _split2(a)
    bh, bl = _split2(b)
    return _dot(ah, bh) + (_dot(ah, bl) + _dot(al, bh))


def _dot2b(a, b_bf):
    ah, al = _split2(a)
    return _dot(ah, b_bf) + _dot(al, b_bf)


def _sigmoid(x):
    return 1.0 / (1.0 + jnp.exp(-x))


def _log_sigmoid(x):
    return jnp.minimum(x, 0.0) - jnp.log1p(jnp.exp(-jnp.abs(x)))


def _gelu(x):
    return jax.nn.gelu(x)


def _rms(x, g):
    return x * lax.rsqrt(jnp.mean(x * x, axis=-1, keepdims=True) + EPS) * g


def _iota(shape, dim):
    return lax.broadcasted_iota(jnp.int32, shape, dim)


def _seg_cumsum(x, rowmod, L):
    s = 1
    while s < L:
        x = x + jnp.where(rowmod >= s, pltpu.roll(x, s, 0), 0.0)
        s *= 2
    return x


def _seg_last(x, rowmod, L):
    R = x.shape[0]
    s = 1
    while s < L:
        x = jnp.where(rowmod + s < L, pltpu.roll(x, R - s, 0), x)
        s *= 2
    return x


def _expand_heads(x, n_heads, width):
    R = x.shape[0]
    lane_head = _iota((R, n_heads * width), 1) // width
    out = jnp.zeros((R, n_heads * width), f32)
    for h in range(n_heads):
        out = jnp.where(lane_head == h, x[:, h:h + 1], out)
    return out


def _pick_block(n, cands):
    for c in cands:
        if n % c == 0:
            return c
    raise ValueError(f"no block size in {cands} divides {n}")


def _ada_kernel(c_ref, w_ref, b_ref, o_ref):
    c = c_ref[...]
    a = _bf(c * _sigmoid(c))
    o_ref[...] = _dot(a, _bf(w_ref[...])) + b_ref[...]


def _ada_call(c_rows, w_ada, b_ada):
    L, D, N6 = w_ada.shape
    Rc = c_rows.shape[0]
    tn = 1024
    return pl.pallas_call(
        _ada_kernel,
        grid=(L, N6 // tn),
        in_specs=[pl.BlockSpec((Rc, D), lambda l, j: (0, 0)),
                  pl.BlockSpec((None, D, tn), lambda l, j: (l, 0, j)),
                  pl.BlockSpec((None, 1, tn), lambda l, j: (l, 0, j))],
        out_specs=pl.BlockSpec((None, Rc, tn), lambda l, j: (l, 0, j)),
        out_shape=jax.ShapeDtypeStruct((L, Rc, N6), f32),
        compiler_params=_cparams("arbitrary", "arbitrary"),
    )(c_rows, w_ada, b_ada.reshape(L, 1, N6))


class _Rows:
    def __init__(self, B, T, DB, TS):
        self.B, self.T, self.DB, self.TS = B, T, DB, TS
        self.Np, self.Ns = B * T, DB * TS
        self.N = self.Np + self.Ns
        self.R = _pick_block(math.gcd(T, self.Ns), (256, 128))
        self.nPb = self.Np // self.R
        self.nSb = self.Ns // self.R
        self.nb = self.nPb + self.nSb
        self.blocks_per_seq = T // self.R

    def mod_specs(self, l, chunk):
        R, nPb, nSb, Ns = self.R, self.nPb, self.nSb, self.Ns
        s = pl.BlockSpec((None, R, D_MODEL), lambda i: (l, jnp.clip(i - nPb, 0, nSb - 1), chunk))
        p = pl.BlockSpec((None, 8, D_MODEL), lambda i: (l, Ns // 8, chunk))
        return s, p

    def pick(self, i, s_ref, p_ref):
        b = jnp.minimum(i // self.blocks_per_seq, self.B - 1)
        return jnp.where(i >= self.nPb, s_ref[...], p_ref[pl.ds(b, 1), :])


def _modnorm_kernel(x_ref, g_ref, shs_ref, shp_ref, scs_ref, scp_ref, h_ref, *, rows):
    i = pl.program_id(0)
    sh = rows.pick(i, shs_ref, shp_ref)
    sc = rows.pick(i, scs_ref, scp_ref)
    h_ref[...] = _bf(_rms(x_ref[...], g_ref[...]) * (1.0 + sc) + sh)


def _modnorm_call(rows, x, gain, mod, l, sh_chunk, sc_chunk):
    R = rows.R
    shs, shp = rows.mod_specs(l, sh_chunk)
    scs, scp = rows.mod_specs(l, sc_chunk)
    return pl.pallas_call(
        functools.partial(_modnorm_kernel, rows=rows),
        grid=(rows.nb,),
        in_specs=[pl.BlockSpec((R, D_MODEL), lambda i: (i, 0)),
                  pl.BlockSpec((1, D_MODEL), lambda i: (0, 0)), shs, shp, scs, scp],
        out_specs=pl.BlockSpec((R, D_MODEL), lambda i: (i, 0)),
        out_shape=jax.ShapeDtypeStruct((rows.N, D_MODEL), bf16),
        compiler_params=_cparams("arbitrary"),
    )(x, gain.reshape(1, D_MODEL), mod, mod, mod, mod)


def _mm_kernel(x_ref, w_ref, o_ref):
    o_ref[...] = _dot(x_ref[...], w_ref[...]).astype(o_ref.dtype)


def _mm_call(x, w, out_dtype=f32):
    M, K = x.shape
    N = w.shape[1]
    tm = _pick_block(M, (1088, 1024, 640, 512, 256, 128))
    return pl.pallas_call(
        _mm_kernel,
        grid=(M // tm,),
        in_specs=[pl.BlockSpec((tm, K), lambda i: (i, 0)),
                  pl.BlockSpec((K, N), lambda i: (0, 0))],
        out_specs=pl.BlockSpec((tm, N), lambda i: (i, 0)),
        out_shape=jax.ShapeDtypeStruct((M, N), out_dtype),
        compiler_params=_cparams("arbitrary"),
    )(x, w)


def _gla_kernel(*refs, R, L, carry):
    if carry:
        z_ref, wa2_ref, ba_ref, gn_ref, y_ref, so_ref, s_scr = refs
    else:
        z_ref, wa2_ref, ba_ref, gn_ref, s0_ref, y_ref, so_ref = refs
    H, DK, DV = GLA_HEADS, GLA_DK, GLA_DV
    HK, HV = H * DK, H * DV
    nchunk = R // L
    z = z_ref[...]
    q = z[:, 0:HK] * (DK ** -0.5)
    k = z[:, HK:2 * HK]
    v = z[:, 2 * HK:2 * HK + HV]
    r = z[:, 2 * HK + HV:2 * HK + 2 * HV]
    ag = z[:, 2 * HK + 2 * HV:2 * HK + 2 * HV + 128]
    la = _log_sigmoid(_dot3(ag, wa2_ref[...]) + ba_ref[...]) * (1.0 / GLA_TAU)
    rowmod = _iota((R, HK), 0) % L
    b = _seg_cumsum(la, rowmod, L)
    tot = _seg_last(b, rowmod, L)

    sel_row_head = _iota((HK, 128), 0) // DK
    sel_lane = _iota((HK, 128), 1)
    att = jnp.zeros((R, 128), f32)
    for d in range(L):
        ks = k if d == 0 else pltpu.roll(k, d, 0)
        bs = b if d == 0 else pltpu.roll(b, d, 0)
        p = q * ks * jnp.exp(jnp.where(rowmod >= d, b - bs, NEG))
        att = att + _dot2b(p, _bf(sel_lane == d * H + sel_row_head))
    ex_row = _iota((128, HV), 0)
    ex_lane_head = _iota((128, HV), 1) // DV
    o = jnp.zeros((R, HV), f32)
    for d in range(L):
        vs = v if d == 0 else pltpu.roll(v, d, 0)
        o = o + _dot2b(att, _bf(ex_row == d * H + ex_lane_head)) * vs

    qt = _bf(q * jnp.exp(b))
    kd = k * jnp.exp(tot - b)
    kdT = _bf(kd.T)
    totT = tot.T
    vb = _bf(v)
    blockdiag = (_iota((HK, HV), 0) // DK) == (_iota((HK, HV), 1) // DV)
    row_chunk = _iota((R, HK), 0) // L
    col_chunk = _iota((HK, R), 1) // L
    if carry:
        @pl.when(pl.program_id(1) == 0)
        def _():
            s_scr[...] = jnp.zeros((HK, HV), f32)
        S = s_scr[...]
    for c in range(nchunk):
        if not carry:
            rows_ = [jnp.concatenate([s0_ref[c, h] if g == h else jnp.zeros((DK, DV), f32) for g in range(H)], axis=1)
                     for h in range(H)]
            S = jnp.concatenate(rows_, axis=0)
        o = o + _dot(jnp.where(row_chunk == c, qt, jnp.zeros_like(qt)), _bf(S))
        U = _dot(jnp.where(col_chunk == c, kdT, jnp.zeros_like(kdT)), vb)
        dcol = jnp.exp(totT[:, c * L:c * L + 1])
        S = dcol * S + jnp.where(blockdiag, U, 0.0)
        if not carry:
            for h in range(H):
                so_ref[c, h] = S[h * DK:(h + 1) * DK, h * DV:(h + 1) * DV]
    if carry:
        s_scr[...] = S

        @pl.when(pl.program_id(1) == pl.num_programs(1) - 1)
        def _():
            for h in range(H):
                so_ref[h] = S[h * DK:(h + 1) * DK, h * DV:(h + 1) * DV]

    gn = gn_ref[...]
    outs = []
    for h in range(H):
        oh = o[:, h * DV:(h + 1) * DV]
        outs.append(_rms(oh, gn[:, h * DV:(h + 1) * DV]))
    y_ref[...] = _bf(jnp.concatenate(outs, axis=1) * (r * _sigmoid(r)))


def _gla_call(zA, row0, B, T, L, wa2p, ba, gn, s0):
    H, DK, DV = GLA_HEADS, GLA_DK, GLA_DV
    carry = s0 is None
    if carry:
        R = _pick_block(T, (256, 128))
        nt = T // R
        grid = (B, nt)
        rb0 = row0 // R
        in_specs = [pl.BlockSpec((R, W_A), lambda b, t: (rb0 + b * nt + t, 0)),
                    pl.BlockSpec((128, H * DK), lambda b, t: (0, 0)),
                    pl.BlockSpec((1, H * DK), lambda b, t: (0, 0)),
                    pl.BlockSpec((1, H * DV), lambda b, t: (0, 0))]
        out_specs = [pl.BlockSpec((R, H * DV), lambda b, t: (b * nt + t, 0)),
                     pl.BlockSpec((None, H, DK, DV), lambda b, t: (b, 0, 0, 0))]
        scratch = [pltpu.VMEM((H * DK, H * DV), f32)]
        args = (zA, wa2p, ba, gn)
        sem = ("arbitrary", "arbitrary")
    else:
        R = 128
        nb = (B * T) // R
        cpb = R // L
        grid = (nb,)
        rb0 = row0 // R
        in_specs = [pl.BlockSpec((R, W_A), lambda i: (rb0 + i, 0)),
                    pl.BlockSpec((128, H * DK), lambda i: (0, 0)),
                    pl.BlockSpec((1, H * DK), lambda i: (0, 0)),
                    pl.BlockSpec((1, H * DV), lambda i: (0, 0)),
                    pl.BlockSpec((cpb, H, DK, DV), lambda i: (i, 0, 0, 0))]
        out_specs = [pl.BlockSpec((R, H * DV), lambda i: (i, 0)),
                     pl.BlockSpec((cpb, H, DK, DV), lambda i: (i, 0, 0, 0))]
        scratch = []
        args = (zA, wa2p, ba, gn, s0)
        sem = ("arbitrary",)
    return pl.pallas_call(
        functools.partial(_gla_kernel, R=R, L=L, carry=carry),
        grid=grid, in_specs=in_specs, out_specs=out_specs,
        out_shape=[jax.ShapeDtypeStruct((B * T, H * DV), bf16),
                   jax.ShapeDtypeStruct((B, H, DK, DV), f32)],
        scratch_shapes=scratch,
        compiler_params=_cparams(*sem),
    )(*args)


def _mlstm_kernel(*refs, R, L, carry):
    if carry:
        z_ref, bi_ref, bf_ref, gn_ref, y_ref, co_ref, no_ref, mo_ref, c_scr, n_scr, m_scr = refs
    else:
        z_ref, bi_ref, bf_ref, gn_ref, c0_ref, n0_ref, m0_ref, y_ref, co_ref, no_ref, mo_ref = refs
    H, DK, DV = ML_HEADS, ML_DK, ML_DV
    HK, HV = H * DK, H * DV
    nchunk = R // L
    z = z_ref[...]
    q = z[:, 0:HK]
    k = z[:, HK:2 * HK] * (DK ** -0.5)
    v = z[:, 2 * HK:2 * HK + HV]
    og = z[:, 2 * HK + HV:2 * HK + 2 * HV]
    ig = z[:, 2 * HK + 2 * HV:2 * HK + 2 * HV + 128] + bi_ref[...]
    fg = z[:, 2 * HK + 2 * HV + 128:2 * HK + 2 * HV + 256] + bf_ref[...]
    rowmod = _iota((R, 128), 0) % L
    row_chunk128 = _iota((R, 128), 0) // L
    F = _seg_cumsum(_log_sigmoid(fg), rowmod, L)
    FT = F.T
    IT = ig.T

    if carry:
        @pl.when(pl.program_id(1) == 0)
        def _():
            c_scr[...] = jnp.zeros((HK, HV), f32)
            n_scr[...] = jnp.zeros((1, HK), f32)
            m_scr[...] = jnp.zeros((1, 128), f32)

    tpos = _iota((R, R), 0)
    spos = _iota((R, R), 1)
    causal = (tpos // L == spos // L) & (spos <= tpos)
    qb = _bf(q)
    kb = _bf(k)
    vb = _bf(v)
    lane_head_k = _iota((R, HK), 1) // DK

    Dms, qks, MX = [], [], jnp.zeros((R, 128), f32)
    lane128 = _iota((R, 128), 1)
    for h in range(H):
        Dm = jnp.where(causal, F[:, h:h + 1] - FT[h:h + 1, :] + IT[h:h + 1, :], NEG)
        Dms.append(Dm)
        MX = jnp.where(lane128 == h, jnp.max(Dm, axis=1, keepdims=True), MX)
        qks.append(_dot_nt(jnp.where(lane_head_k == h, qb, jnp.zeros_like(qb)), kb))

    if carry:
        Mprev = jnp.zeros((R, 128), f32)
        m_run = m_scr[...]
        for c in range(nchunk):
            Mprev = jnp.where(row_chunk128 == c, m_run, Mprev)
            last = c * L + L - 1
            m_run = jnp.maximum(F[last:last + 1, :] + m_run, MX[last:last + 1, :])
    else:
        Mprev = jnp.zeros((R, 128), f32)
        for c in range(nchunk):
            Mprev = jnp.where(row_chunk128 == c, m0_ref[c], Mprev)
    Mt = jnp.maximum(F + Mprev, MX)
    Mnew = _seg_last(Mt, rowmod, L)
    Fend = _seg_last(F, rowmod, L)
    Wprev = jnp.exp(F + Mprev - Mt)
    Wsrc = jnp.exp(Fend - F + ig - Mnew)
    Wold = jnp.exp(Fend + Mprev - Mnew)
    Emt = jnp.exp(-Mt)

    kw = k * _expand_heads(Wsrc, H, DK)
    kwT = _bf(kw.T)
    Wold_k = _expand_heads(Wold, H, DK)
    Wold_v = _expand_heads(Wold, H, DV)
    blockdiag = (_iota((HK, HV), 0) // DK) == (_iota((HK, HV), 1) // DV)
    row_chunk = _iota((R, HK), 0) // L
    col_chunk = _iota((HK, R), 1) // L

    qC = jnp.zeros((R, HV), f32)
    Nrows = jnp.zeros((R, HK), f32)
    if carry:
        C = c_scr[...]
        n = n_scr[...]
    for c in range(nchunk):
        if not carry:
            rows_ = [jnp.concatenate([c0_ref[c, h] if g == h else jnp.zeros((DK, DV), f32) for g in range(H)], axis=1)
                     for h in range(H)]
            C = jnp.concatenate(rows_, axis=0)
            n = n0_ref[c]
        in_chunk = row_chunk == c
        qC = qC + _dot(jnp.where(in_chunk, qb, jnp.zeros_like(qb)), _bf(C))
        Nrows = jnp.where(in_chunk, n, Nrows)
        last = c * L + L - 1
        U = _dot(jnp.where(col_chunk == c, kwT, jnp.zeros_like(kwT)), vb)
        C = Wold_v[last:last + 1, :] * C + jnp.where(blockdiag, U, 0.0)
        n = Wold_k[last:last + 1, :] * n + jnp.sum(jnp.where(in_chunk, kw, 0.0), axis=0, keepdims=True)
        if not carry:
            for h in range(H):
                co_ref[c, h] = C[h * DK:(h + 1) * DK, h * DV:(h + 1) * DV]
            no_ref[c] = n
            mo_ref[c] = Mnew[last:last + 1, :]
    if carry:
        c_scr[...] = C
        n_scr[...] = n
        m_scr[...] = Mnew[R - 1:R, :]

        @pl.when(pl.program_id(1) == pl.num_programs(1) - 1)
        def _():
            for h in range(H):
                co_ref[h] = C[h * DK:(h + 1) * DK, h * DV:(h + 1) * DV]
            no_ref[...] = n
            mo_ref[...] = Mnew[R - 1:R, :]

    qn = _dot2b(q * Nrows, _bf(_iota((HK, 128), 1) == _iota((HK, 128), 0) // DK))
    gn = gn_ref[...]
    outs = []
    for h in range(H):
        A = jnp.exp(Dms[h] - Mt[:, h:h + 1]) * qks[h]
        num = Wprev[:, h:h + 1] * qC[:, h * DV:(h + 1) * DV] + _dot(_bf(A), vb[:, h * DV:(h + 1) * DV])
        nq = Wprev[:, h:h + 1] * qn[:, h:h + 1] + jnp.sum(A, axis=1, keepdims=True)
        hh = num / jnp.maximum(jnp.abs(nq), Emt[:, h:h + 1])
        outs.append(_rms(hh, gn[:, h * DV:(h + 1) * DV]))
    y_ref[...] = _bf(_sigmoid(og) * jnp.concatenate(outs, axis=1))


def _mlstm_call(zC, row0, B, T, L, bi, bfg, gn, st):
    H, DK, DV = ML_HEADS, ML_DK, ML_DV
    carry = st is None
    R = 128
    out_shape = [jax.ShapeDtypeStruct((B * T, H * DV), bf16),
                 jax.ShapeDtypeStruct((B, H, DK, DV), f32),
                 jax.ShapeDtypeStruct((B, 1, H * DK), f32),
                 jax.ShapeDtypeStruct((B, 1, 128), f32)]
    rb0 = row0 // R
    if carry:
        nt = T // R
        grid = (B, nt)
        c2 = lambda b, t: (0, 0)
        in_specs = [pl.BlockSpec((R, W_C), lambda b, t: (rb0 + b * nt + t, 0)),
                    pl.BlockSpec((1, 128), c2), pl.BlockSpec((1, 128), c2), pl.BlockSpec((1, H * DV), c2)]
        out_specs = [pl.BlockSpec((R, H * DV), lambda b, t: (b * nt + t, 0)),
                     pl.BlockSpec((None, H, DK, DV), lambda b, t: (b, 0, 0, 0)),
                     pl.BlockSpec((None, 1, H * DK), lambda b, t: (b, 0, 0)),
                     pl.BlockSpec((None, 1, 128), lambda b, t: (b, 0, 0))]
        scratch = [pltpu.VMEM((H * DK, H * DV), f32), pltpu.VMEM((1, H * DK), f32), pltpu.VMEM((1, 128), f32)]
        args = (zC, bi, bfg, gn)
        sem = ("arbitrary", "arbitrary")
    else:
        c0, n0, m0 = st
        nb = (B * T) // R
        cpb = R // L
        grid = (nb,)
        c2 = lambda i: (0, 0)
        in_specs = [pl.BlockSpec((R, W_C), lambda i: (rb0 + i, 0)),
                    pl.BlockSpec((1, 128), c2), pl.BlockSpec((1, 128), c2), pl.BlockSpec((1, H * DV), c2),
                    pl.BlockSpec((cpb, H, DK, DV), lambda i: (i, 0, 0, 0)),
                    pl.BlockSpec((cpb, 1, H * DK), lambda i: (i, 0, 0)),
                    pl.BlockSpec((cpb, 1, 128), lambda i: (i, 0, 0))]
        out_specs = [pl.BlockSpec((R, H * DV), lambda i: (i, 0)),
                     pl.BlockSpec((cpb, H, DK, DV), lambda i: (i, 0, 0, 0)),
                     pl.BlockSpec((cpb, 1, H * DK), lambda i: (i, 0, 0)),
                     pl.BlockSpec((cpb, 1, 128), lambda i: (i, 0, 0))]
        scratch = []
        args = (zC, bi, bfg, gn, c0, n0, m0)
        sem = ("arbitrary",)
    return pl.pallas_call(
        functools.partial(_mlstm_kernel, R=R, L=L, carry=carry),
        grid=grid, in_specs=in_specs, out_specs=out_specs, out_shape=out_shape,
        scratch_shapes=scratch, compiler_params=_cparams(*sem),
    )(*args)


def _sgu_kernel(z_ref, gv_ref, ws_ref, bcol_ref, y_ref, *v_out, nch):
    G = SG_GROUPS
    W = SG_WIDTH
    cw = W // G
    tril = _iota((SG_CHUNK, SG_CHUNK), 1) <= _iota((SG_CHUNK, SG_CHUNK), 0)
    for c in range(nch):
        rs = slice(c * SG_CHUNK, (c + 1) * SG_CHUNK)
        u = z_ref[rs, 0:W]
        vn = _rms(_gelu(z_ref[rs, W:2 * W]), gv_ref[...])
        if v_out:
            v_out[0][rs, :] = vn
        vb = _bf(vn)
        bcol = bcol_ref[...]
        outs = []
        for g in range(G):
            wc = _bf(jnp.where(tril, ws_ref[g], 0.0))
            outs.append(_dot(wc, vb[:, g * cw:(g + 1) * cw]) + bcol[:, g:g + 1])
        y_ref[rs, :] = _bf(_gelu(u) * jnp.concatenate(outs, axis=1))


def _sgu_call(zD, row0, n_rows, gv, ws, bcol, want_v):
    nch = _pick_block(n_rows // SG_CHUNK, (4, 2, 1))
    Rb = nch * SG_CHUNK
    rb0 = row0 // Rb
    out_shape = [jax.ShapeDtypeStruct((n_rows, SG_WIDTH), bf16)]
    out_specs = [pl.BlockSpec((Rb, SG_WIDTH), lambda i: (i, 0))]
    if want_v:
        out_shape.append(jax.ShapeDtypeStruct((n_rows, SG_WIDTH), f32))
        out_specs.append(pl.BlockSpec((Rb, SG_WIDTH), lambda i: (i, 0)))
    return pl.pallas_call(
        functools.partial(_sgu_kernel, nch=nch),
        grid=(n_rows // Rb,),
        in_specs=[pl.BlockSpec((Rb, W_D), lambda i: (rb0 + i, 0)),
                  pl.BlockSpec((1, SG_WIDTH), lambda i: (0, 0)),
                  pl.BlockSpec((SG_GROUPS, SG_CHUNK, SG_CHUNK), lambda i: (0, 0, 0)),
                  pl.BlockSpec((SG_CHUNK, 128), lambda i: (0, 0))],
        out_specs=out_specs, out_shape=out_shape,
        compiler_params=_cparams("arbitrary"),
    )(zD, gv, ws, bcol)


def _mla_prep_kernel(z_ref, cos_ref, sin_ref, gq_ref, gkv_ref, wqn_ref, wqr_ref, wa_ref, pb_ref, rot_ref,
                     q_ref, k_ref):
    z = z_ref[...]
    dq = z[:, 0:MLA_Q_RANK]
    dkv = z[:, MLA_Q_RANK:MLA_Q_RANK + MLA_KV_RANK]
    kr = z[:, MLA_Q_RANK + MLA_KV_RANK:MLA_Q_RANK + MLA_KV_RANK + 128]
    cos = cos_ref[...]
    sin = sin_ref[...]
    rot = rot_ref[...]
    dqn = _bf(_rms(dq, gq_ref[...]))
    qn = _dot(dqn, wqn_ref[...])
    qr = _dot(dqn, wqr_ref[...])
    qrope = qr * cos + _dot2b(qr, rot) * sin
    q_ref[...] = _bf(_dot(_bf(qn), wa_ref[...]) + _dot(_bf(qrope), pb_ref[...]))
    ckv = _rms(dkv, gkv_ref[...])
    krope = kr * cos[:, 0:128] + _dot2b(kr, rot[0:128, 0:128]) * sin[:, 0:128]
    k_ref[...] = jnp.concatenate([ckv, krope], axis=1)


def _mla_prep_call(rows, zB, cos, sin, gq, gkv, wqn, wqr, wa, pb, rot):
    R = rows.R
    HC = MLA_HEADS * 256
    c2 = lambda i: (0, 0)
    return pl.pallas_call(
        _mla_prep_kernel,
        grid=(rows.nb,),
        in_specs=[pl.BlockSpec((R, W_B), lambda i: (i, 0)),
                  pl.BlockSpec((R, 256), lambda i: (i, 0)),
                  pl.BlockSpec((R, 256), lambda i: (i, 0)),
                  pl.BlockSpec((1, MLA_Q_RANK), c2), pl.BlockSpec((1, MLA_KV_RANK), c2),
                  pl.BlockSpec((MLA_Q_RANK, MLA_HEADS * MLA_NOPE), c2),
                  pl.BlockSpec((MLA_Q_RANK, MLA_HEADS * MLA_ROPE), c2),
                  pl.BlockSpec((MLA_HEADS * MLA_NOPE, HC), c2),
                  pl.BlockSpec((MLA_HEADS * MLA_ROPE, HC), c2),
                  pl.BlockSpec((256, 256), c2)],
        out_specs=[pl.BlockSpec((R, HC), lambda i: (i, 0)), pl.BlockSpec((R, 256), lambda i: (i, 0))],
        out_shape=[jax.ShapeDtypeStruct((rows.N, HC), bf16), jax.ShapeDtypeStruct((rows.N, 256), f32)],
        compiler_params=_cparams("arbitrary"),
    )(zB, cos, sin, gq, gkv, wqn, wqr, wa, pb, rot)


def _flash_kernel(q_ref, k_ref, wuv_ref, y_ref, m_scr, l_scr, acc_scr, *, tq, tk):
    qi = pl.program_id(1)
    ki = pl.program_id(2)
    H = MLA_HEADS

    @pl.when(ki == 0)
    def _():
        m_scr[...] = jnp.full(m_scr.shape, NEG, f32)
        l_scr[...] = jnp.zeros(l_scr.shape, f32)
        acc_scr[...] = jnp.zeros(acc_scr.shape, f32)

    @pl.when(ki * tk <= qi * tq + tq - 1)
    def _():
        kb = _bf(k_ref[...])
        vb = kb[:, 0:MLA_KV_RANK]
        visible = (ki * tk + _iota((tq, tk), 1)) <= (qi * tq + _iota((tq, tk), 0))
        for h in range(H):
            s = _dot_nt(q_ref[:, h * 256:(h + 1) * 256], kb) * MLA_SCALE
            s = jnp.where(visible, s, NEG)
            m_prev = m_scr[h]
            m_new = jnp.maximum(m_prev, jnp.max(s, axis=1, keepdims=True))
            alpha = jnp.exp(m_prev - m_new)
            p = jnp.exp(s - m_new)
            l_scr[h] = alpha * l_scr[h] + jnp.sum(p, axis=1, keepdims=True)
            acc_scr[h] = alpha * acc_scr[h] + _dot(_bf(p), vb)
            m_scr[h] = m_new

    @pl.when(ki == pl.num_programs(2) - 1)
    def _():
        y = jnp.zeros((tq, MLA_HEADS * MLA_V), f32)
        for h in range(H):
            y = y + _dot(_bf(acc_scr[h] / l_scr[h]), wuv_ref[h])
        y_ref[...] = _bf(y)


def _flash_call(qcat, kcat, wuv, B, T):
    tq = 128
    tk = _pick_block(T, (512, 256, 128))
    nq, nk = T // tq, T // tk
    H = MLA_HEADS
    return pl.pallas_call(
        functools.partial(_flash_kernel, tq=tq, tk=tk),
        grid=(B, nq, nk),
        in_specs=[pl.BlockSpec((tq, H * 256), lambda b, i, j: (b * nq + i, 0)),
                  pl.BlockSpec((tk, 256), lambda b, i, j: (b * nk + jnp.minimum(j, (i * tq + tq - 1) // tk), 0)),
                  pl.BlockSpec((H, MLA_KV_RANK, H * MLA_V), lambda b, i, j: (0, 0, 0))],
        out_specs=pl.BlockSpec((tq, H * MLA_V), lambda b, i, j: (b * nq + i, 0)),
        out_shape=jax.ShapeDtypeStruct((B * T, H * MLA_V), bf16),
        scratch_shapes=[pltpu.VMEM((H, tq, 1), f32), pltpu.VMEM((H, tq, 1), f32),
                        pltpu.VMEM((H, tq, MLA_KV_RANK), f32)],
        compiler_params=_cparams("arbitrary", "arbitrary", "arbitrary"),
    )(qcat, kcat, wuv)


def _paged_kernel(pt_ref, q_ref, kn_ref, wuv_ref, *refs, P, TS):
    ck_refs = refs[:P]
    kr_refs = refs[P:2 * P]
    y_ref, m_scr, l_scr, acc_scr = refs[2 * P:]
    j = pl.program_id(1)
    H = MLA_HEADS
    NR = TS * H
    q = q_ref[...]
    ql = q[:, 0:MLA_KV_RANK]
    qr = q[:, MLA_KV_RANK:MLA_KV_RANK + MLA_ROPE]

    @pl.when(j == 0)
    def _():
        m_scr[...] = jnp.full(m_scr.shape, NEG, f32)
        l_scr[...] = jnp.zeros(l_scr.shape, f32)
        acc_scr[...] = jnp.zeros(acc_scr.shape, f32)

    cks = [_bf(r[...]) for r in ck_refs]
    ss = [(_dot_nt(ql, cks[p]) + _dot_nt(qr, _bf(kr_refs[p][...]))) * MLA_SCALE for p in range(P)]
    m_prev = m_scr[...]
    m_new = m_prev
    for s in ss:
        m_new = jnp.maximum(m_new, jnp.max(s, axis=1, keepdims=True))
    alpha = jnp.exp(m_prev - m_new)
    l = alpha * l_scr[...]
    acc = alpha * acc_scr[...]
    for p in range(P):
        e = jnp.exp(ss[p] - m_new)
        l = l + jnp.sum(e, axis=1, keepdims=True)
        acc = acc + _dot(_bf(e), cks[p])
    m_scr[...] = m_new
    l_scr[...] = l
    acc_scr[...] = acc

    @pl.when(j == pl.num_programs(1) - 1)
    def _():
        kn = _bf(kn_ref[...])
        s = _dot_nt(q, kn) * MLA_SCALE
        ok = (_iota((NR, 8), 1) <= _iota((NR, 8), 0) // H) & (_iota((NR, 8), 1) < TS)
        s = jnp.where(ok, s, NEG)
        m2 = jnp.maximum(m_new, jnp.max(s, axis=1, keepdims=True))
        a2 = jnp.exp(m_new - m2)
        e = jnp.exp(s - m2)
        l2 = a2 * l + jnp.sum(e, axis=1, keepdims=True)
        acc2 = a2 * acc + _dot(_bf(e), kn[:, 0:MLA_KV_RANK])
        o = acc2 / l2
        row_head = _iota((NR, MLA_KV_RANK), 0) % H
        y = jnp.zeros((NR, H * MLA_V), f32)
        for h in range(H):
            y = y + _dot(_bf(jnp.where(row_head == h, o, 0.0)), wuv_ref[h])
        y_ref[...] = _bf(jnp.sum(y.reshape(TS, H, H * MLA_V), axis=1))


def _paged_call(page_table, q3, kn3, wuv, cache_ckv, cache_kr, l, P):
    DB, n_pages = page_table.shape
    TS = q3.shape[1] // MLA_HEADS
    H = MLA_HEADS
    page = cache_ckv.shape[2]

    def ck_spec(p):
        return pl.BlockSpec((None, None, page, MLA_KV_RANK),
                            lambda b, j, pt: (l, pt[b * n_pages + j * P + p], 0, 0))

    def kr_spec(p):
        return pl.BlockSpec((None, None, page, MLA_ROPE),
                            lambda b, j, pt: (l, pt[b * n_pages + j * P + p], 0, 0))

    grid_spec = pltpu.PrefetchScalarGridSpec(
        num_scalar_prefetch=1,
        grid=(DB, n_pages // P),
        in_specs=[pl.BlockSpec((None, TS * H, 256), lambda b, j, pt: (b, 0, 0)),
                  pl.BlockSpec((None, 8, 256), lambda b, j, pt: (b, 0, 0)),
                  pl.BlockSpec((H, MLA_KV_RANK, H * MLA_V), lambda b, j, pt: (0, 0, 0))]
                 + [ck_spec(p) for p in range(P)] + [kr_spec(p) for p in range(P)],
        out_specs=pl.BlockSpec((None, TS, H * MLA_V), lambda b, j, pt: (b, 0, 0)),
        scratch_shapes=[pltpu.VMEM((TS * H, 1), f32), pltpu.VMEM((TS * H, 1), f32),
                        pltpu.VMEM((TS * H, MLA_KV_RANK), f32)],
    )
    return pl.pallas_call(
        functools.partial(_paged_kernel, P=P, TS=TS),
        grid_spec=grid_spec,
        out_shape=jax.ShapeDtypeStruct((DB, TS, H * MLA_V), bf16),
        compiler_params=_cparams("arbitrary", "arbitrary"),
    )(page_table.reshape(-1), q3, kn3, wuv, *([cache_ckv] * P), *([cache_kr] * P))


def _merge_kernel(h_ref, ya_ref, yb_ref, yc_ref, yd_ref, g0, g1, g2, g3, b0, b1, b2, b3, o_ref):
    h = h_ref[...]
    acc = None
    for y_ref, g_ref, b_ref in ((ya_ref, g0, b0), (yb_ref, g1, b1), (yc_ref, g2, b2), (yd_ref, g3, b3)):
        gate = _sigmoid(_dot(h, _bf(g_ref[...])))
        term = gate * _dot(y_ref[...], _bf(b_ref[...]))
        acc = term if acc is None else acc + term
    o_ref[...] = _bf(acc)


def _merge_call(h1, ys, w_in, w_br, l):
    N = h1.shape[0]
    D = D_MODEL
    tm = _pick_block(N, (1088, 640, 512, 256, 128))
    tn = 256
    ncol = D // tn
    gate_specs = [pl.BlockSpec((None, D, tn), (lambda i, j, br=br: (l, 0, br * ncol + j))) for br in range(N_BRANCH)]
    br_specs = [pl.BlockSpec((None, None, BRANCH_WIDTH, tn), (lambda i, j, br=br: (l, br, 0, j))) for br in range(N_BRANCH)]
    y_specs = [pl.BlockSpec((tm, BRANCH_WIDTH), lambda i, j: (i, 0)) for _ in range(N_BRANCH)]
    return pl.pallas_call(
        _merge_kernel,
        grid=(N // tm, ncol),
        in_specs=[pl.BlockSpec((tm, D), lambda i, j: (i, 0))] + y_specs + gate_specs + br_specs,
        out_specs=pl.BlockSpec((tm, tn), lambda i, j: (i, j)),
        out_shape=jax.ShapeDtypeStruct((N, D), bf16),
        compiler_params=_cparams("arbitrary", "arbitrary"),
    )(h1, *ys, w_in, w_in, w_in, w_in, w_br, w_br, w_br, w_br)


def _outproj_kernel(x_ref, mg_ref, wo_ref, g_ref, g1s, g1p, shs, shp, scs, scp, xo_ref, h_ref, *, rows):
    i = pl.program_id(0)
    x = x_ref[...] + rows.pick(i, g1s, g1p) * _dot(mg_ref[...], wo_ref[...])
    xo_ref[...] = x
    h_ref[...] = _rms(x, g_ref[...]) * (1.0 + rows.pick(i, scs, scp)) + rows.pick(i, shs, shp)


def _outproj_call(rows, x, merged, wo_bf, gain2, mod, l):
    R = rows.R
    D = D_MODEL
    g1s, g1p = rows.mod_specs(l, 2)
    shs, shp = rows.mod_specs(l, 3)
    scs, scp = rows.mod_specs(l, 4)
    return pl.pallas_call(
        functools.partial(_outproj_kernel, rows=rows),
        grid=(rows.nb,),
        in_specs=[pl.BlockSpec((R, D), lambda i: (i, 0)), pl.BlockSpec((R, D), lambda i: (i, 0)),
                  pl.BlockSpec((D, D), lambda i: (0, 0)), pl.BlockSpec((1, D), lambda i: (0, 0)),
                  g1s, g1p, shs, shp, scs, scp],
        out_specs=[pl.BlockSpec((R, D), lambda i: (i, 0)), pl.BlockSpec((R, D), lambda i: (i, 0))],
        out_shape=[jax.ShapeDtypeStruct((rows.N, D), f32), jax.ShapeDtypeStruct((rows.N, D), f32)],
        compiler_params=_cparams("arbitrary"),
    )(x, merged, wo_bf, gain2.reshape(1, D), mod, mod, mod, mod, mod, mod)


def _peer_score_kernel(h_ref, wqh_ref, wql_ref, k1_ref, k2_ref, a_ref, b_ref, g_ref,
                       v1_scr, i1_scr, v2_scr, i2_scr, top_scr, ea_scr, eb_scr):
    T = h_ref.shape[0]
    NK, K = PEER_NKEYS, PEER_TOPK
    hh, hl = _split2(h_ref[...])
    q = _dot(hh, wqh_ref[...]) + (_dot(hh, wql_ref[...]) + _dot(hl, wqh_ref[...]))
    key_idx = _iota((NK, T), 0).astype(f32)
    cand_idx = _iota((K * K, T), 0).astype(f32)
    neg_inf = -jnp.inf

    def topk_into(s, idx, n_rows, val_scr, idx_scr):
        for kk in range(K):
            m = jnp.max(s, axis=0, keepdims=True)
            pos = jnp.min(jnp.where(s == m, idx, float(n_rows)), axis=0, keepdims=True)
            val_scr[pl.ds(kk, 1), :] = m
            idx_scr[pl.ds(kk, 1), :] = pos
            s = jnp.where(idx == pos, neg_inf, s)

    for h in range(PEER_HEADS):
        qh, ql = _split2(q[:, h * PEER_DKEY:(h + 1) * PEER_DKEY])
        for k_ref, val_scr, idx_scr in ((k1_ref, v1_scr, i1_scr), (k2_ref, v2_scr, i2_scr)):
            kh, kl = _split2(k_ref[h])
            sT = _dot_nt(kh, qh) + (_dot_nt(kh, ql) + _dot_nt(kl, qh))
            topk_into(sT, key_idx, NK, val_scr, idx_scr)
        v2 = v2_scr[...]
        i2 = i2_scr[...]
        cand = jnp.concatenate([v1_scr[pl.ds(k1, 1), :] + v2 for k1 in range(K)], axis=0)
        ca = jnp.concatenate([jnp.broadcast_to(i1_scr[pl.ds(k1, 1), :], (K, T)) for k1 in range(K)], axis=0)
        cb = jnp.concatenate([i2] * K, axis=0)
        for kk in range(K):
            m = jnp.max(cand, axis=0, keepdims=True)
            pos = jnp.min(jnp.where(cand == m, cand_idx, float(K * K)), axis=0, keepdims=True)
            sel = cand_idx == pos
            r = h * K + kk
            top_scr[pl.ds(r, 1), :] = m
            ea_scr[pl.ds(r, 1), :] = jnp.sum(jnp.where(sel, ca, 0.0), axis=0, keepdims=True)
            eb_scr[pl.ds(r, 1), :] = jnp.sum(jnp.where(sel, cb, 0.0), axis=0, keepdims=True)
            cand = jnp.where(sel, neg_inf, cand)
        tops = top_scr[pl.ds(h * K, K), :]
        e = jnp.exp(tops - jnp.max(tops, axis=0, keepdims=True))
        top_scr[pl.ds(h * K, K), :] = e / jnp.sum(e, axis=0, keepdims=True)
    g_ref[...] = top_scr[...].T
    a_ref[...] = ea_scr[...].T
    b_ref[...] = eb_scr[...].T


def _peer_score_call(rows, h2, wqh, wql, k1p, k2p):
    R = rows.R
    D = D_MODEL
    HK = PEER_HEADS * PEER_TOPK
    o_spec = pl.BlockSpec((R, HK), lambda i: (i, 0))
    o_shape = jax.ShapeDtypeStruct((rows.N, HK), f32)
    return pl.pallas_call(
        _peer_score_kernel,
        grid=(rows.nb,),
        in_specs=[pl.BlockSpec((R, D), lambda i: (i, 0)),
                  pl.BlockSpec((D, PEER_HEADS * PEER_DKEY), lambda i: (0, 0)),
                  pl.BlockSpec((D, PEER_HEADS * PEER_DKEY), lambda i: (0, 0)),
                  pl.BlockSpec((PEER_HEADS, PEER_NKEYS, PEER_DKEY), lambda i: (0, 0, 0)),
                  pl.BlockSpec((PEER_HEADS, PEER_NKEYS, PEER_DKEY), lambda i: (0, 0, 0))],
        out_specs=[o_spec, o_spec, o_spec],
        out_shape=[o_shape, o_shape, o_shape],
        scratch_shapes=[pltpu.VMEM((PEER_TOPK, R), f32)] * 4 + [pltpu.VMEM((HK, R), f32)] * 3,
        compiler_params=_cparams("arbitrary"),
    )(h2, wqh, wql, k1p, k2p)


def _peer_mix_kernel(x_ref, h_ref, ea_ref, eb_ref, g_ref, ut_ref, v_ref, g2s, g2p, o_ref,
                     a_scr, hb_scr, acc_scr, *, rows, T, nc, ac):
    i = pl.program_id(0)
    j = pl.program_id(1)
    NK = PEER_NKEYS
    pitch = T + PEER_PITCH_PAD

    @pl.when(j == 0)
    def _():
        hb_scr[...] = _bf(h_ref[...])
        acc_scr[...] = jnp.zeros(acc_scr.shape, f32)

    @pl.when(j < nc)
    def _():
        res = _dot(hb_scr[...], ut_ref[...])
        for al in range(ac):
            start = pl.multiple_of((j * ac + al) * pitch, 8)
            a_scr[pl.ds(start, T), :] = res[:, al * NK:(al + 1) * NK]

    @pl.when(j == nc)
    def _():
        sub = _iota((NK, NK), 0).astype(f32)

        def tok(t, carry):
            ia = ea_ref[pl.ds(t, 1), :]
            ib = eb_ref[pl.ds(t, 1), :]
            gg = g_ref[pl.ds(t, 1), :]
            onehot_a = _bf(jnp.where(sub == ia, 1.0, 0.0))
            gate_b = _bf(jnp.where(sub == ib, gg, 0.0))
            w = _dot_nt(onehot_a, gate_b)
            at = a_scr[pl.ds(t, NK, stride=pitch), :]
            a_scr[pl.ds(t, NK, stride=pitch), :] = w * _gelu(at)
            return carry

        lax.fori_loop(0, T, tok, 0)

    @pl.when(j >= nc)
    def _():
        jj = j - nc
        parts = []
        for al in range(ac):
            start = pl.multiple_of((jj * ac + al) * pitch, 8)
            parts.append(_bf(a_scr[pl.ds(start, T), :]))
        acc_scr[...] += _dot(jnp.concatenate(parts, axis=1), v_ref[...])

    @pl.when(j == 2 * nc - 1)
    def _():
        o_ref[...] = x_ref[...] + rows.pick(i, g2s, g2p) * acc_scr[...]


def _peer_mix_call(rows, x, h2, ea, eb, g, ut, vb, mod, l):
    T = rows.R
    D = D_MODEL
    NK = PEER_NKEYS
    ac = 8
    nc = NK // ac
    HK = PEER_HEADS * PEER_TOPK
    g2s_, g2p_ = rows.mod_specs(l, 5)
    R, nPb, nSb, Ns = rows.R, rows.nPb, rows.nSb, rows.Ns
    g2s = pl.BlockSpec((None, R, D), lambda i, j: (l, jnp.clip(i - nPb, 0, nSb - 1), 5))
    g2p = pl.BlockSpec((None, 8, D), lambda i, j: (l, Ns // 8, 5))
    row = lambda i, j: (i, 0)
    return pl.pallas_call(
        functools.partial(_peer_mix_kernel, rows=rows, T=T, nc=nc, ac=ac),
        grid=(rows.nb, 2 * nc),
        in_specs=[pl.BlockSpec((T, D), row), pl.BlockSpec((T, D), row),
                  pl.BlockSpec((T, HK), row), pl.BlockSpec((T, HK), row), pl.BlockSpec((T, HK), row),
                  pl.BlockSpec((D, ac * NK), lambda i, j: (0, jnp.minimum(j, nc - 1))),
                  pl.BlockSpec((ac * NK, D), lambda i, j: (jnp.maximum(j - nc, 0), 0)),
                  g2s, g2p],
        out_specs=pl.BlockSpec((T, D), row),
        out_shape=jax.ShapeDtypeStruct((rows.N, D), f32),
        scratch_shapes=[pltpu.VMEM((NK * (T + PEER_PITCH_PAD), NK), f32),
                        pltpu.VMEM((T, D), bf16), pltpu.VMEM((T, D), f32)],
        compiler_params=_cparams("arbitrary", "arbitrary"),
    )(x, h2, ea, eb, g, ut, vb, mod, mod)


def _final_kernel(x_ref, g_ref, o_ref):
    o_ref[...] = _rms(x_ref[...], g_ref[...])


def _final_call(rows, x, gain):
    R, D = rows.R, D_MODEL
    return pl.pallas_call(
        _final_kernel, grid=(rows.nb,),
        in_specs=[pl.BlockSpec((R, D), lambda i: (i, 0)), pl.BlockSpec((1, D), lambda i: (0, 0))],
        out_specs=pl.BlockSpec((R, D), lambda i: (i, 0)),
        out_shape=jax.ShapeDtypeStruct((rows.N, D), f32),
        compiler_params=_cparams("arbitrary"),
    )(x, gain.reshape(1, D))


def _mla_constants():
    H, R = MLA_HEADS, MLA_ROPE
    half = R // 2
    rot = np.zeros((H * R, H * R), np.float32)
    for g in range(H):
        for jj in range(half):
            rot[g * R + half + jj, g * R + jj] = -1.0
            rot[g * R + jj, g * R + half + jj] = 1.0
    place = np.zeros((H * R, H * 256), np.float32)
    for g in range(H):
        for jj in range(R):
            place[g * R + jj, g * 256 + MLA_KV_RANK + jj] = 1.0
    return jnp.asarray(rot, bf16), jnp.asarray(place, bf16)


def kernel(x_prompt, x_sample, cache_mla_ckv, cache_mla_kr, state_gla, state_mlstm_C, state_mlstm_n, state_mlstm_m, page_table, c_prompt, c_sample, w_ada, b_ada, norm1, norm2, w_in, gla_wa2, gla_ba, gla_norm, mla_w_uq, mla_gq, mla_gkv, mla_w_uk, mla_w_uv, ml_bi, ml_bf, ml_norm, sg_gv, sg_ws, sg_bs, w_br, w_o, peer_wq, peer_k1, peer_k2, peer_u, peer_v, final_norm):
    D = D_MODEL
    B, T, _ = x_prompt.shape
    DB, TS, _ = x_sample.shape
    depth = w_ada.shape[0]
    n_pages = page_table.shape[1]
    page = cache_mla_ckv.shape[2]
    past_len = n_pages * page
    rows = _Rows(B, T, DB, TS)
    Np, Ns = rows.Np, rows.Ns
    assert Ns % 128 == 0 and T % 128 == 0 and TS == 4

    c_rows = jnp.concatenate([jnp.repeat(c_sample, TS, axis=0), c_prompt,
                              jnp.zeros((16 - B, D), f32)], axis=0)
    mod = _ada_call(c_rows, w_ada, b_ada)

    x = jnp.concatenate([x_prompt.reshape(Np, D), x_sample.reshape(Ns, D)], axis=0)

    inv = ROPE_THETA ** (-jnp.arange(MLA_ROPE // 2, dtype=f32) / (MLA_ROPE // 2))
    pos_p = jnp.arange(T, dtype=jnp.int32)
    pos_s = past_len + jnp.arange(TS, dtype=jnp.int32)

    def table(fn):
        tp = fn(pos_p.astype(f32)[:, None] * inv)
        ts = fn(pos_s.astype(f32)[:, None] * inv)
        rows_ = jnp.concatenate([jnp.tile(tp, (B, 1)), jnp.tile(ts, (DB, 1))], axis=0)
        return jnp.tile(rows_, (1, 2 * MLA_HEADS))

    cos_t, sin_t = table(jnp.cos), table(jnp.sin)
    rot, place = _mla_constants()
    eye_h = jnp.eye(MLA_HEADS, dtype=f32)
    Ls = math.gcd(TS, GLA_CHUNK)
    Lp_gla = math.gcd(T, GLA_CHUNK)
    Lp_ml = math.gcd(T, ML_CHUNK)
    Ls_ml = math.gcd(TS, ML_CHUNK)
    P = _pick_block(n_pages, (16, 8, 4, 2, 1))
    spb = SG_CHUNK // TS

    new_p, new_s = [], []
    for l in range(depth):
        wl = w_in[l]
        g0 = N_BRANCH * D
        zc = lambda n: jnp.zeros((D, n), f32)
        wA = _bf(jnp.concatenate([wl[:, g0:g0 + 1552], zc(112)], axis=1))
        wB = _bf(jnp.concatenate([wl[:, g0 + 1552:g0 + 2096], zc(96)], axis=1))
        wC = _bf(jnp.concatenate([wl[:, g0 + 2096:g0 + 3632], wl[:, g0 + 3632:g0 + 3636], zc(124),
                                  wl[:, g0 + 3636:g0 + 3640], zc(124)], axis=1))
        wD = _bf(wl[:, g0 + 3640:g0 + 4664])
        wa2p = jnp.concatenate([gla_wa2[l], jnp.zeros((128 - GLA_RANK, GLA_HEADS * GLA_DK), f32)], axis=0)
        uq = mla_w_uq[l].reshape(MLA_Q_RANK, MLA_HEADS, MLA_NOPE + MLA_ROPE)
        wqn = _bf(uq[:, :, :MLA_NOPE].reshape(MLA_Q_RANK, MLA_HEADS * MLA_NOPE))
        wqr = _bf(uq[:, :, MLA_NOPE:].reshape(MLA_Q_RANK, MLA_HEADS * MLA_ROPE))
        uk = jnp.transpose(mla_w_uk[l], (1, 2, 0))
        wa = jnp.einsum('hnr,hg->hngr', uk, eye_h)
        wa = _bf(jnp.pad(wa, ((0, 0), (0, 0), (0, 0), (0, 256 - MLA_KV_RANK))).reshape(MLA_HEADS * MLA_NOPE, MLA_HEADS * 256))
        wuv = _bf(jnp.einsum('rhv,hg->hrgv', mla_w_uv[l], eye_h).reshape(MLA_HEADS, MLA_KV_RANK, MLA_HEADS * MLA_V))
        pad128 = lambda a: jnp.pad(a, (0, 128 - a.shape[0])).reshape(1, 128)
        bcol_p = jnp.pad(sg_bs[l].T, ((0, 0), (0, 128 - SG_GROUPS)))
        ws_s = jnp.stack([jnp.kron(jnp.eye(spb, dtype=f32), sg_ws[l, g, :TS, :TS]) for g in range(SG_GROUPS)])
        bcol_s = jnp.pad(jnp.tile(sg_bs[l][:, :TS].T, (spb, 1)), ((0, 0), (0, 128 - SG_GROUPS)))
        wo_bf = _bf(w_o[l])
        wq = peer_wq[l]
        wqh = _bf(wq)
        wql = _bf(wq - wqh.astype(f32))
        half = PEER_DKEY // 2
        k1p = jnp.pad(peer_k1[l], ((0, 0), (0, 0), (0, half)))
        k2p = jnp.pad(peer_k2[l], ((0, 0), (0, 0), (half, 0)))
        ut = _bf(peer_u[l]).T
        vb = _bf(peer_v[l])

        h1 = _modnorm_call(rows, x, norm1[l], mod, l, 0, 1)
        zA, zB, zC, zD = (_mm_call(h1, w) for w in (wA, wB, wC, wD))

        gn_a = gla_norm[l].reshape(1, -1)
        ba = gla_ba[l].reshape(1, -1)
        ya_p, S_p = _gla_call(zA, 0, B, T, Lp_gla, wa2p, ba, gn_a, None)
        ya_s, S_s = _gla_call(zA, Np, DB, TS, Ls, wa2p, ba, gn_a, state_gla[l])

        qcat, kcat = _mla_prep_call(rows, zB, cos_t, sin_t, mla_gq[l].reshape(1, -1), mla_gkv[l].reshape(1, -1),
                                    wqn, wqr, wa, place, rot)
        yb_p = _flash_call(qcat, kcat, wuv, B, T)
        q3 = qcat[Np:].reshape(DB, TS * MLA_HEADS, 256)
        kn3 = jnp.pad(kcat[Np:].reshape(DB, TS, 256), ((0, 0), (0, 8 - TS), (0, 0)))
        yb_s = _paged_call(page_table, q3, kn3, wuv, cache_mla_ckv, cache_mla_kr, l, P).reshape(Ns, -1)

        bi, bfg, gn_c = pad128(ml_bi[l]), pad128(ml_bf[l]), ml_norm[l].reshape(1, -1)
        yc_p, C_p, n_p, m_p = _mlstm_call(zC, 0, B, T, Lp_ml, bi, bfg, gn_c, None)
        st = (state_mlstm_C[l], state_mlstm_n[l].reshape(DB, 1, -1),
              jnp.pad(state_mlstm_m[l], ((0, 0), (0, 128 - ML_HEADS))).reshape(DB, 1, 128))
        yc_s, C_s, n_s, m_s = _mlstm_call(zC, Np, DB, TS, Ls_ml, bi, bfg, gn_c, st)

        gv = sg_gv[l].reshape(1, -1)
        (yd_p,) = _sgu_call(zD, 0, Np, gv, sg_ws[l], bcol_p, False)
        yd_s, v_s = _sgu_call(zD, Np, Ns, gv, ws_s, bcol_s, True)

        ys = [jnp.concatenate([p_, s_], axis=0) for p_, s_ in ((ya_p, ya_s), (yb_p, yb_s), (yc_p, yc_s), (yd_p, yd_s))]
        merged = _merge_call(h1, ys, w_in, w_br, l)
        x, h2 = _outproj_call(rows, x, merged, wo_bf, norm2[l], mod, l)

        ea, eb, g = _peer_score_call(rows, h2, wqh, wql, k1p, k2p)
        x = _peer_mix_call(rows, x, h2, ea, eb, g, ut, vb, mod, l)

        new_p.append(dict(ckv=kcat[:Np, :MLA_KV_RANK].reshape(B, T, -1),
                          kr=kcat[:Np, MLA_KV_RANK:MLA_KV_RANK + MLA_ROPE].reshape(B, T, -1),
                          S=S_p, C=C_p, n=n_p.reshape(B, ML_HEADS, ML_DK), m=m_p[:, 0, :ML_HEADS]))
        new_s.append(dict(ckv=kcat[Np:, :MLA_KV_RANK].reshape(DB, TS, -1),
                          kr=kcat[Np:, MLA_KV_RANK:MLA_KV_RANK + MLA_ROPE].reshape(DB, TS, -1),
                          S=S_s, C=C_s, n=n_s.reshape(DB, ML_HEADS, ML_DK), m=m_s[:, 0, :ML_HEADS],
                          v=v_s.reshape(DB, TS, -1)))

    y = _final_call(rows, x, final_norm)
    stk = lambda outs, name: jnp.stack([o[name] for o in outs], axis=0)
    return (y[:Np].reshape(B, T, D), y[Np:].reshape(DB, TS, D),
            stk(new_p, 'ckv'), stk(new_p, 'kr'), stk(new_p, 'S'), stk(new_p, 'C'), stk(new_p, 'n'), stk(new_p, 'm'),
            stk(new_s, 'ckv'), stk(new_s, 'kr'), stk(new_s, 'S'), stk(new_s, 'C'), stk(new_s, 'n'), stk(new_s, 'm'),
            stk(new_s, 'v'))
```

```python
import functools
import math

import numpy as np
import jax
import jax.numpy as jnp
from jax import lax
from jax.experimental import pallas as pl
from jax.experimental.pallas import tpu as pltpu

f32 = jnp.float32
bf16 = jnp.bfloat16

D_MODEL = 2048
EPS = 1e-6
ADA_CHUNKS = 6
GLA_HEADS, GLA_DK, GLA_DV, GLA_RANK, GLA_TAU, GLA_CHUNK = 4, 64, 128, 16, 16.0, 16
MLA_HEADS, MLA_Q_RANK, MLA_KV_RANK, MLA_NOPE, MLA_ROPE, MLA_V = 8, 384, 128, 64, 32, 64
MLA_SCALE = (MLA_NOPE + MLA_ROPE) ** -0.5
ROPE_THETA = 10000.0
ML_HEADS, ML_DK, ML_DV, ML_CHUNK = 4, 64, 128, 64
SG_GROUPS, SG_CHUNK, SG_WIDTH = 4, 128, 512
PEER_HEADS, PEER_NKEYS, PEER_DKEY, PEER_TOPK = 8, 128, 128, 16
PEER_EXPERTS = PEER_NKEYS * PEER_NKEYS
BRANCH_WIDTH = 512
N_BRANCH = 4

W_A = 1664
W_B = 640
W_C = 1792
W_D = 1024

NEG = -1e30
VMEM_LIMIT_V7X = 56 * 1024 * 1024
PEER_PITCH_PAD = 8
PEER_TOKEN_UNROLL = 8

def _cparams(*sem):
    return pltpu.CompilerParams(dimension_semantics=sem, vmem_limit_bytes=VMEM_LIMIT_V7X)


def _bf(x):
    return x.astype(bf16)


def _dot(a, b):
    return jnp.dot(a, b, preferred_element_type=f32)


def _dot_nt(a, b):
    return lax.dot_general(a, b, (((1,), (1,)), ((), ())), preferred_element_type=f32)


def _split2(x):
    hi = _bf(x)
    lo = _bf(x - hi.astype(f32))
    return hi, lo


def _dot3(a, b):
    ah, al = _split2(a)
    bh, bl = _split2(b)
    return _dot(ah, bh) + (_dot(ah, bl) + _dot(al, bh))


def _dot2b(a, b_bf):
    ah, al = _split2(a)
    return _dot(ah, b_bf) + _dot(al, b_bf)


def _sigmoid(x):
    return 1.0 / (1.0 + jnp.exp(-x))


def _log_sigmoid(x):
    return jnp.minimum(x, 0.0) - jnp.log1p(jnp.exp(-jnp.abs(x)))


def _gelu(x):
    return jax.nn.gelu(x)


def _rms(x, g):
    return x * lax.rsqrt(jnp.mean(x * x, axis=-1, keepdims=True) + EPS) * g


def _iota(shape, dim):
    return lax.broadcasted_iota(jnp.int32, shape, dim)


def _seg_cumsum(x, rowmod, L):
    s = 1
    while s < L:
        x = x + jnp.where(rowmod >= s, pltpu.roll(x, s, 0), 0.0)
        s *= 2
    return x


def _seg_last(x, rowmod, L):
    R = x.shape[0]
    s = 1
    while s < L:
        x = jnp.where(rowmod + s < L, pltpu.roll(x, R - s, 0), x)
        s *= 2
    return x


def _expand_heads(x, n_heads, width):
    R = x.shape[0]
    lane_head = _iota((R, n_heads * width), 1) // width
    out = jnp.zeros((R, n_heads * width), f32)
    for h in range(n_heads):
        out = jnp.where(lane_head == h, x[:, h:h + 1], out)
    return out


def _pick_block(n, cands):
    for c in cands:
        if n % c == 0:
            return c
    raise ValueError(f"no block size in {cands} divides {n}")


def _ada_kernel(c_ref, w_ref, b_ref, o_ref):
    c = c_ref[...]
    a = _bf(c * _sigmoid(c))
    o_ref[...] = _dot(a, _bf(w_ref[...])) + b_ref[...]


def _ada_call(c_rows, w_ada, b_ada):
    L, D, N6 = w_ada.shape
    Rc = c_rows.shape[0]
    tn = 1024
    return pl.pallas_call(
        _ada_kernel, name="ada_mod",
        grid=(L, N6 // tn),
        in_specs=[pl.BlockSpec((Rc, D), lambda l, j: (0, 0)),
                  pl.BlockSpec((None, D, tn), lambda l, j: (l, 0, j)),
                  pl.BlockSpec((None, 1, tn), lambda l, j: (l, 0, j))],
        out_specs=pl.BlockSpec((None, Rc, tn), lambda l, j: (l, 0, j)),
        out_shape=jax.ShapeDtypeStruct((L, Rc, N6), f32),
        compiler_params=_cparams("arbitrary", "arbitrary"),
    )(c_rows, w_ada, b_ada.reshape(L, 1, N6))


class _Rows:
    def __init__(self, B, T, DB, TS):
        self.B, self.T, self.DB, self.TS = B, T, DB, TS
        self.Np, self.Ns = B * T, DB * TS
        self.N = self.Np + self.Ns
        self.R = _pick_block(math.gcd(T, self.Ns), (256, 128))
        self.nPb = self.Np // self.R
        self.nSb = self.Ns // self.R
        self.nb = self.nPb + self.nSb
        self.blocks_per_seq = T // self.R

    def mod_specs(self, l, chunk):
        R, nPb, nSb, Ns = self.R, self.nPb, self.nSb, self.Ns
        s = pl.BlockSpec((None, R, D_MODEL), lambda i: (l, jnp.clip(i - nPb, 0, nSb - 1), chunk))
        p = pl.BlockSpec((None, 8, D_MODEL), lambda i: (l, Ns // 8, chunk))
        return s, p

    def pick(self, i, s_ref, p_ref):
        b = jnp.minimum(i // self.blocks_per_seq, self.B - 1)
        return jnp.where(i >= self.nPb, s_ref[...], p_ref[pl.ds(b, 1), :])


def _modnorm_kernel(x_ref, g_ref, shs_ref, shp_ref, scs_ref, scp_ref, h_ref, *, rows):
    i = pl.program_id(0)
    sh = rows.pick(i, shs_ref, shp_ref)
    sc = rows.pick(i, scs_ref, scp_ref)
    h_ref[...] = _bf(_rms(x_ref[...], g_ref[...]) * (1.0 + sc) + sh)


def _modnorm_call(rows, x, gain, mod, l, sh_chunk, sc_chunk):
    R = rows.R
    shs, shp = rows.mod_specs(l, sh_chunk)
    scs, scp = rows.mod_specs(l, sc_chunk)
    return pl.pallas_call(
        functools.partial(_modnorm_kernel, rows=rows), name="modnorm",
        grid=(rows.nb,),
        in_specs=[pl.BlockSpec((R, D_MODEL), lambda i: (i, 0)),
                  pl.BlockSpec((1, D_MODEL), lambda i: (0, 0)), shs, shp, scs, scp],
        out_specs=pl.BlockSpec((R, D_MODEL), lambda i: (i, 0)),
        out_shape=jax.ShapeDtypeStruct((rows.N, D_MODEL), bf16),
        compiler_params=_cparams("arbitrary"),
    )(x, gain.reshape(1, D_MODEL), mod, mod, mod, mod)


def _mm_kernel(x_ref, w_ref, o_ref):
    o_ref[...] = _dot(x_ref[...], w_ref[...]).astype(o_ref.dtype)


def _mm_call(x, w, out_dtype=f32):
    M, K = x.shape
    N = w.shape[1]
    tm = _pick_block(M, (1088, 1024, 640, 512, 256, 128))
    return pl.pallas_call(
        _mm_kernel, name="mixer_proj",
        grid=(M // tm,),
        in_specs=[pl.BlockSpec((tm, K), lambda i: (i, 0)),
                  pl.BlockSpec((K, N), lambda i: (0, 0))],
        out_specs=pl.BlockSpec((tm, N), lambda i: (i, 0)),
        out_shape=jax.ShapeDtypeStruct((M, N), out_dtype),
        compiler_params=_cparams("arbitrary"),
    )(x, w)


def _gla_kernel(*refs, R, L, carry):
    if carry:
        z_ref, wa2_ref, ba_ref, gn_ref, y_ref, so_ref, s_scr = refs
    else:
        z_ref, wa2_ref, ba_ref, gn_ref, s0_ref, y_ref, so_ref = refs
    H, DK, DV = GLA_HEADS, GLA_DK, GLA_DV
    HK, HV = H * DK, H * DV
    nchunk = R // L
    z = z_ref[...]
    q = z[:, 0:HK] * (DK ** -0.5)
    k = z[:, HK:2 * HK]
    v = z[:, 2 * HK:2 * HK + HV]
    r = z[:, 2 * HK + HV:2 * HK + 2 * HV]
    ag = z[:, 2 * HK + 2 * HV:2 * HK + 2 * HV + 128]
    la = _log_sigmoid(_dot3(ag, wa2_ref[...]) + ba_ref[...]) * (1.0 / GLA_TAU)
    rowmod = _iota((R, HK), 0) % L
    b = _seg_cumsum(la, rowmod, L)
    tot = _seg_last(b, rowmod, L)

    sel_row_head = _iota((HK, 128), 0) // DK
    sel_lane = _iota((HK, 128), 1)
    att = jnp.zeros((R, 128), f32)
    for d in range(L):
        ks = k if d == 0 else pltpu.roll(k, d, 0)
        bs = b if d == 0 else pltpu.roll(b, d, 0)
        p = q * ks * jnp.exp(jnp.where(rowmod >= d, b - bs, NEG))
        att = att + _dot2b(p, _bf(sel_lane == d * H + sel_row_head))
    ex_row = _iota((128, HV), 0)
    ex_lane_head = _iota((128, HV), 1) // DV
    o = jnp.zeros((R, HV), f32)
    for d in range(L):
        vs = v if d == 0 else pltpu.roll(v, d, 0)
        o = o + _dot2b(att, _bf(ex_row == d * H + ex_lane_head)) * vs

    qt = _bf(q * jnp.exp(b))
    kd = k * jnp.exp(tot - b)
    kdT = _bf(kd.T)
    totT = tot.T
    vb = _bf(v)
    blockdiag = (_iota((HK, HV), 0) // DK) == (_iota((HK, HV), 1) // DV)
    row_chunk = _iota((R, HK), 0) // L
    col_chunk = _iota((HK, R), 1) // L
    if carry:
        @pl.when(pl.program_id(1) == 0)
        def _():
            s_scr[...] = jnp.zeros((HK, HV), f32)
        S = s_scr[...]
    for c in range(nchunk):
        if not carry:
            rows_ = [jnp.concatenate([s0_ref[c, h] if g == h else jnp.zeros((DK, DV), f32) for g in range(H)], axis=1)
                     for h in range(H)]
            S = jnp.concatenate(rows_, axis=0)
        o = o + _dot(jnp.where(row_chunk == c, qt, jnp.zeros_like(qt)), _bf(S))
        U = _dot(jnp.where(col_chunk == c, kdT, jnp.zeros_like(kdT)), vb)
        dcol = jnp.exp(totT[:, c * L:c * L + 1])
        S = dcol * S + jnp.where(blockdiag, U, 0.0)
        if not carry:
            for h in range(H):
                so_ref[c, h] = S[h * DK:(h + 1) * DK, h * DV:(h + 1) * DV]
    if carry:
        s_scr[...] = S

        @pl.when(pl.program_id(1) == pl.num_programs(1) - 1)
        def _():
            for h in range(H):
                so_ref[h] = S[h * DK:(h + 1) * DK, h * DV:(h + 1) * DV]

    gn = gn_ref[...]
    outs = []
    for h in range(H):
        oh = o[:, h * DV:(h + 1) * DV]
        outs.append(_rms(oh, gn[:, h * DV:(h + 1) * DV]))
    y_ref[...] = _bf(jnp.concatenate(outs, axis=1) * (r * _sigmoid(r)))


def _gla_call(zA, row0, B, T, L, wa2p, ba, gn, s0):
    H, DK, DV = GLA_HEADS, GLA_DK, GLA_DV
    carry = s0 is None
    if carry:
        R = _pick_block(T, (256, 128))
        nt = T // R
        grid = (B, nt)
        rb0 = row0 // R
        in_specs = [pl.BlockSpec((R, W_A), lambda b, t: (rb0 + b * nt + t, 0)),
                    pl.BlockSpec((128, H * DK), lambda b, t: (0, 0)),
                    pl.BlockSpec((1, H * DK), lambda b, t: (0, 0)),
                    pl.BlockSpec((1, H * DV), lambda b, t: (0, 0))]
        out_specs = [pl.BlockSpec((R, H * DV), lambda b, t: (b * nt + t, 0)),
                     pl.BlockSpec((None, H, DK, DV), lambda b, t: (b, 0, 0, 0))]
        scratch = [pltpu.VMEM((H * DK, H * DV), f32)]
        args = (zA, wa2p, ba, gn)
        sem = ("arbitrary", "arbitrary")
    else:
        R = 128
        nb = (B * T) // R
        cpb = R // L
        grid = (nb,)
        rb0 = row0 // R
        in_specs = [pl.BlockSpec((R, W_A), lambda i: (rb0 + i, 0)),
                    pl.BlockSpec((128, H * DK), lambda i: (0, 0)),
                    pl.BlockSpec((1, H * DK), lambda i: (0, 0)),
                    pl.BlockSpec((1, H * DV), lambda i: (0, 0)),
                    pl.BlockSpec((cpb, H, DK, DV), lambda i: (i, 0, 0, 0))]
        out_specs = [pl.BlockSpec((R, H * DV), lambda i: (i, 0)),
                     pl.BlockSpec((cpb, H, DK, DV), lambda i: (i, 0, 0, 0))]
        scratch = []
        args = (zA, wa2p, ba, gn, s0)
        sem = ("arbitrary",)
    return pl.pallas_call(
        functools.partial(_gla_kernel, R=R, L=L, carry=carry), name="gla_prompt" if carry else "gla_sample",
        grid=grid, in_specs=in_specs, out_specs=out_specs,
        out_shape=[jax.ShapeDtypeStruct((B * T, H * DV), bf16),
                   jax.ShapeDtypeStruct((B, H, DK, DV), f32)],
        scratch_shapes=scratch,
        compiler_params=_cparams(*sem),
    )(*args)


def _mlstm_kernel(*refs, R, L, carry):
    if carry:
        z_ref, bi_ref, bf_ref, gn_ref, y_ref, co_ref, no_ref, mo_ref, c_scr, n_scr, m_scr = refs
    else:
        z_ref, bi_ref, bf_ref, gn_ref, c0_ref, n0_ref, m0_ref, y_ref, co_ref, no_ref, mo_ref = refs
    H, DK, DV = ML_HEADS, ML_DK, ML_DV
    HK, HV = H * DK, H * DV
    nchunk = R // L
    z = z_ref[...]
    q = z[:, 0:HK]
    k = z[:, HK:2 * HK] * (DK ** -0.5)
    v = z[:, 2 * HK:2 * HK + HV]
    og = z[:, 2 * HK + HV:2 * HK + 2 * HV]
    ig = z[:, 2 * HK + 2 * HV:2 * HK + 2 * HV + 128] + bi_ref[...]
    fg = z[:, 2 * HK + 2 * HV + 128:2 * HK + 2 * HV + 256] + bf_ref[...]
    rowmod = _iota((R, 128), 0) % L
    row_chunk128 = _iota((R, 128), 0) // L
    F = _seg_cumsum(_log_sigmoid(fg), rowmod, L)
    FT = F.T
    IT = ig.T

    if carry:
        @pl.when(pl.program_id(1) == 0)
        def _():
            c_scr[...] = jnp.zeros((HK, HV), f32)
            n_scr[...] = jnp.zeros((1, HK), f32)
            m_scr[...] = jnp.zeros((1, 128), f32)

    tpos = _iota((R, R), 0)
    spos = _iota((R, R), 1)
    causal = (tpos // L == spos // L) & (spos <= tpos)
    qb = _bf(q)
    kb = _bf(k)
    vb = _bf(v)
    lane_head_k = _iota((R, HK), 1) // DK

    Dms, qks, MX = [], [], jnp.zeros((R, 128), f32)
    lane128 = _iota((R, 128), 1)
    for h in range(H):
        Dm = jnp.where(causal, F[:, h:h + 1] - FT[h:h + 1, :] + IT[h:h + 1, :], NEG)
        Dms.append(Dm)
        MX = jnp.where(lane128 == h, jnp.max(Dm, axis=1, keepdims=True), MX)
        qks.append(_dot_nt(jnp.where(lane_head_k == h, qb, jnp.zeros_like(qb)), kb))

    if carry:
        Mprev = jnp.zeros((R, 128), f32)
        m_run = m_scr[...]
        for c in range(nchunk):
            Mprev = jnp.where(row_chunk128 == c, m_run, Mprev)
            last = c * L + L - 1
            m_run = jnp.maximum(F[last:last + 1, :] + m_run, MX[last:last + 1, :])
    else:
        Mprev = jnp.zeros((R, 128), f32)
        for c in range(nchunk):
            Mprev = jnp.where(row_chunk128 == c, m0_ref[c], Mprev)
    Mt = jnp.maximum(F + Mprev, MX)
    Mnew = _seg_last(Mt, rowmod, L)
    Fend = _seg_last(F, rowmod, L)
    Wprev = jnp.exp(F + Mprev - Mt)
    Wsrc = jnp.exp(Fend - F + ig - Mnew)
    Wold = jnp.exp(Fend + Mprev - Mnew)
    Emt = jnp.exp(-Mt)

    kw = k * _expand_heads(Wsrc, H, DK)
    kwT = _bf(kw.T)
    Wold_k = _expand_heads(Wold, H, DK)
    Wold_v = _expand_heads(Wold, H, DV)
    blockdiag = (_iota((HK, HV), 0) // DK) == (_iota((HK, HV), 1) // DV)
    row_chunk = _iota((R, HK), 0) // L
    col_chunk = _iota((HK, R), 1) // L

    qC = jnp.zeros((R, HV), f32)
    Nrows = jnp.zeros((R, HK), f32)
    if carry:
        C = c_scr[...]
        n = n_scr[...]
    for c in range(nchunk):
        if not carry:
            rows_ = [jnp.concatenate([c0_ref[c, h] if g == h else jnp.zeros((DK, DV), f32) for g in range(H)], axis=1)
                     for h in range(H)]
            C = jnp.concatenate(rows_, axis=0)
            n = n0_ref[c]
        in_chunk = row_chunk == c
        qC = qC + _dot(jnp.where(in_chunk, qb, jnp.zeros_like(qb)), _bf(C))
        Nrows = jnp.where(in_chunk, n, Nrows)
        last = c * L + L - 1
        U = _dot(jnp.where(col_chunk == c, kwT, jnp.zeros_like(kwT)), vb)
        C = Wold_v[last:last + 1, :] * C + jnp.where(blockdiag, U, 0.0)
        n = Wold_k[last:last + 1, :] * n + jnp.sum(jnp.where(in_chunk, kw, 0.0), axis=0, keepdims=True)
        if not carry:
            for h in range(H):
                co_ref[c, h] = C[h * DK:(h + 1) * DK, h * DV:(h + 1) * DV]
            no_ref[c] = n
            mo_ref[c] = Mnew[last:last + 1, :]
    if carry:
        c_scr[...] = C
        n_scr[...] = n
        m_scr[...] = Mnew[R - 1:R, :]

        @pl.when(pl.program_id(1) == pl.num_programs(1) - 1)
        def _():
            for h in range(H):
                co_ref[h] = C[h * DK:(h + 1) * DK, h * DV:(h + 1) * DV]
            no_ref[...] = n
            mo_ref[...] = Mnew[R - 1:R, :]

    qn = _dot2b(q * Nrows, _bf(_iota((HK, 128), 1) == _iota((HK, 128), 0) // DK))
    gn = gn_ref[...]
    outs = []
    for h in range(H):
        A = jnp.exp(Dms[h] - Mt[:, h:h + 1]) * qks[h]
        num = Wprev[:, h:h + 1] * qC[:, h * DV:(h + 1) * DV] + _dot(_bf(A), vb[:, h * DV:(h + 1) * DV])
        nq = Wprev[:, h:h + 1] * qn[:, h:h + 1] + jnp.sum(A, axis=1, keepdims=True)
        hh = num / jnp.maximum(jnp.abs(nq), Emt[:, h:h + 1])
        outs.append(_rms(hh, gn[:, h * DV:(h + 1) * DV]))
    y_ref[...] = _bf(_sigmoid(og) * jnp.concatenate(outs, axis=1))


def _mlstm_call(zC, row0, B, T, L, bi, bfg, gn, st):
    H, DK, DV = ML_HEADS, ML_DK, ML_DV
    carry = st is None
    R = 128
    out_shape = [jax.ShapeDtypeStruct((B * T, H * DV), bf16),
                 jax.ShapeDtypeStruct((B, H, DK, DV), f32),
                 jax.ShapeDtypeStruct((B, 1, H * DK), f32),
                 jax.ShapeDtypeStruct((B, 1, 128), f32)]
    rb0 = row0 // R
    if carry:
        nt = T // R
        grid = (B, nt)
        c2 = lambda b, t: (0, 0)
        in_specs = [pl.BlockSpec((R, W_C), lambda b, t: (rb0 + b * nt + t, 0)),
                    pl.BlockSpec((1, 128), c2), pl.BlockSpec((1, 128), c2), pl.BlockSpec((1, H * DV), c2)]
        out_specs = [pl.BlockSpec((R, H * DV), lambda b, t: (b * nt + t, 0)),
                     pl.BlockSpec((None, H, DK, DV), lambda b, t: (b, 0, 0, 0)),
                     pl.BlockSpec((None, 1, H * DK), lambda b, t: (b, 0, 0)),
                     pl.BlockSpec((None, 1, 128), lambda b, t: (b, 0, 0))]
        scratch = [pltpu.VMEM((H * DK, H * DV), f32), pltpu.VMEM((1, H * DK), f32), pltpu.VMEM((1, 128), f32)]
        args = (zC, bi, bfg, gn)
        sem = ("arbitrary", "arbitrary")
    else:
        c0, n0, m0 = st
        nb = (B * T) // R
        cpb = R // L
        grid = (nb,)
        c2 = lambda i: (0, 0)
        in_specs = [pl.BlockSpec((R, W_C), lambda i: (rb0 + i, 0)),
                    pl.BlockSpec((1, 128), c2), pl.BlockSpec((1, 128), c2), pl.BlockSpec((1, H * DV), c2),
                    pl.BlockSpec((cpb, H, DK, DV), lambda i: (i, 0, 0, 0)),
                    pl.BlockSpec((cpb, 1, H * DK), lambda i: (i, 0, 0)),
                    pl.BlockSpec((cpb, 1, 128), lambda i: (i, 0, 0))]
        out_specs = [pl.BlockSpec((R, H * DV), lambda i: (i, 0)),
                     pl.BlockSpec((cpb, H, DK, DV), lambda i: (i, 0, 0, 0)),
                     pl.BlockSpec((cpb, 1, H * DK), lambda i: (i, 0, 0)),
                     pl.BlockSpec((cpb, 1, 128), lambda i: (i, 0, 0))]
        scratch = []
        args = (zC, bi, bfg, gn, c0, n0, m0)
        sem = ("arbitrary",)
    return pl.pallas_call(
        functools.partial(_mlstm_kernel, R=R, L=L, carry=carry), name="mlstm_prompt" if carry else "mlstm_sample",
        grid=grid, in_specs=in_specs, out_specs=out_specs, out_shape=out_shape,
        scratch_shapes=scratch, compiler_params=_cparams(*sem),
    )(*args)


def _sgu_kernel(z_ref, gv_ref, ws_ref, bcol_ref, y_ref, *v_out, nch):
    G = SG_GROUPS
    W = SG_WIDTH
    cw = W // G
    tril = _iota((SG_CHUNK, SG_CHUNK), 1) <= _iota((SG_CHUNK, SG_CHUNK), 0)
    for c in range(nch):
        rs = slice(c * SG_CHUNK, (c + 1) * SG_CHUNK)
        u = z_ref[rs, 0:W]
        vn = _rms(_gelu(z_ref[rs, W:2 * W]), gv_ref[...])
        if v_out:
            v_out[0][rs, :] = vn
        vb = _bf(vn)
        bcol = bcol_ref[...]
        outs = []
        for g in range(G):
            wc = _bf(jnp.where(tril, ws_ref[g], 0.0))
            outs.append(_dot(wc, vb[:, g * cw:(g + 1) * cw]) + bcol[:, g:g + 1])
        y_ref[rs, :] = _bf(_gelu(u) * jnp.concatenate(outs, axis=1))


def _sgu_call(zD, row0, n_rows, gv, ws, bcol, want_v):
    nch = _pick_block(n_rows // SG_CHUNK, (4, 2, 1))
    Rb = nch * SG_CHUNK
    rb0 = row0 // Rb
    out_shape = [jax.ShapeDtypeStruct((n_rows, SG_WIDTH), bf16)]
    out_specs = [pl.BlockSpec((Rb, SG_WIDTH), lambda i: (i, 0))]
    if want_v:
        out_shape.append(jax.ShapeDtypeStruct((n_rows, SG_WIDTH), f32))
        out_specs.append(pl.BlockSpec((Rb, SG_WIDTH), lambda i: (i, 0)))
    return pl.pallas_call(
        functools.partial(_sgu_kernel, nch=nch), name="sgu",
        grid=(n_rows // Rb,),
        in_specs=[pl.BlockSpec((Rb, W_D), lambda i: (rb0 + i, 0)),
                  pl.BlockSpec((1, SG_WIDTH), lambda i: (0, 0)),
                  pl.BlockSpec((SG_GROUPS, SG_CHUNK, SG_CHUNK), lambda i: (0, 0, 0)),
                  pl.BlockSpec((SG_CHUNK, 128), lambda i: (0, 0))],
        out_specs=out_specs, out_shape=out_shape,
        compiler_params=_cparams("arbitrary"),
    )(zD, gv, ws, bcol)


def _mla_prep_kernel(z_ref, cos_ref, sin_ref, gq_ref, gkv_ref, wqn_ref, wqr_ref, wa_ref, pb_ref, rot_ref,
                     q_ref, k_ref):
    z = z_ref[...]
    dq = z[:, 0:MLA_Q_RANK]
    dkv = z[:, MLA_Q_RANK:MLA_Q_RANK + MLA_KV_RANK]
    kr = z[:, MLA_Q_RANK + MLA_KV_RANK:MLA_Q_RANK + MLA_KV_RANK + 128]
    cos = cos_ref[...]
    sin = sin_ref[...]
    rot = rot_ref[...]
    dqn = _bf(_rms(dq, gq_ref[...]))
    qn = _dot(dqn, wqn_ref[...])
    qr = _dot(dqn, wqr_ref[...])
    qrope = qr * cos + _dot2b(qr, rot) * sin
    q_ref[...] = _bf(_dot(_bf(qn), wa_ref[...]) + _dot(_bf(qrope), pb_ref[...]))
    ckv = _rms(dkv, gkv_ref[...])
    krope = kr * cos[:, 0:128] + _dot2b(kr, rot[0:128, 0:128]) * sin[:, 0:128]
    k_ref[...] = jnp.concatenate([ckv, krope], axis=1)


def _mla_prep_call(rows, zB, cos, sin, gq, gkv, wqn, wqr, wa, pb, rot):
    R = rows.R
    HC = MLA_HEADS * 256
    c2 = lambda i: (0, 0)
    return pl.pallas_call(
        _mla_prep_kernel, name="mla_prep",
        grid=(rows.nb,),
        in_specs=[pl.BlockSpec((R, W_B), lambda i: (i, 0)),
                  pl.BlockSpec((R, 256), lambda i: (i, 0)),
                  pl.BlockSpec((R, 256), lambda i: (i, 0)),
                  pl.BlockSpec((1, MLA_Q_RANK), c2), pl.BlockSpec((1, MLA_KV_RANK), c2),
                  pl.BlockSpec((MLA_Q_RANK, MLA_HEADS * MLA_NOPE), c2),
                  pl.BlockSpec((MLA_Q_RANK, MLA_HEADS * MLA_ROPE), c2),
                  pl.BlockSpec((MLA_HEADS * MLA_NOPE, HC), c2),
                  pl.BlockSpec((MLA_HEADS * MLA_ROPE, HC), c2),
                  pl.BlockSpec((256, 256), c2)],
        out_specs=[pl.BlockSpec((R, HC), lambda i: (i, 0)), pl.BlockSpec((R, 256), lambda i: (i, 0))],
        out_shape=[jax.ShapeDtypeStruct((rows.N, HC), bf16), jax.ShapeDtypeStruct((rows.N, 256), f32)],
        compiler_params=_cparams("arbitrary"),
    )(zB, cos, sin, gq, gkv, wqn, wqr, wa, pb, rot)


def _flash_kernel(q_ref, k_ref, wuv_ref, y_ref, qs_scr, m_scr, l_scr, acc_scr, *, tq, tk):
    qi = pl.program_id(1)
    ki = pl.program_id(2)
    H = MLA_HEADS
    NR = H * tq
    diag = (qi * tq + tq - 1) // tk

    @pl.when(ki == 0)
    def _():
        for h in range(H):
            qs_scr[h * tq:(h + 1) * tq, :] = q_ref[:, h * 256:(h + 1) * 256]
        m_scr[...] = jnp.full(m_scr.shape, NEG, f32)
        l_scr[...] = jnp.zeros(l_scr.shape, f32)
        acc_scr[...] = jnp.zeros(acc_scr.shape, f32)

    def step(masked):
        kb = _bf(k_ref[...])
        s = _dot_nt(qs_scr[...], kb) * MLA_SCALE
        if masked:
            qpos = qi * tq + _iota((NR, tk), 0) % tq
            s = jnp.where(ki * tk + _iota((NR, tk), 1) <= qpos, s, NEG)
        m_prev = m_scr[...]
        m_next = jnp.maximum(m_prev, jnp.max(s, axis=1, keepdims=True))
        alpha = jnp.exp(m_prev - m_next)
        p = jnp.exp(s - jnp.concatenate([m_next] * (tk // 128), axis=1))
        l_scr[...] = alpha * l_scr[...] + jnp.sum(p, axis=1, keepdims=True)
        acc_scr[...] = alpha * acc_scr[...] + _dot(_bf(p), kb[:, 0:MLA_KV_RANK])
        m_scr[...] = m_next

    pl.when(ki < diag)(functools.partial(step, False))
    pl.when(ki == diag)(functools.partial(step, True))

    @pl.when(ki == pl.num_programs(2) - 1)
    def _():
        o = _bf(acc_scr[...] / l_scr[...])
        y = jnp.zeros((tq, MLA_HEADS * MLA_V), f32)
        for h in range(H):
            y = y + _dot(o[h * tq:(h + 1) * tq, :], wuv_ref[h])
        y_ref[...] = _bf(y)


def _flash_call(qcat, kcat, wuv, B, T):
    tq = 128
    tk = _pick_block(T, (512, 256, 128))
    nq, nk = T // tq, T // tk
    H = MLA_HEADS
    return pl.pallas_call(
        functools.partial(_flash_kernel, tq=tq, tk=tk), name="mla_flash",
        grid=(B, nq, nk),
        in_specs=[pl.BlockSpec((tq, H * 256), lambda b, i, j: (b * nq + i, 0)),
                  pl.BlockSpec((tk, 256), lambda b, i, j: (b * nk + jnp.minimum(j, (i * tq + tq - 1) // tk), 0)),
                  pl.BlockSpec((H, MLA_KV_RANK, H * MLA_V), lambda b, i, j: (0, 0, 0))],
        out_specs=pl.BlockSpec((tq, H * MLA_V), lambda b, i, j: (b * nq + i, 0)),
        out_shape=jax.ShapeDtypeStruct((B * T, H * MLA_V), bf16),
        scratch_shapes=[pltpu.VMEM((H * tq, 256), bf16), pltpu.VMEM((H * tq, 128), f32),
                        pltpu.VMEM((H * tq, 128), f32), pltpu.VMEM((H * tq, MLA_KV_RANK), f32)],
        compiler_params=_cparams("arbitrary", "arbitrary", "arbitrary"),
    )(qcat, kcat, wuv)


def _paged_kernel(pt_ref, q_ref, kn_ref, wuv_ref, ck_hbm, kr_hbm, y_ref,
                  ck_buf, kr_buf, kb_scr, sem, *, l, n_pages, page, TS, chunk):
    b = pl.program_id(0)
    nb = pl.num_programs(0)
    H = MLA_HEADS
    NR = TS * H
    n_keys = n_pages * page
    slot = b % 2

    def page_copies(bb, sl, p):
        pg = pt_ref[bb * n_pages + p]
        rows_ = pl.ds(pl.multiple_of(p * page, page), page)
        return (pltpu.make_async_copy(ck_hbm.at[l, pg], ck_buf.at[sl, rows_, :], sem.at[0, sl]),
                pltpu.make_async_copy(kr_hbm.at[l, pg], kr_buf.at[sl, rows_, :], sem.at[1, sl]))

    def start_all(bb, sl):
        def body(p, c):
            for cp in page_copies(bb, sl, p):
                cp.start()
            return c
        lax.fori_loop(0, n_pages, body, 0)

    def wait_all(bb, sl):
        def body(p, c):
            for cp in page_copies(bb, sl, p):
                cp.wait()
            return c
        lax.fori_loop(0, n_pages, body, 0)

    @pl.when(b == 0)
    def _():
        start_all(0, 0)

    @pl.when(b + 1 < nb)
    def _():
        start_all(b + 1, 1 - slot)

    wait_all(b, slot)

    q = q_ref[...]
    ql = q[:, 0:MLA_KV_RANK]
    qr = q[:, MLA_KV_RANK:MLA_KV_RANK + MLA_ROPE]
    kn = _bf(kn_ref[...])
    s_new = _dot_nt(q, kn) * MLA_SCALE
    ok = (_iota((NR, 8), 1) <= _iota((NR, 8), 0) // H) & (_iota((NR, 8), 1) < TS)
    s_new = jnp.where(ok, s_new, NEG)
    m = jnp.max(s_new, axis=1, keepdims=True)
    ss = []
    for c in range(n_keys // chunk):
        rows_ = pl.ds(c * chunk, chunk)
        kb_scr[rows_, :] = _bf(ck_buf[slot, rows_, :])
        s = (_dot_nt(ql, kb_scr[rows_, :]) + _dot_nt(qr, _bf(kr_buf[slot, rows_, :]))) * MLA_SCALE
        m = jnp.maximum(m, jnp.max(s, axis=1, keepdims=True))
        ss.append(s)
    e = jnp.exp(s_new - m)
    lsum = jnp.sum(e, axis=1, keepdims=True)
    acc = _dot(_bf(e), kn[:, 0:MLA_KV_RANK])
    for c in range(n_keys // chunk):
        e = jnp.exp(ss[c] - m)
        lsum = lsum + jnp.sum(e, axis=1, keepdims=True)
        acc = acc + _dot(_bf(e), kb_scr[pl.ds(c * chunk, chunk), :])
    o = acc / lsum
    row_head = _iota((NR, MLA_KV_RANK), 0) % H
    y = jnp.zeros((NR, H * MLA_V), f32)
    for h in range(H):
        y = y + _dot(_bf(jnp.where(row_head == h, o, 0.0)), wuv_ref[h])
    y_ref[...] = _bf(jnp.sum(y.reshape(TS, H, H * MLA_V), axis=1))


def _paged_call(page_table, q3, kn3, wuv, cache_ckv, cache_kr, l):
    DB, n_pages = page_table.shape
    TS = q3.shape[1] // MLA_HEADS
    H = MLA_HEADS
    page = cache_ckv.shape[2]
    n_keys = n_pages * page
    chunk = _pick_block(n_keys, (2048, 1024, 512, 256, 128))
    grid_spec = pltpu.PrefetchScalarGridSpec(
        num_scalar_prefetch=1,
        grid=(DB,),
        in_specs=[pl.BlockSpec((None, TS * H, 256), lambda b, pt: (b, 0, 0)),
                  pl.BlockSpec((None, 8, 256), lambda b, pt: (b, 0, 0)),
                  pl.BlockSpec((H, MLA_KV_RANK, H * MLA_V), lambda b, pt: (0, 0, 0)),
                  pl.BlockSpec(memory_space=pl.ANY), pl.BlockSpec(memory_space=pl.ANY)],
        out_specs=pl.BlockSpec((None, TS, H * MLA_V), lambda b, pt: (b, 0, 0)),
        scratch_shapes=[pltpu.VMEM((2, n_keys, MLA_KV_RANK), f32),
                        pltpu.VMEM((2, n_keys, MLA_ROPE), f32),
                        pltpu.VMEM((n_keys, MLA_KV_RANK), bf16),
                        pltpu.SemaphoreType.DMA((2, 2))],
    )
    return pl.pallas_call(
        functools.partial(_paged_kernel, l=l, n_pages=n_pages, page=page, TS=TS, chunk=chunk), name="mla_paged",
        grid_spec=grid_spec,
        out_shape=jax.ShapeDtypeStruct((DB, TS, H * MLA_V), bf16),
        compiler_params=_cparams("arbitrary"),
    )(page_table.reshape(-1), q3, kn3, wuv, cache_ckv, cache_kr)


def _merge_kernel(h_ref, ya_ref, yb_ref, yc_ref, yd_ref, g0, g1, g2, g3, b0, b1, b2, b3, o_ref):
    h = h_ref[...]
    acc = None
    for y_ref, g_ref, b_ref in ((ya_ref, g0, b0), (yb_ref, g1, b1), (yc_ref, g2, b2), (yd_ref, g3, b3)):
        gate = _sigmoid(_dot(h, _bf(g_ref[...])))
        term = gate * _dot(y_ref[...], _bf(b_ref[...]))
        acc = term if acc is None else acc + term
    o_ref[...] = _bf(acc)


def _merge_call(h1, ys, w_in, w_br, l):
    N = h1.shape[0]
    D = D_MODEL
    tm = _pick_block(N, (1088, 640, 512, 256, 128))
    tn = 256
    ncol = D // tn
    gate_specs = [pl.BlockSpec((None, D, tn), (lambda i, j, br=br: (l, 0, br * ncol + j))) for br in range(N_BRANCH)]
    br_specs = [pl.BlockSpec((None, None, BRANCH_WIDTH, tn), (lambda i, j, br=br: (l, br, 0, j))) for br in range(N_BRANCH)]
    y_specs = [pl.BlockSpec((tm, BRANCH_WIDTH), lambda i, j: (i, 0)) for _ in range(N_BRANCH)]
    return pl.pallas_call(
        _merge_kernel, name="merge",
        grid=(N // tm, ncol),
        in_specs=[pl.BlockSpec((tm, D), lambda i, j: (i, 0))] + y_specs + gate_specs + br_specs,
        out_specs=pl.BlockSpec((tm, tn), lambda i, j: (i, j)),
        out_shape=jax.ShapeDtypeStruct((N, D), bf16),
        compiler_params=_cparams("arbitrary", "arbitrary"),
    )(h1, *ys, w_in, w_in, w_in, w_in, w_br, w_br, w_br, w_br)


def _outproj_kernel(x_ref, mg_ref, wo_ref, g_ref, g1s, g1p, shs, shp, scs, scp, xo_ref, h_ref, *, rows):
    i = pl.program_id(0)
    x = x_ref[...] + rows.pick(i, g1s, g1p) * _dot(mg_ref[...], wo_ref[...])
    xo_ref[...] = x
    h_ref[...] = _rms(x, g_ref[...]) * (1.0 + rows.pick(i, scs, scp)) + rows.pick(i, shs, shp)


def _outproj_call(rows, x, merged, wo_bf, gain2, mod, l):
    R = rows.R
    D = D_MODEL
    g1s, g1p = rows.mod_specs(l, 2)
    shs, shp = rows.mod_specs(l, 3)
    scs, scp = rows.mod_specs(l, 4)
    return pl.pallas_call(
        functools.partial(_outproj_kernel, rows=rows), name="outproj",
        grid=(rows.nb,),
        in_specs=[pl.BlockSpec((R, D), lambda i: (i, 0)), pl.BlockSpec((R, D), lambda i: (i, 0)),
                  pl.BlockSpec((D, D), lambda i: (0, 0)), pl.BlockSpec((1, D), lambda i: (0, 0)),
                  g1s, g1p, shs, shp, scs, scp],
        out_specs=[pl.BlockSpec((R, D), lambda i: (i, 0)), pl.BlockSpec((R, D), lambda i: (i, 0))],
        out_shape=[jax.ShapeDtypeStruct((rows.N, D), f32), jax.ShapeDtypeStruct((rows.N, D), f32)],
        compiler_params=_cparams("arbitrary"),
    )(x, merged, wo_bf, gain2.reshape(1, D), mod, mod, mod, mod, mod, mod)


def _peer_candidates():
    K = PEER_TOPK
    pairs = [(k1, k2) for k1 in range(K) for k2 in range(K) if (k1 + 1) * (k2 + 1) <= K]
    n = -(-len(pairs) // 8) * 8
    g1 = np.zeros((n, K), np.float32)
    g2 = np.zeros((n, K), np.float32)
    for r, (k1, k2) in enumerate(pairs):
        g1[r, k1] = 1.0
        g2[r, k2] = 1.0
    return len(pairs), jnp.asarray(g1, bf16), jnp.asarray(g2, bf16)


def _select_rows3(sel_bf, v):
    hi = _bf(v)
    r1 = v - hi.astype(f32)
    mid = _bf(r1)
    lo = _bf(r1 - mid.astype(f32))
    return (_dot(sel_bf, hi) + _dot(sel_bf, mid)) + _dot(sel_bf, lo)


def _peer_score_kernel(h_ref, wqh_ref, wql_ref, k1_ref, k2_ref, g1_ref, g2_ref, a_ref, b_ref, g_ref,
                       v1_scr, i1_scr, v2_scr, i2_scr, top_scr, ea_scr, eb_scr, *, n_cand):
    T = h_ref.shape[0]
    NK, K = PEER_NKEYS, PEER_TOPK
    LT = 128
    NC = g1_ref.shape[0]
    hh, hl = _split2(h_ref[...])
    q = _dot(hh, wqh_ref[...]) + (_dot(hh, wql_ref[...]) + _dot(hl, wqh_ref[...]))
    key_idx = _iota((NK, LT), 0).astype(f32)
    cand_idx = _iota((NC, LT), 0).astype(f32)
    neg_inf = -jnp.inf
    g1 = g1_ref[...]
    g2 = g2_ref[...]

    def topk_into(s, val_scr, idx_scr):
        for kk in range(K):
            m = jnp.max(s, axis=0, keepdims=True)
            pos = jnp.min(jnp.where(s == m, key_idx, float(NK)), axis=0, keepdims=True)
            val_scr[pl.ds(kk, 1), :] = m
            idx_scr[pl.ds(kk, 1), :] = pos
            s = jnp.where(key_idx == pos, neg_inf, s)

    for h in range(PEER_HEADS):
        for lt in range(T // LT):
            cols = slice(lt * LT, (lt + 1) * LT)
            qh, ql = _split2(q[lt * LT:(lt + 1) * LT, h * PEER_DKEY:(h + 1) * PEER_DKEY])
            for k_ref, val_scr, idx_scr in ((k1_ref, v1_scr, i1_scr), (k2_ref, v2_scr, i2_scr)):
                kh, kl = _split2(k_ref[h])
                sT = _dot_nt(kh, qh) + (_dot_nt(kh, ql) + _dot_nt(kl, qh))
                topk_into(sT, val_scr, idx_scr)
            cand = _select_rows3(g1, v1_scr[...]) + _select_rows3(g2, v2_scr[...])
            cand = jnp.where(cand_idx < float(n_cand), cand, neg_inf)
            ca = _dot(g1, _bf(i1_scr[...]))
            cb = _dot(g2, _bf(i2_scr[...]))
            for kk in range(K):
                m = jnp.max(cand, axis=0, keepdims=True)
                pos = jnp.min(jnp.where(cand == m, cand_idx, float(NC)), axis=0, keepdims=True)
                sel = cand_idx == pos
                r = h * K + kk
                top_scr[pl.ds(r, 1), cols] = m
                ea_scr[pl.ds(r, 1), cols] = jnp.sum(jnp.where(sel, ca, 0.0), axis=0, keepdims=True)
                eb_scr[pl.ds(r, 1), cols] = jnp.sum(jnp.where(sel, cb, 0.0), axis=0, keepdims=True)
                cand = jnp.where(sel, neg_inf, cand)
        tops = top_scr[pl.ds(h * K, K), :]
        e = jnp.exp(tops - jnp.max(tops, axis=0, keepdims=True))
        top_scr[pl.ds(h * K, K), :] = e / jnp.sum(e, axis=0, keepdims=True)
    g_ref[...] = top_scr[...].T
    a_ref[...] = ea_scr[...].T
    b_ref[...] = eb_scr[...].T


def _peer_score_call(rows, h2, wqh, wql, k1p, k2p):
    R = rows.R
    D = D_MODEL
    HK = PEER_HEADS * PEER_TOPK
    o_spec = pl.BlockSpec((R, HK), lambda i: (i, 0))
    o_shape = jax.ShapeDtypeStruct((rows.N, HK), f32)
    n_cand, g1, g2 = _peer_candidates()
    return pl.pallas_call(
        functools.partial(_peer_score_kernel, n_cand=n_cand), name="peer_score",
        grid=(rows.nb,),
        in_specs=[pl.BlockSpec((R, D), lambda i: (i, 0)),
                  pl.BlockSpec((D, PEER_HEADS * PEER_DKEY), lambda i: (0, 0)),
                  pl.BlockSpec((D, PEER_HEADS * PEER_DKEY), lambda i: (0, 0)),
                  pl.BlockSpec((PEER_HEADS, PEER_NKEYS, PEER_DKEY), lambda i: (0, 0, 0)),
                  pl.BlockSpec((PEER_HEADS, PEER_NKEYS, PEER_DKEY), lambda i: (0, 0, 0)),
                  pl.BlockSpec(g1.shape, lambda i: (0, 0)), pl.BlockSpec(g2.shape, lambda i: (0, 0))],
        out_specs=[o_spec, o_spec, o_spec],
        out_shape=[o_shape, o_shape, o_shape],
        scratch_shapes=[pltpu.VMEM((PEER_TOPK, 128), f32)] * 4 + [pltpu.VMEM((HK, R), f32)] * 3,
        compiler_params=_cparams("arbitrary"),
    )(h2, wqh, wql, k1p, k2p, g1, g2)


def _peer_mix_kernel(x_ref, h_ref, ea_ref, eb_ref, g_ref, u_ref, v_ref, g2s, g2p, o_ref,
                     a_scr, hb_scr, acc_scr, *, rows, T, nc, ac):
    i = pl.program_id(0)
    j = pl.program_id(1)
    NK = PEER_NKEYS
    pitch = T + PEER_PITCH_PAD

    @pl.when(j == 0)
    def _():
        hb_scr[...] = _bf(h_ref[...])
        acc_scr[...] = jnp.zeros(acc_scr.shape, f32)

    @pl.when(j < nc)
    def _():
        res = _dot_nt(hb_scr[...], u_ref[...])
        for al in range(ac):
            start = pl.multiple_of((j * ac + al) * pitch, 8)
            a_scr[pl.ds(start, T), :] = res[:, al * NK:(al + 1) * NK]

    @pl.when(j == nc)
    def _():
        sub = _iota((NK, NK), 0).astype(f32)

        def tok(t):
            ia = ea_ref[pl.ds(t, 1), :]
            ib = eb_ref[pl.ds(t, 1), :]
            gg = g_ref[pl.ds(t, 1), :]
            onehot_a = _bf(jnp.where(sub == ia, 1.0, 0.0))
            gate_b = _bf(jnp.where(sub == ib, gg, 0.0))
            w = _dot_nt(onehot_a, gate_b)
            at = a_scr[pl.ds(t, NK, stride=pitch), :]
            a_scr[pl.ds(t, NK, stride=pitch), :] = w * _gelu(at)

        def tok_group(tg, carry):
            for u in range(PEER_TOKEN_UNROLL):
                tok(tg * PEER_TOKEN_UNROLL + u)
            return carry

        lax.fori_loop(0, T // PEER_TOKEN_UNROLL, tok_group, 0)

    @pl.when(j >= nc)
    def _():
        jj = j - nc
        parts = []
        for al in range(ac):
            start = pl.multiple_of((jj * ac + al) * pitch, 8)
            parts.append(_bf(a_scr[pl.ds(start, T), :]))
        acc_scr[...] += _dot(jnp.concatenate(parts, axis=1), v_ref[...])

    @pl.when(j == 2 * nc - 1)
    def _():
        o_ref[...] = x_ref[...] + rows.pick(i, g2s, g2p) * acc_scr[...]


def _peer_mix_call(rows, x, h2, ea, eb, g, ub, vb, mod, l):
    T = rows.R
    D = D_MODEL
    NK = PEER_NKEYS
    ac = 8
    nc = NK // ac
    HK = PEER_HEADS * PEER_TOPK
    R, nPb, nSb, Ns = rows.R, rows.nPb, rows.nSb, rows.Ns
    g2s = pl.BlockSpec((None, R, D), lambda i, j: (l, jnp.clip(i - nPb, 0, nSb - 1), 5))
    g2p = pl.BlockSpec((None, 8, D), lambda i, j: (l, Ns // 8, 5))
    row = lambda i, j: (i, 0)
    return pl.pallas_call(
        functools.partial(_peer_mix_kernel, rows=rows, T=T, nc=nc, ac=ac), name="peer_mix",
        grid=(rows.nb, 2 * nc),
        in_specs=[pl.BlockSpec((T, D), row), pl.BlockSpec((T, D), row),
                  pl.BlockSpec((T, HK), row), pl.BlockSpec((T, HK), row), pl.BlockSpec((T, HK), row),
                  pl.BlockSpec((ac * NK, D), lambda i, j: (jnp.minimum(j, nc - 1), 0)),
                  pl.BlockSpec((ac * NK, D), lambda i, j: (jnp.maximum(j - nc, 0), 0)),
                  g2s, g2p],
        out_specs=pl.BlockSpec((T, D), row),
        out_shape=jax.ShapeDtypeStruct((rows.N, D), f32),
        scratch_shapes=[pltpu.VMEM((NK * (T + PEER_PITCH_PAD), NK), f32),
                        pltpu.VMEM((T, D), bf16), pltpu.VMEM((T, D), f32)],
        compiler_params=_cparams("arbitrary", "arbitrary"),
    )(x, h2, ea, eb, g, ub, vb, mod, mod)


def _final_kernel(x_ref, g_ref, o_ref):
    o_ref[...] = _rms(x_ref[...], g_ref[...])


def _final_call(rows, x, gain):
    R, D = rows.R, D_MODEL
    return pl.pallas_call(
        _final_kernel, name="final_norm", grid=(rows.nb,),
        in_specs=[pl.BlockSpec((R, D), lambda i: (i, 0)), pl.BlockSpec((1, D), lambda i: (0, 0))],
        out_specs=pl.BlockSpec((R, D), lambda i: (i, 0)),
        out_shape=jax.ShapeDtypeStruct((rows.N, D), f32),
        compiler_params=_cparams("arbitrary"),
    )(x, gain.reshape(1, D))


def _mla_constants():
    H, R = MLA_HEADS, MLA_ROPE
    half = R // 2
    rot = np.zeros((H * R, H * R), np.float32)
    for g in range(H):
        for jj in range(half):
            rot[g * R + half + jj, g * R + jj] = -1.0
            rot[g * R + jj, g * R + half + jj] = 1.0
    place = np.zeros((H * R, H * 256), np.float32)
    for g in range(H):
        for jj in range(R):
            place[g * R + jj, g * 256 + MLA_KV_RANK + jj] = 1.0
    return jnp.asarray(rot, bf16), jnp.asarray(place, bf16)


def kernel(x_prompt, x_sample, cache_mla_ckv, cache_mla_kr, state_gla, state_mlstm_C, state_mlstm_n, state_mlstm_m, page_table, c_prompt, c_sample, w_ada, b_ada, norm1, norm2, w_in, gla_wa2, gla_ba, gla_norm, mla_w_uq, mla_gq, mla_gkv, mla_w_uk, mla_w_uv, ml_bi, ml_bf, ml_norm, sg_gv, sg_ws, sg_bs, w_br, w_o, peer_wq, peer_k1, peer_k2, peer_u, peer_v, final_norm):
    D = D_MODEL
    B, T, _ = x_prompt.shape
    DB, TS, _ = x_sample.shape
    depth = w_ada.shape[0]
    n_pages = page_table.shape[1]
    page = cache_mla_ckv.shape[2]
    past_len = n_pages * page
    rows = _Rows(B, T, DB, TS)
    Np, Ns = rows.Np, rows.Ns
    assert Ns % 128 == 0 and T % 128 == 0 and TS == 4

    c_rows = jnp.concatenate([jnp.repeat(c_sample, TS, axis=0), c_prompt,
                              jnp.zeros((16 - B, D), f32)], axis=0)
    mod = _ada_call(c_rows, w_ada, b_ada)

    x = jnp.concatenate([x_prompt.reshape(Np, D), x_sample.reshape(Ns, D)], axis=0)

    inv = ROPE_THETA ** (-jnp.arange(MLA_ROPE // 2, dtype=f32) / (MLA_ROPE // 2))
    pos_p = jnp.arange(T, dtype=jnp.int32)
    pos_s = past_len + jnp.arange(TS, dtype=jnp.int32)

    def table(fn):
        tp = fn(pos_p.astype(f32)[:, None] * inv)
        ts = fn(pos_s.astype(f32)[:, None] * inv)
        rows_ = jnp.concatenate([jnp.tile(tp, (B, 1)), jnp.tile(ts, (DB, 1))], axis=0)
        return jnp.tile(rows_, (1, 2 * MLA_HEADS))

    cos_t, sin_t = table(jnp.cos), table(jnp.sin)
    rot, place = _mla_constants()
    eye_h = jnp.eye(MLA_HEADS, dtype=f32)
    Ls = math.gcd(TS, GLA_CHUNK)
    Lp_gla = math.gcd(T, GLA_CHUNK)
    Lp_ml = math.gcd(T, ML_CHUNK)
    Ls_ml = math.gcd(TS, ML_CHUNK)
    spb = SG_CHUNK // TS

    new_p, new_s = [], []
    for l in range(depth):
        wl = w_in[l]
        g0 = N_BRANCH * D
        zc = lambda n: jnp.zeros((D, n), f32)
        wA = _bf(jnp.concatenate([wl[:, g0:g0 + 1552], zc(112)], axis=1))
        wB = _bf(jnp.concatenate([wl[:, g0 + 1552:g0 + 2096], zc(96)], axis=1))
        wC = _bf(jnp.concatenate([wl[:, g0 + 2096:g0 + 3632], wl[:, g0 + 3632:g0 + 3636], zc(124),
                                  wl[:, g0 + 3636:g0 + 3640], zc(124)], axis=1))
        wD = _bf(wl[:, g0 + 3640:g0 + 4664])
        wa2p = jnp.concatenate([gla_wa2[l], jnp.zeros((128 - GLA_RANK, GLA_HEADS * GLA_DK), f32)], axis=0)
        uq = mla_w_uq[l].reshape(MLA_Q_RANK, MLA_HEADS, MLA_NOPE + MLA_ROPE)
        wqn = _bf(uq[:, :, :MLA_NOPE].reshape(MLA_Q_RANK, MLA_HEADS * MLA_NOPE))
        wqr = _bf(uq[:, :, MLA_NOPE:].reshape(MLA_Q_RANK, MLA_HEADS * MLA_ROPE))
        uk = jnp.transpose(mla_w_uk[l], (1, 2, 0))
        wa = jnp.einsum('hnr,hg->hngr', uk, eye_h)
        wa = _bf(jnp.pad(wa, ((0, 0), (0, 0), (0, 0), (0, 256 - MLA_KV_RANK))).reshape(MLA_HEADS * MLA_NOPE, MLA_HEADS * 256))
        wuv = _bf(jnp.einsum('rhv,hg->hrgv', mla_w_uv[l], eye_h).reshape(MLA_HEADS, MLA_KV_RANK, MLA_HEADS * MLA_V))
        pad128 = lambda a: jnp.pad(a, (0, 128 - a.shape[0])).reshape(1, 128)
        bcol_p = jnp.pad(sg_bs[l].T, ((0, 0), (0, 128 - SG_GROUPS)))
        ws_s = jnp.stack([jnp.kron(jnp.eye(spb, dtype=f32), sg_ws[l, g, :TS, :TS]) for g in range(SG_GROUPS)])
        bcol_s = jnp.pad(jnp.tile(sg_bs[l][:, :TS].T, (spb, 1)), ((0, 0), (0, 128 - SG_GROUPS)))
        wo_bf = _bf(w_o[l])
        wq = peer_wq[l]
        wqh = _bf(wq)
        wql = _bf(wq - wqh.astype(f32))
        half = PEER_DKEY // 2
        k1p = jnp.pad(peer_k1[l], ((0, 0), (0, 0), (0, half)))
        k2p = jnp.pad(peer_k2[l], ((0, 0), (0, 0), (half, 0)))
        ub = _bf(peer_u[l])
        vb = _bf(peer_v[l])

        h1 = _modnorm_call(rows, x, norm1[l], mod, l, 0, 1)
        zA, zB, zC, zD = (_mm_call(h1, w) for w in (wA, wB, wC, wD))

        gn_a = gla_norm[l].reshape(1, -1)
        ba = gla_ba[l].reshape(1, -1)
        ya_p, S_p = _gla_call(zA, 0, B, T, Lp_gla, wa2p, ba, gn_a, None)
        ya_s, S_s = _gla_call(zA, Np, DB, TS, Ls, wa2p, ba, gn_a, state_gla[l])

        qcat, kcat = _mla_prep_call(rows, zB, cos_t, sin_t, mla_gq[l].reshape(1, -1), mla_gkv[l].reshape(1, -1),
                                    wqn, wqr, wa, place, rot)
        yb_p = _flash_call(qcat, kcat, wuv, B, T)
        q3 = qcat[Np:].reshape(DB, TS * MLA_HEADS, 256)
        kn3 = jnp.pad(kcat[Np:].reshape(DB, TS, 256), ((0, 0), (0, 8 - TS), (0, 0)))
        yb_s = _paged_call(page_table, q3, kn3, wuv, cache_mla_ckv, cache_mla_kr, l).reshape(Ns, -1)

        bi, bfg, gn_c = pad128(ml_bi[l]), pad128(ml_bf[l]), ml_norm[l].reshape(1, -1)
        yc_p, C_p, n_p, m_p = _mlstm_call(zC, 0, B, T, Lp_ml, bi, bfg, gn_c, None)
        st = (state_mlstm_C[l], state_mlstm_n[l].reshape(DB, 1, -1),
              jnp.pad(state_mlstm_m[l], ((0, 0), (0, 128 - ML_HEADS))).reshape(DB, 1, 128))
        yc_s, C_s, n_s, m_s = _mlstm_call(zC, Np, DB, TS, Ls_ml, bi, bfg, gn_c, st)

        gv = sg_gv[l].reshape(1, -1)
        (yd_p,) = _sgu_call(zD, 0, Np, gv, sg_ws[l], bcol_p, False)
        yd_s, v_s = _sgu_call(zD, Np, Ns, gv, ws_s, bcol_s, True)

        ys = [jnp.concatenate([p_, s_], axis=0) for p_, s_ in ((ya_p, ya_s), (yb_p, yb_s), (yc_p, yc_s), (yd_p, yd_s))]
        merged = _merge_call(h1, ys, w_in, w_br, l)
        x, h2 = _outproj_call(rows, x, merged, wo_bf, norm2[l], mod, l)

        ea, eb, g = _peer_score_call(rows, h2, wqh, wql, k1p, k2p)
        x = _peer_mix_call(rows, x, h2, ea, eb, g, ub, vb, mod, l)

        new_p.append(dict(ckv=kcat[:Np, :MLA_KV_RANK].reshape(B, T, -1),
                          kr=kcat[:Np, MLA_KV_RANK:MLA_KV_RANK + MLA_ROPE].reshape(B, T, -1),
                          S=S_p, C=C_p, n=n_p.reshape(B, ML_HEADS, ML_DK), m=m_p[:, 0, :ML_HEADS]))
        new_s.append(dict(ckv=kcat[Np:, :MLA_KV_RANK].reshape(DB, TS, -1),
                          kr=kcat[Np:, MLA_KV_RANK:MLA_KV_RANK + MLA_ROPE].reshape(DB, TS, -1),
                          S=S_s, C=C_s, n=n_s.reshape(DB, ML_HEADS, ML_DK), m=m_s[:, 0, :ML_HEADS],
                          v=v_s.reshape(DB, TS, -1)))

    y = _final_call(rows, x, final_norm)
    stk = lambda outs, name: jnp.stack([o[name] for o in outs], axis=0)
    return (y[:Np].reshape(B, T, D), y[Np:].reshape(DB, TS, D),
            stk(new_p, 'ckv'), stk(new_p, 'kr'), stk(new_p, 'S'), stk(new_p, 'C'), stk(new_p, 'n'), stk(new_p, 'm'),
            stk(new_s, 'ckv'), stk(new_s, 'kr'), stk(new_s, 'S'), stk(new_s, 'C'), stk(new_s, 'n'), stk(new_s, 'm'),
            stk(new_s, 'v'))
```

```python
import functools
import math

import numpy as np
import jax
import jax.numpy as jnp
from jax import lax
from jax.experimental import pallas as pl
from jax.experimental.pallas import tpu as pltpu

f32 = jnp.float32
bf16 = jnp.bfloat16

D_MODEL = 2048
EPS = 1e-6
ADA_CHUNKS = 6
GLA_HEADS, GLA_DK, GLA_DV, GLA_RANK, GLA_TAU, GLA_CHUNK = 4, 64, 128, 16, 16.0, 16
MLA_HEADS, MLA_Q_RANK, MLA_KV_RANK, MLA_NOPE, MLA_ROPE, MLA_V = 8, 384, 128, 64, 32, 64
MLA_SCALE = (MLA_NOPE + MLA_ROPE) ** -0.5
ROPE_THETA = 10000.0
ML_HEADS, ML_DK, ML_DV, ML_CHUNK = 4, 64, 128, 64
SG_GROUPS, SG_CHUNK, SG_WIDTH = 4, 128, 512
PEER_HEADS, PEER_NKEYS, PEER_DKEY, PEER_TOPK = 8, 128, 128, 16
PEER_EXPERTS = PEER_NKEYS * PEER_NKEYS
BRANCH_WIDTH = 512
N_BRANCH = 4

W_A = 1664
W_B = 640
W_C = 1792
W_D = 1024

NEG = -1e30
VMEM_LIMIT_V7X = 56 * 1024 * 1024
PEER_PITCH_PAD = 8
PEER_TOKEN_UNROLL = 8

def _cparams(*sem):
    return pltpu.CompilerParams(dimension_semantics=sem, vmem_limit_bytes=VMEM_LIMIT_V7X)


def _bf(x):
    return x.astype(bf16)


def _dot(a, b):
    return jnp.dot(a, b, preferred_element_type=f32)


def _dot_nt(a, b):
    return lax.dot_general(a, b, (((1,), (1,)), ((), ())), preferred_element_type=f32)


def _split2(x):
    hi = _bf(x)
    lo = _bf(x - hi.astype(f32))
    return hi, lo


def _dot3(a, b):
    ah, al = _split2(a)
    bh, bl = _split2(b)
    return _dot(ah, bh) + (_dot(ah, bl) + _dot(al, bh))


def _dot2b(a, b_bf):
    ah, al = _split2(a)
    return _dot(ah, b_bf) + _dot(al, b_bf)


def _sigmoid(x):
    return 1.0 / (1.0 + jnp.exp(-x))


def _log_sigmoid(x):
    return jnp.minimum(x, 0.0) - jnp.log1p(jnp.exp(-jnp.abs(x)))


def _gelu(x):
    return jax.nn.gelu(x)


def _rms(x, g):
    return x * lax.rsqrt(jnp.mean(x * x, axis=-1, keepdims=True) + EPS) * g


def _iota(shape, dim):
    return lax.broadcasted_iota(jnp.int32, shape, dim)


def _seg_cumsum(x, rowmod, L):
    s = 1
    while s < L:
        x = x + jnp.where(rowmod >= s, pltpu.roll(x, s, 0), 0.0)
        s *= 2
    return x


def _seg_last(x, rowmod, L):
    R = x.shape[0]
    s = 1
    while s < L:
        x = jnp.where(rowmod + s < L, pltpu.roll(x, R - s, 0), x)
        s *= 2
    return x


def _expand_heads(x, n_heads, width):
    R = x.shape[0]
    lane_head = _iota((R, n_heads * width), 1) // width
    out = jnp.zeros((R, n_heads * width), f32)
    for h in range(n_heads):
        out = jnp.where(lane_head == h, x[:, h:h + 1], out)
    return out


def _pick_block(n, cands):
    for c in cands:
        if n % c == 0:
            return c
    raise ValueError(f"no block size in {cands} divides {n}")


def _ada_kernel(c_ref, w_ref, b_ref, o_ref):
    c = c_ref[...]
    a = _bf(c * _sigmoid(c))
    o_ref[...] = _dot(a, _bf(w_ref[...])) + b_ref[...]


def _ada_call(c_rows, w_ada, b_ada):
    L, D, N6 = w_ada.shape
    Rc = c_rows.shape[0]
    tn = 1024
    return pl.pallas_call(
        _ada_kernel, name="ada_mod",
        grid=(L, N6 // tn),
        in_specs=[pl.BlockSpec((Rc, D), lambda l, j: (0, 0)),
                  pl.BlockSpec((None, D, tn), lambda l, j: (l, 0, j)),
                  pl.BlockSpec((None, 1, tn), lambda l, j: (l, 0, j))],
        out_specs=pl.BlockSpec((None, Rc, tn), lambda l, j: (l, 0, j)),
        out_shape=jax.ShapeDtypeStruct((L, Rc, N6), f32),
        compiler_params=_cparams("arbitrary", "arbitrary"),
    )(c_rows, w_ada, b_ada.reshape(L, 1, N6))


class _Rows:
    def __init__(self, B, T, DB, TS, block_sizes=(256, 128)):
        self.B, self.T, self.DB, self.TS = B, T, DB, TS
        self.Np, self.Ns = B * T, DB * TS
        self.N = self.Np + self.Ns
        self.R = _pick_block(math.gcd(T, self.Ns), block_sizes)
        self.nPb = self.Np // self.R
        self.nSb = self.Ns // self.R
        self.nb = self.nPb + self.nSb
        self.blocks_per_seq = T // self.R

    def mod_specs(self, l, chunk):
        R, nPb, nSb, Ns = self.R, self.nPb, self.nSb, self.Ns
        s = pl.BlockSpec((None, R, D_MODEL), lambda i: (l, jnp.clip(i - nPb, 0, nSb - 1), chunk))
        p = pl.BlockSpec((None, 8, D_MODEL), lambda i: (l, Ns // 8, chunk))
        return s, p

    def pick(self, i, s_ref, p_ref):
        b = jnp.minimum(i // self.blocks_per_seq, self.B - 1)
        return jnp.where(i >= self.nPb, s_ref[...], p_ref[pl.ds(b, 1), :])


def _modnorm_kernel(x_ref, g_ref, shs_ref, shp_ref, scs_ref, scp_ref, h_ref, *, rows):
    i = pl.program_id(0)
    sh = rows.pick(i, shs_ref, shp_ref)
    sc = rows.pick(i, scs_ref, scp_ref)
    h_ref[...] = _bf(_rms(x_ref[...], g_ref[...]) * (1.0 + sc) + sh)


def _modnorm_call(rows, x, gain, mod, l, sh_chunk, sc_chunk):
    R = rows.R
    shs, shp = rows.mod_specs(l, sh_chunk)
    scs, scp = rows.mod_specs(l, sc_chunk)
    return pl.pallas_call(
        functools.partial(_modnorm_kernel, rows=rows), name="modnorm",
        grid=(rows.nb,),
        in_specs=[pl.BlockSpec((R, D_MODEL), lambda i: (i, 0)),
                  pl.BlockSpec((1, D_MODEL), lambda i: (0, 0)), shs, shp, scs, scp],
        out_specs=pl.BlockSpec((R, D_MODEL), lambda i: (i, 0)),
        out_shape=jax.ShapeDtypeStruct((rows.N, D_MODEL), bf16),
        compiler_params=_cparams("arbitrary"),
    )(x, gain.reshape(1, D_MODEL), mod, mod, mod, mod)


def _mm_kernel(x_ref, wt_ref, o_ref):
    o_ref[...] = _dot_nt(x_ref[...], wt_ref[...]).astype(o_ref.dtype)


def _mm_call(x, wt, out_dtype=f32):
    M, K = x.shape
    N = wt.shape[0]
    tm = _pick_block(M, (1088, 1024, 640, 512, 256, 128))
    return pl.pallas_call(
        _mm_kernel, name="mixer_proj",
        grid=(M // tm,),
        in_specs=[pl.BlockSpec((tm, K), lambda i: (i, 0)),
                  pl.BlockSpec((N, K), lambda i: (0, 0))],
        out_specs=pl.BlockSpec((tm, N), lambda i: (i, 0)),
        out_shape=jax.ShapeDtypeStruct((M, N), out_dtype),
        compiler_params=_cparams("arbitrary"),
    )(x, wt)


def _gla_kernel(*refs, R, L, carry):
    if carry:
        z_ref, wa2_ref, ba_ref, gn_ref, y_ref, so_ref, s_scr = refs
    else:
        z_ref, wa2_ref, ba_ref, gn_ref, s0_ref, y_ref, so_ref = refs
    H, DK, DV = GLA_HEADS, GLA_DK, GLA_DV
    HK, HV = H * DK, H * DV
    nchunk = R // L
    z = z_ref[...]
    q = z[:, 0:HK] * (DK ** -0.5)
    k = z[:, HK:2 * HK]
    v = z[:, 2 * HK:2 * HK + HV]
    r = z[:, 2 * HK + HV:2 * HK + 2 * HV]
    ag = z[:, 2 * HK + 2 * HV:2 * HK + 2 * HV + 128]
    la = _log_sigmoid(_dot3(ag, wa2_ref[...]) + ba_ref[...]) * (1.0 / GLA_TAU)
    rowmod = _iota((R, HK), 0) % L
    b = _seg_cumsum(la, rowmod, L)
    tot = _seg_last(b, rowmod, L)

    sel_row_head = _iota((HK, 128), 0) // DK
    sel_lane = _iota((HK, 128), 1)
    att = jnp.zeros((R, 128), f32)
    for d in range(L):
        ks = k if d == 0 else pltpu.roll(k, d, 0)
        bs = b if d == 0 else pltpu.roll(b, d, 0)
        p = q * ks * jnp.exp(jnp.where(rowmod >= d, b - bs, NEG))
        att = att + _dot2b(p, _bf(sel_lane == d * H + sel_row_head))
    ex_row = _iota((128, HV), 0)
    ex_lane_head = _iota((128, HV), 1) // DV
    o = jnp.zeros((R, HV), f32)
    for d in range(L):
        vs = v if d == 0 else pltpu.roll(v, d, 0)
        o = o + _dot2b(att, _bf(ex_row == d * H + ex_lane_head)) * vs

    qt = _bf(q * jnp.exp(b))
    kd = k * jnp.exp(tot - b)
    kdT = _bf(kd.T)
    totT = tot.T
    vb = _bf(v)
    blockdiag = (_iota((HK, HV), 0) // DK) == (_iota((HK, HV), 1) // DV)
    row_chunk = _iota((R, HK), 0) // L
    col_chunk = _iota((HK, R), 1) // L
    if carry:
        @pl.when(pl.program_id(1) == 0)
        def _():
            s_scr[...] = jnp.zeros((HK, HV), f32)
        S = s_scr[...]
    for c in range(nchunk):
        if not carry:
            rows_ = [jnp.concatenate([s0_ref[c, h] if g == h else jnp.zeros((DK, DV), f32) for g in range(H)], axis=1)
                     for h in range(H)]
            S = jnp.concatenate(rows_, axis=0)
        o = o + _dot(jnp.where(row_chunk == c, qt, jnp.zeros_like(qt)), _bf(S))
        U = _dot(jnp.where(col_chunk == c, kdT, jnp.zeros_like(kdT)), vb)
        dcol = jnp.exp(totT[:, c * L:c * L + 1])
        S = dcol * S + jnp.where(blockdiag, U, 0.0)
        if not carry:
            for h in range(H):
                so_ref[c, h] = S[h * DK:(h + 1) * DK, h * DV:(h + 1) * DV]
    if carry:
        s_scr[...] = S

        @pl.when(pl.program_id(1) == pl.num_programs(1) - 1)
        def _():
            for h in range(H):
                so_ref[h] = S[h * DK:(h + 1) * DK, h * DV:(h + 1) * DV]

    gn = gn_ref[...]
    outs = []
    for h in range(H):
        oh = o[:, h * DV:(h + 1) * DV]
        outs.append(_rms(oh, gn[:, h * DV:(h + 1) * DV]))
    y_ref[...] = _bf(jnp.concatenate(outs, axis=1) * (r * _sigmoid(r)))


def _gla_call(zA, row0, B, T, L, wa2p, ba, gn, s0):
    H, DK, DV = GLA_HEADS, GLA_DK, GLA_DV
    carry = s0 is None
    if carry:
        R = _pick_block(T, (256, 128))
        nt = T // R
        grid = (B, nt)
        rb0 = row0 // R
        in_specs = [pl.BlockSpec((R, W_A), lambda b, t: (rb0 + b * nt + t, 0)),
                    pl.BlockSpec((128, H * DK), lambda b, t: (0, 0)),
                    pl.BlockSpec((1, H * DK), lambda b, t: (0, 0)),
                    pl.BlockSpec((1, H * DV), lambda b, t: (0, 0))]
        out_specs = [pl.BlockSpec((R, H * DV), lambda b, t: (b * nt + t, 0)),
                     pl.BlockSpec((None, H, DK, DV), lambda b, t: (b, 0, 0, 0))]
        scratch = [pltpu.VMEM((H * DK, H * DV), f32)]
        args = (zA, wa2p, ba, gn)
        sem = ("arbitrary", "arbitrary")
    else:
        R = 128
        nb = (B * T) // R
        cpb = R // L
        grid = (nb,)
        rb0 = row0 // R
        in_specs = [pl.BlockSpec((R, W_A), lambda i: (rb0 + i, 0)),
                    pl.BlockSpec((128, H * DK), lambda i: (0, 0)),
                    pl.BlockSpec((1, H * DK), lambda i: (0, 0)),
                    pl.BlockSpec((1, H * DV), lambda i: (0, 0)),
                    pl.BlockSpec((cpb, H, DK, DV), lambda i: (i, 0, 0, 0))]
        out_specs = [pl.BlockSpec((R, H * DV), lambda i: (i, 0)),
                     pl.BlockSpec((cpb, H, DK, DV), lambda i: (i, 0, 0, 0))]
        scratch = []
        args = (zA, wa2p, ba, gn, s0)
        sem = ("arbitrary",)
    return pl.pallas_call(
        functools.partial(_gla_kernel, R=R, L=L, carry=carry), name="gla_prompt" if carry else "gla_sample",
        grid=grid, in_specs=in_specs, out_specs=out_specs,
        out_shape=[jax.ShapeDtypeStruct((B * T, H * DV), bf16),
                   jax.ShapeDtypeStruct((B, H, DK, DV), f32)],
        scratch_shapes=scratch,
        compiler_params=_cparams(*sem),
    )(*args)


def _mlstm_kernel(*refs, R, L, carry):
    if carry:
        z_ref, bi_ref, bf_ref, gn_ref, y_ref, co_ref, no_ref, mo_ref, c_scr, n_scr, m_scr = refs
    else:
        z_ref, bi_ref, bf_ref, gn_ref, c0_ref, n0_ref, m0_ref, y_ref, co_ref, no_ref, mo_ref = refs
    H, DK, DV = ML_HEADS, ML_DK, ML_DV
    HK, HV = H * DK, H * DV
    nchunk = R // L
    z = z_ref[...]
    q = z[:, 0:HK]
    k = z[:, HK:2 * HK] * (DK ** -0.5)
    v = z[:, 2 * HK:2 * HK + HV]
    og = z[:, 2 * HK + HV:2 * HK + 2 * HV]
    ig = z[:, 2 * HK + 2 * HV:2 * HK + 2 * HV + 128] + bi_ref[...]
    fg = z[:, 2 * HK + 2 * HV + 128:2 * HK + 2 * HV + 256] + bf_ref[...]
    rowmod = _iota((R, 128), 0) % L
    row_chunk128 = _iota((R, 128), 0) // L
    F = _seg_cumsum(_log_sigmoid(fg), rowmod, L)
    FT = F.T
    IT = ig.T

    if carry:
        @pl.when(pl.program_id(1) == 0)
        def _():
            c_scr[...] = jnp.zeros((HK, HV), f32)
            n_scr[...] = jnp.zeros((1, HK), f32)
            m_scr[...] = jnp.zeros((1, 128), f32)

    tpos = _iota((R, R), 0)
    spos = _iota((R, R), 1)
    causal = (tpos // L == spos // L) & (spos <= tpos)
    qb = _bf(q)
    kb = _bf(k)
    vb = _bf(v)
    lane_head_k = _iota((R, HK), 1) // DK

    Dms, qks, MX = [], [], jnp.zeros((R, 128), f32)
    lane128 = _iota((R, 128), 1)
    for h in range(H):
        Dm = jnp.where(causal, F[:, h:h + 1] - FT[h:h + 1, :] + IT[h:h + 1, :], NEG)
        Dms.append(Dm)
        MX = jnp.where(lane128 == h, jnp.max(Dm, axis=1, keepdims=True), MX)
        qks.append(_dot_nt(jnp.where(lane_head_k == h, qb, jnp.zeros_like(qb)), kb))

    if carry:
        Mprev = jnp.zeros((R, 128), f32)
        m_run = m_scr[...]
        for c in range(nchunk):
            Mprev = jnp.where(row_chunk128 == c, m_run, Mprev)
            last = c * L + L - 1
            m_run = jnp.maximum(F[last:last + 1, :] + m_run, MX[last:last + 1, :])
    else:
        Mprev = jnp.zeros((R, 128), f32)
        for c in range(nchunk):
            Mprev = jnp.where(row_chunk128 == c, m0_ref[c], Mprev)
    Mt = jnp.maximum(F + Mprev, MX)
    Mnew = _seg_last(Mt, rowmod, L)
    Fend = _seg_last(F, rowmod, L)
    Wprev = jnp.exp(F + Mprev - Mt)
    Wsrc = jnp.exp(Fend - F + ig - Mnew)
    Wold = jnp.exp(Fend + Mprev - Mnew)
    Emt = jnp.exp(-Mt)

    kw = k * _expand_heads(Wsrc, H, DK)
    kwT = _bf(kw.T)
    Wold_k = _expand_heads(Wold, H, DK)
    Wold_v = _expand_heads(Wold, H, DV)
    blockdiag = (_iota((HK, HV), 0) // DK) == (_iota((HK, HV), 1) // DV)
    row_chunk = _iota((R, HK), 0) // L
    col_chunk = _iota((HK, R), 1) // L

    qC = jnp.zeros((R, HV), f32)
    Nrows = jnp.zeros((R, HK), f32)
    if carry:
        C = c_scr[...]
        n = n_scr[...]
    for c in range(nchunk):
        if not carry:
            rows_ = [jnp.concatenate([c0_ref[c, h] if g == h else jnp.zeros((DK, DV), f32) for g in range(H)], axis=1)
                     for h in range(H)]
            C = jnp.concatenate(rows_, axis=0)
            n = n0_ref[c]
        in_chunk = row_chunk == c
        qC = qC + _dot(jnp.where(in_chunk, qb, jnp.zeros_like(qb)), _bf(C))
        Nrows = jnp.where(in_chunk, n, Nrows)
        last = c * L + L - 1
        U = _dot(jnp.where(col_chunk == c, kwT, jnp.zeros_like(kwT)), vb)
        C = Wold_v[last:last + 1, :] * C + jnp.where(blockdiag, U, 0.0)
        n = Wold_k[last:last + 1, :] * n + jnp.sum(jnp.where(in_chunk, kw, 0.0), axis=0, keepdims=True)
        if not carry:
            for h in range(H):
                co_ref[c, h] = C[h * DK:(h + 1) * DK, h * DV:(h + 1) * DV]
            no_ref[c] = n
            mo_ref[c] = Mnew[last:last + 1, :]
    if carry:
        c_scr[...] = C
        n_scr[...] = n
        m_scr[...] = Mnew[R - 1:R, :]

        @pl.when(pl.program_id(1) == pl.num_programs(1) - 1)
        def _():
            for h in range(H):
                co_ref[h] = C[h * DK:(h + 1) * DK, h * DV:(h + 1) * DV]
            no_ref[...] = n
            mo_ref[...] = Mnew[R - 1:R, :]

    qn = _dot2b(q * Nrows, _bf(_iota((HK, 128), 1) == _iota((HK, 128), 0) // DK))
    gn = gn_ref[...]
    outs = []
    for h in range(H):
        A = jnp.exp(Dms[h] - Mt[:, h:h + 1]) * qks[h]
        num = Wprev[:, h:h + 1] * qC[:, h * DV:(h + 1) * DV] + _dot(_bf(A), vb[:, h * DV:(h + 1) * DV])
        nq = Wprev[:, h:h + 1] * qn[:, h:h + 1] + jnp.sum(A, axis=1, keepdims=True)
        hh = num / jnp.maximum(jnp.abs(nq), Emt[:, h:h + 1])
        outs.append(_rms(hh, gn[:, h * DV:(h + 1) * DV]))
    y_ref[...] = _bf(_sigmoid(og) * jnp.concatenate(outs, axis=1))


def _mlstm_call(zC, row0, B, T, L, bi, bfg, gn, st):
    H, DK, DV = ML_HEADS, ML_DK, ML_DV
    carry = st is None
    R = 128
    out_shape = [jax.ShapeDtypeStruct((B * T, H * DV), bf16),
                 jax.ShapeDtypeStruct((B, H, DK, DV), f32),
                 jax.ShapeDtypeStruct((B, 1, H * DK), f32),
                 jax.ShapeDtypeStruct((B, 1, 128), f32)]
    rb0 = row0 // R
    if carry:
        nt = T // R
        grid = (B, nt)
        c2 = lambda b, t: (0, 0)
        in_specs = [pl.BlockSpec((R, W_C), lambda b, t: (rb0 + b * nt + t, 0)),
                    pl.BlockSpec((1, 128), c2), pl.BlockSpec((1, 128), c2), pl.BlockSpec((1, H * DV), c2)]
        out_specs = [pl.BlockSpec((R, H * DV), lambda b, t: (b * nt + t, 0)),
                     pl.BlockSpec((None, H, DK, DV), lambda b, t: (b, 0, 0, 0)),
                     pl.BlockSpec((None, 1, H * DK), lambda b, t: (b, 0, 0)),
                     pl.BlockSpec((None, 1, 128), lambda b, t: (b, 0, 0))]
        scratch = [pltpu.VMEM((H * DK, H * DV), f32), pltpu.VMEM((1, H * DK), f32), pltpu.VMEM((1, 128), f32)]
        args = (zC, bi, bfg, gn)
        sem = ("arbitrary", "arbitrary")
    else:
        c0, n0, m0 = st
        nb = (B * T) // R
        cpb = R // L
        grid = (nb,)
        c2 = lambda i: (0, 0)
        in_specs = [pl.BlockSpec((R, W_C), lambda i: (rb0 + i, 0)),
                    pl.BlockSpec((1, 128), c2), pl.BlockSpec((1, 128), c2), pl.BlockSpec((1, H * DV), c2),
                    pl.BlockSpec((cpb, H, DK, DV), lambda i: (i, 0, 0, 0)),
                    pl.BlockSpec((cpb, 1, H * DK), lambda i: (i, 0, 0)),
                    pl.BlockSpec((cpb, 1, 128), lambda i: (i, 0, 0))]
        out_specs = [pl.BlockSpec((R, H * DV), lambda i: (i, 0)),
                     pl.BlockSpec((cpb, H, DK, DV), lambda i: (i, 0, 0, 0)),
                     pl.BlockSpec((cpb, 1, H * DK), lambda i: (i, 0, 0)),
                     pl.BlockSpec((cpb, 1, 128), lambda i: (i, 0, 0))]
        scratch = []
        args = (zC, bi, bfg, gn, c0, n0, m0)
        sem = ("arbitrary",)
    return pl.pallas_call(
        functools.partial(_mlstm_kernel, R=R, L=L, carry=carry), name="mlstm_prompt" if carry else "mlstm_sample",
        grid=grid, in_specs=in_specs, out_specs=out_specs, out_shape=out_shape,
        scratch_shapes=scratch, compiler_params=_cparams(*sem),
    )(*args)


def _sgu_kernel(z_ref, gv_ref, ws_ref, bcol_ref, y_ref, *v_out, nch):
    G = SG_GROUPS
    W = SG_WIDTH
    cw = W // G
    tril = _iota((SG_CHUNK, SG_CHUNK), 1) <= _iota((SG_CHUNK, SG_CHUNK), 0)
    for c in range(nch):
        rs = slice(c * SG_CHUNK, (c + 1) * SG_CHUNK)
        u = z_ref[rs, 0:W]
        vn = _rms(_gelu(z_ref[rs, W:2 * W]), gv_ref[...])
        if v_out:
            v_out[0][rs, :] = vn
        vb = _bf(vn)
        bcol = bcol_ref[...]
        outs = []
        for g in range(G):
            wc = _bf(jnp.where(tril, ws_ref[g], 0.0))
            outs.append(_dot(wc, vb[:, g * cw:(g + 1) * cw]) + bcol[:, g:g + 1])
        y_ref[rs, :] = _bf(_gelu(u) * jnp.concatenate(outs, axis=1))


def _sgu_call(zD, row0, n_rows, gv, ws, bcol, want_v):
    nch = _pick_block(n_rows // SG_CHUNK, (4, 2, 1))
    Rb = nch * SG_CHUNK
    rb0 = row0 // Rb
    out_shape = [jax.ShapeDtypeStruct((n_rows, SG_WIDTH), bf16)]
    out_specs = [pl.BlockSpec((Rb, SG_WIDTH), lambda i: (i, 0))]
    if want_v:
        out_shape.append(jax.ShapeDtypeStruct((n_rows, SG_WIDTH), f32))
        out_specs.append(pl.BlockSpec((Rb, SG_WIDTH), lambda i: (i, 0)))
    return pl.pallas_call(
        functools.partial(_sgu_kernel, nch=nch), name="sgu",
        grid=(n_rows // Rb,),
        in_specs=[pl.BlockSpec((Rb, W_D), lambda i: (rb0 + i, 0)),
                  pl.BlockSpec((1, SG_WIDTH), lambda i: (0, 0)),
                  pl.BlockSpec((SG_GROUPS, SG_CHUNK, SG_CHUNK), lambda i: (0, 0, 0)),
                  pl.BlockSpec((SG_CHUNK, 128), lambda i: (0, 0))],
        out_specs=out_specs, out_shape=out_shape,
        compiler_params=_cparams("arbitrary"),
    )(zD, gv, ws, bcol)


def _mla_prep_kernel(z_ref, cos_ref, sin_ref, gq_ref, gkv_ref, wqn_ref, wqr_ref, wa_ref, pb_ref, rot_ref,
                     q_ref, k_ref):
    z = z_ref[...]
    dq = z[:, 0:MLA_Q_RANK]
    dkv = z[:, MLA_Q_RANK:MLA_Q_RANK + MLA_KV_RANK]
    kr = z[:, MLA_Q_RANK + MLA_KV_RANK:MLA_Q_RANK + MLA_KV_RANK + 128]
    cos = cos_ref[...]
    sin = sin_ref[...]
    rot = rot_ref[...]
    dqn = _bf(_rms(dq, gq_ref[...]))
    qn = _dot(dqn, wqn_ref[...])
    qr = _dot(dqn, wqr_ref[...])
    qrope = qr * cos + _dot2b(qr, rot) * sin
    q_ref[...] = _bf(_dot(_bf(qn), wa_ref[...]) + _dot(_bf(qrope), pb_ref[...]))
    ckv = _rms(dkv, gkv_ref[...])
    krope = kr * cos[:, 0:128] + _dot2b(kr, rot[0:128, 0:128]) * sin[:, 0:128]
    k_ref[...] = jnp.concatenate([ckv, krope], axis=1)


def _mla_prep_call(rows, zB, cos, sin, gq, gkv, wqn, wqr, wa, pb, rot):
    R = rows.R
    HC = MLA_HEADS * 256
    c2 = lambda i: (0, 0)
    return pl.pallas_call(
        _mla_prep_kernel, name="mla_prep",
        grid=(rows.nb,),
        in_specs=[pl.BlockSpec((R, W_B), lambda i: (i, 0)),
                  pl.BlockSpec((R, 256), lambda i: (i, 0)),
                  pl.BlockSpec((R, 256), lambda i: (i, 0)),
                  pl.BlockSpec((1, MLA_Q_RANK), c2), pl.BlockSpec((1, MLA_KV_RANK), c2),
                  pl.BlockSpec((MLA_Q_RANK, MLA_HEADS * MLA_NOPE), c2),
                  pl.BlockSpec((MLA_Q_RANK, MLA_HEADS * MLA_ROPE), c2),
                  pl.BlockSpec((MLA_HEADS * MLA_NOPE, HC), c2),
                  pl.BlockSpec((MLA_HEADS * MLA_ROPE, HC), c2),
                  pl.BlockSpec((256, 256), c2)],
        out_specs=[pl.BlockSpec((R, HC), lambda i: (i, 0)), pl.BlockSpec((R, 256), lambda i: (i, 0))],
        out_shape=[jax.ShapeDtypeStruct((rows.N, HC), bf16), jax.ShapeDtypeStruct((rows.N, 256), f32)],
        compiler_params=_cparams("arbitrary"),
    )(zB, cos, sin, gq, gkv, wqn, wqr, wa, pb, rot)


def _flash_kernel(q_ref, k_ref, wuv_ref, y_ref, qs_scr, m_scr, l_scr, acc_scr, *, tq, tk):
    qi = pl.program_id(1)
    ki = pl.program_id(2)
    H = MLA_HEADS
    NR = H * tq
    diag = (qi * tq + tq - 1) // tk

    @pl.when(ki == 0)
    def _():
        for h in range(H):
            qs_scr[h * tq:(h + 1) * tq, :] = q_ref[:, h * 256:(h + 1) * 256]
        m_scr[...] = jnp.full(m_scr.shape, NEG, f32)
        l_scr[...] = jnp.zeros(l_scr.shape, f32)
        acc_scr[...] = jnp.zeros(acc_scr.shape, f32)

    def step(masked):
        kb = _bf(k_ref[...])
        s = _dot_nt(qs_scr[...], kb) * MLA_SCALE
        if masked:
            qpos = qi * tq + _iota((NR, tk), 0) % tq
            s = jnp.where(ki * tk + _iota((NR, tk), 1) <= qpos, s, NEG)
        m_prev = m_scr[...]
        m_next = jnp.maximum(m_prev, jnp.max(s, axis=1, keepdims=True))
        alpha = jnp.exp(m_prev - m_next)
        p = jnp.exp(s - jnp.concatenate([m_next] * (tk // 128), axis=1))
        l_scr[...] = alpha * l_scr[...] + jnp.sum(p, axis=1, keepdims=True)
        acc_scr[...] = alpha * acc_scr[...] + _dot(_bf(p), kb[:, 0:MLA_KV_RANK])
        m_scr[...] = m_next

    pl.when(ki < diag)(functools.partial(step, False))
    pl.when(ki == diag)(functools.partial(step, True))

    @pl.when(ki == pl.num_programs(2) - 1)
    def _():
        o = _bf(acc_scr[...] / l_scr[...])
        y = jnp.zeros((tq, MLA_HEADS * MLA_V), f32)
        for h in range(H):
            y = y + _dot(o[h * tq:(h + 1) * tq, :], wuv_ref[h])
        y_ref[...] = _bf(y)


def _flash_call(qcat, kcat, wuv, B, T):
    tq = 128
    tk = _pick_block(T, (512, 256, 128))
    nq, nk = T // tq, T // tk
    H = MLA_HEADS
    return pl.pallas_call(
        functools.partial(_flash_kernel, tq=tq, tk=tk), name="mla_flash",
        grid=(B, nq, nk),
        in_specs=[pl.BlockSpec((tq, H * 256), lambda b, i, j: (b * nq + i, 0)),
                  pl.BlockSpec((tk, 256), lambda b, i, j: (b * nk + jnp.minimum(j, (i * tq + tq - 1) // tk), 0)),
                  pl.BlockSpec((H, MLA_KV_RANK, H * MLA_V), lambda b, i, j: (0, 0, 0))],
        out_specs=pl.BlockSpec((tq, H * MLA_V), lambda b, i, j: (b * nq + i, 0)),
        out_shape=jax.ShapeDtypeStruct((B * T, H * MLA_V), bf16),
        scratch_shapes=[pltpu.VMEM((H * tq, 256), bf16), pltpu.VMEM((H * tq, 128), f32),
                        pltpu.VMEM((H * tq, 128), f32), pltpu.VMEM((H * tq, MLA_KV_RANK), f32)],
        compiler_params=_cparams("arbitrary", "arbitrary", "arbitrary"),
    )(qcat, kcat, wuv)


def _paged_kernel(pt_ref, q_ref, kn_ref, wuv_ref, ck_hbm, krt_hbm, y_ref,
                  ck_buf, krt_buf, kb_scr, sem, *, l, n_pages, page, TS, chunk):
    b = pl.program_id(0)
    nb = pl.num_programs(0)
    H = MLA_HEADS
    NR = TS * H
    n_keys = n_pages * page
    slot = b % 2

    def page_copies(bb, sl, p):
        pg = pt_ref[bb * n_pages + p]
        keys_ = pl.ds(pl.multiple_of(p * page, page), page)
        return (pltpu.make_async_copy(ck_hbm.at[l, pg], ck_buf.at[sl, keys_, :], sem.at[0, sl]),
                pltpu.make_async_copy(krt_hbm.at[l, pg], krt_buf.at[sl, :, keys_], sem.at[1, sl]))

    def start_all(bb, sl):
        def body(p, c):
            for cp in page_copies(bb, sl, p):
                cp.start()
            return c
        lax.fori_loop(0, n_pages, body, 0)

    def wait_all(bb, sl):
        def body(p, c):
            for cp in page_copies(bb, sl, p):
                cp.wait()
            return c
        lax.fori_loop(0, n_pages, body, 0)

    @pl.when(b == 0)
    def _():
        start_all(0, 0)

    @pl.when(b + 1 < nb)
    def _():
        start_all(b + 1, 1 - slot)

    wait_all(b, slot)

    q = q_ref[...]
    ql = q[:, 0:MLA_KV_RANK]
    qr = q[:, MLA_KV_RANK:MLA_KV_RANK + MLA_ROPE]
    kn = _bf(kn_ref[...])
    s_new = _dot_nt(q, kn) * MLA_SCALE
    ok = (_iota((NR, 8), 1) <= _iota((NR, 8), 0) // H) & (_iota((NR, 8), 1) < TS)
    s_new = jnp.where(ok, s_new, NEG)
    m = jnp.max(s_new, axis=1, keepdims=True)
    ss = []
    for c in range(n_keys // chunk):
        rows_ = pl.ds(c * chunk, chunk)
        kb_scr[rows_, :] = _bf(ck_buf[slot, rows_, :])
        s = (_dot_nt(ql, kb_scr[rows_, :]) + _dot(qr, _bf(krt_buf[slot, :, rows_]))) * MLA_SCALE
        m = jnp.maximum(m, jnp.max(s, axis=1, keepdims=True))
        ss.append(s)
    e = jnp.exp(s_new - m)
    lsum = jnp.sum(e, axis=1, keepdims=True)
    acc = _dot(_bf(e), kn[:, 0:MLA_KV_RANK])
    for c in range(n_keys // chunk):
        e = jnp.exp(ss[c] - m)
        lsum = lsum + jnp.sum(e, axis=1, keepdims=True)
        acc = acc + _dot(_bf(e), kb_scr[pl.ds(c * chunk, chunk), :])
    o = acc / lsum
    row_head = _iota((NR, MLA_KV_RANK), 0) % H
    y = jnp.zeros((NR, H * MLA_V), f32)
    for h in range(H):
        y = y + _dot(_bf(jnp.where(row_head == h, o, 0.0)), wuv_ref[h])
    y_ref[...] = _bf(jnp.sum(y.reshape(TS, H, H * MLA_V), axis=1))


def _paged_call(page_table, q3, kn3, wuv, cache_ckv, cache_krt, l):
    DB, n_pages = page_table.shape
    TS = q3.shape[1] // MLA_HEADS
    H = MLA_HEADS
    page = cache_ckv.shape[2]
    n_keys = n_pages * page
    chunk = _pick_block(n_keys, (2048, 1024, 512, 256, 128))
    grid_spec = pltpu.PrefetchScalarGridSpec(
        num_scalar_prefetch=1,
        grid=(DB,),
        in_specs=[pl.BlockSpec((None, TS * H, 256), lambda b, pt: (b, 0, 0)),
                  pl.BlockSpec((None, 8, 256), lambda b, pt: (b, 0, 0)),
                  pl.BlockSpec((H, MLA_KV_RANK, H * MLA_V), lambda b, pt: (0, 0, 0)),
                  pl.BlockSpec(memory_space=pl.ANY), pl.BlockSpec(memory_space=pl.ANY)],
        out_specs=pl.BlockSpec((None, TS, H * MLA_V), lambda b, pt: (b, 0, 0)),
        scratch_shapes=[pltpu.VMEM((2, n_keys, MLA_KV_RANK), f32),
                        pltpu.VMEM((2, MLA_ROPE, n_keys), f32),
                        pltpu.VMEM((n_keys, MLA_KV_RANK), bf16),
                        pltpu.SemaphoreType.DMA((2, 2))],
    )
    return pl.pallas_call(
        functools.partial(_paged_kernel, l=l, n_pages=n_pages, page=page, TS=TS, chunk=chunk), name="mla_paged",
        grid_spec=grid_spec,
        out_shape=jax.ShapeDtypeStruct((DB, TS, H * MLA_V), bf16),
        compiler_params=_cparams("arbitrary"),
    )(page_table.reshape(-1), q3, kn3, wuv, cache_ckv, cache_krt)


def _merge_kernel(h_ref, ya_ref, yb_ref, yc_ref, yd_ref, g0, g1, g2, g3, b0, b1, b2, b3, o_ref):
    h = h_ref[...]
    acc = None
    for y_ref, g_ref, b_ref in ((ya_ref, g0, b0), (yb_ref, g1, b1), (yc_ref, g2, b2), (yd_ref, g3, b3)):
        gate = _sigmoid(_dot_nt(h, _bf(g_ref[...])))
        term = gate * _dot(y_ref[...], _bf(b_ref[...]))
        acc = term if acc is None else acc + term
    o_ref[...] = _bf(acc)


def _merge_call(h1, ys, w_in_t, w_br, l):
    N = h1.shape[0]
    D = D_MODEL
    tm = _pick_block(N, (1088, 640, 512, 256, 128))
    tn = 256
    ncol = D // tn
    gate_specs = [pl.BlockSpec((None, tn, D), (lambda i, j, br=br: (l, br * ncol + j, 0))) for br in range(N_BRANCH)]
    br_specs = [pl.BlockSpec((None, None, BRANCH_WIDTH, tn), (lambda i, j, br=br: (l, br, 0, j))) for br in range(N_BRANCH)]
    y_specs = [pl.BlockSpec((tm, BRANCH_WIDTH), lambda i, j: (i, 0)) for _ in range(N_BRANCH)]
    return pl.pallas_call(
        _merge_kernel, name="merge",
        grid=(N // tm, ncol),
        in_specs=[pl.BlockSpec((tm, D), lambda i, j: (i, 0))] + y_specs + gate_specs + br_specs,
        out_specs=pl.BlockSpec((tm, tn), lambda i, j: (i, j)),
        out_shape=jax.ShapeDtypeStruct((N, D), bf16),
        compiler_params=_cparams("arbitrary", "arbitrary"),
    )(h1, *ys, w_in_t, w_in_t, w_in_t, w_in_t, w_br, w_br, w_br, w_br)


def _outproj_kernel(x_ref, mg_ref, wo_ref, g_ref, g1s, g1p, shs, shp, scs, scp, xo_ref, h_ref, hb_ref, *, rows):
    i = pl.program_id(0)
    x = x_ref[...] + rows.pick(i, g1s, g1p) * _dot(mg_ref[...], wo_ref[...])
    xo_ref[...] = x
    h = _rms(x, g_ref[...]) * (1.0 + rows.pick(i, scs, scp)) + rows.pick(i, shs, shp)
    h_ref[...] = h
    hb_ref[...] = _bf(h)


def _outproj_call(rows, x, merged, wo_bf, gain2, mod, l):
    R = rows.R
    D = D_MODEL
    g1s, g1p = rows.mod_specs(l, 2)
    shs, shp = rows.mod_specs(l, 3)
    scs, scp = rows.mod_specs(l, 4)
    return pl.pallas_call(
        functools.partial(_outproj_kernel, rows=rows), name="outproj",
        grid=(rows.nb,),
        in_specs=[pl.BlockSpec((R, D), lambda i: (i, 0)), pl.BlockSpec((R, D), lambda i: (i, 0)),
                  pl.BlockSpec((D, D), lambda i: (0, 0)), pl.BlockSpec((1, D), lambda i: (0, 0)),
                  g1s, g1p, shs, shp, scs, scp],
        out_specs=[pl.BlockSpec((R, D), lambda i: (i, 0))] * 3,
        out_shape=[jax.ShapeDtypeStruct((rows.N, D), f32), jax.ShapeDtypeStruct((rows.N, D), f32),
                   jax.ShapeDtypeStruct((rows.N, D), bf16)],
        compiler_params=_cparams("arbitrary"),
    )(x, merged, wo_bf, gain2.reshape(1, D), mod, mod, mod, mod, mod, mod)


def _peer_candidates():
    K = PEER_TOPK
    pairs = [(k1, k2) for k1 in range(K) for k2 in range(K) if (k1 + 1) * (k2 + 1) <= K]
    n = -(-len(pairs) // 8) * 8
    g1 = np.zeros((n, K), np.float32)
    g2 = np.zeros((n, K), np.float32)
    for r, (k1, k2) in enumerate(pairs):
        g1[r, k1] = 1.0
        g2[r, k2] = 1.0
    return len(pairs), jnp.asarray(g1, bf16), jnp.asarray(g2, bf16)


def _select_rows3(sel_bf, v):
    hi = _bf(v)
    r1 = v - hi.astype(f32)
    mid = _bf(r1)
    lo = _bf(r1 - mid.astype(f32))
    return (_dot(sel_bf, hi) + _dot(sel_bf, mid)) + _dot(sel_bf, lo)


def _peer_score_kernel(h_ref, wqh_ref, wql_ref, k1_ref, k2_ref, g1_ref, g2_ref, a_ref, b_ref, g_ref,
                       v1_scr, i1_scr, v2_scr, i2_scr, top_scr, ea_scr, eb_scr, *, n_cand):
    T = h_ref.shape[0]
    NK, K = PEER_NKEYS, PEER_TOPK
    LT = 128
    NC = g1_ref.shape[0]
    hh, hl = _split2(h_ref[...])
    q = _dot(hh, wqh_ref[...]) + (_dot(hh, wql_ref[...]) + _dot(hl, wqh_ref[...]))
    key_idx = _iota((NK, LT), 0).astype(f32)
    cand_idx = _iota((NC, LT), 0).astype(f32)
    neg_inf = -jnp.inf
    g1 = g1_ref[...]
    g2 = g2_ref[...]

    def topk_into(s, val_scr, idx_scr):
        for kk in range(K):
            m = jnp.max(s, axis=0, keepdims=True)
            pos = jnp.min(jnp.where(s == m, key_idx, float(NK)), axis=0, keepdims=True)
            val_scr[pl.ds(kk, 1), :] = m
            idx_scr[pl.ds(kk, 1), :] = pos
            s = jnp.where(key_idx == pos, neg_inf, s)

    for h in range(PEER_HEADS):
        for lt in range(T // LT):
            cols = slice(lt * LT, (lt + 1) * LT)
            qh, ql = _split2(q[lt * LT:(lt + 1) * LT, h * PEER_DKEY:(h + 1) * PEER_DKEY])
            for k_ref, val_scr, idx_scr in ((k1_ref, v1_scr, i1_scr), (k2_ref, v2_scr, i2_scr)):
                kh, kl = _split2(k_ref[h])
                sT = _dot_nt(kh, qh) + (_dot_nt(kh, ql) + _dot_nt(kl, qh))
                topk_into(sT, val_scr, idx_scr)
            cand = _select_rows3(g1, v1_scr[...]) + _select_rows3(g2, v2_scr[...])
            cand = jnp.where(cand_idx < float(n_cand), cand, neg_inf)
            ca = _dot(g1, _bf(i1_scr[...]))
            cb = _dot(g2, _bf(i2_scr[...]))
            for kk in range(K):
                m = jnp.max(cand, axis=0, keepdims=True)
                pos = jnp.min(jnp.where(cand == m, cand_idx, float(NC)), axis=0, keepdims=True)
                sel = cand_idx == pos
                r = h * K + kk
                top_scr[pl.ds(r, 1), cols] = m
                ea_scr[pl.ds(r, 1), cols] = jnp.sum(jnp.where(sel, ca, 0.0), axis=0, keepdims=True)
                eb_scr[pl.ds(r, 1), cols] = jnp.sum(jnp.where(sel, cb, 0.0), axis=0, keepdims=True)
                cand = jnp.where(sel, neg_inf, cand)
        tops = top_scr[pl.ds(h * K, K), :]
        e = jnp.exp(tops - jnp.max(tops, axis=0, keepdims=True))
        top_scr[pl.ds(h * K, K), :] = e / jnp.sum(e, axis=0, keepdims=True)
    g_ref[...] = top_scr[...].T
    a_ref[...] = ea_scr[...].T
    b_ref[...] = eb_scr[...].T


def _peer_score_call(rows, h2, wqh, wql, k1p, k2p):
    R = rows.R
    D = D_MODEL
    HK = PEER_HEADS * PEER_TOPK
    o_spec = pl.BlockSpec((R, HK), lambda i: (i, 0))
    o_shape = jax.ShapeDtypeStruct((rows.N, HK), f32)
    n_cand, g1, g2 = _peer_candidates()
    return pl.pallas_call(
        functools.partial(_peer_score_kernel, n_cand=n_cand), name="peer_score",
        grid=(rows.nb,),
        in_specs=[pl.BlockSpec((R, D), lambda i: (i, 0)),
                  pl.BlockSpec((D, PEER_HEADS * PEER_DKEY), lambda i: (0, 0)),
                  pl.BlockSpec((D, PEER_HEADS * PEER_DKEY), lambda i: (0, 0)),
                  pl.BlockSpec((PEER_HEADS, PEER_NKEYS, PEER_DKEY), lambda i: (0, 0, 0)),
                  pl.BlockSpec((PEER_HEADS, PEER_NKEYS, PEER_DKEY), lambda i: (0, 0, 0)),
                  pl.BlockSpec(g1.shape, lambda i: (0, 0)), pl.BlockSpec(g2.shape, lambda i: (0, 0))],
        out_specs=[o_spec, o_spec, o_spec],
        out_shape=[o_shape, o_shape, o_shape],
        scratch_shapes=[pltpu.VMEM((PEER_TOPK, 128), f32)] * 4 + [pltpu.VMEM((HK, R), f32)] * 3,
        compiler_params=_cparams("arbitrary"),
    )(h2, wqh, wql, k1p, k2p, g1, g2)


def _peer_mix_kernel(x_ref, h_ref, ea_ref, eb_ref, g_ref, u_ref, v_ref, g2s, g2p, o_ref,
                     a_scr, acc_scr, *, rows, T, nch, ac, n_half):
    i = pl.program_id(0)
    j = pl.program_id(1)
    NK = PEER_NKEYS
    NA = NK // n_half
    pitch = T + PEER_PITCH_PAD
    half = j // (2 * nch)
    jj = j % (2 * nch)

    @pl.when(j == 0)
    def _():
        acc_scr[...] = jnp.zeros(acc_scr.shape, f32)

    @pl.when(jj < nch)
    def _():
        res = _dot_nt(h_ref[...], u_ref[...])
        for al in range(ac):
            start = pl.multiple_of((jj * ac + al) * pitch, 8)
            a_scr[pl.ds(start, T), :] = res[:, al * NK:(al + 1) * NK]

    @pl.when(jj == nch)
    def _():
        sub_a = (_iota((NA, NK), 0) + half * NA).astype(f32).astype(bf16)
        sub_b = _iota((NK, NK), 0).astype(f32).astype(bf16)
        one = jnp.ones((NA, NK), bf16)
        zero_a = jnp.zeros((NA, NK), bf16)
        zero_b = jnp.zeros((NK, NK), bf16)

        def tok(t):
            ia = jnp.broadcast_to(_bf(ea_ref[pl.ds(t, 1), :]), (NA, NK))
            ib = jnp.broadcast_to(_bf(eb_ref[pl.ds(t, 1), :]), (NK, NK))
            gg = jnp.broadcast_to(_bf(g_ref[pl.ds(t, 1), :]), (NK, NK))
            onehot_a = jnp.where(sub_a == ia, one, zero_a)
            gate_b = jnp.where(sub_b == ib, gg, zero_b)
            w = _dot_nt(onehot_a, gate_b)
            at = a_scr[pl.ds(t, NA, stride=pitch), :]
            a_scr[pl.ds(t, NA, stride=pitch), :] = w * _gelu(at)

        def tok_group(tg, carry):
            for u in range(PEER_TOKEN_UNROLL):
                tok(tg * PEER_TOKEN_UNROLL + u)
            return carry

        lax.fori_loop(0, T // PEER_TOKEN_UNROLL, tok_group, 0)

    @pl.when(jj >= nch)
    def _():
        parts = []
        for al in range(ac):
            start = pl.multiple_of(((jj - nch) * ac + al) * pitch, 8)
            parts.append(_bf(a_scr[pl.ds(start, T), :]))
        acc_scr[...] += _dot(jnp.concatenate(parts, axis=1), v_ref[...])

    @pl.when(j == pl.num_programs(1) - 1)
    def _():
        o_ref[...] = x_ref[...] + rows.pick(i, g2s, g2p) * acc_scr[...]


def _peer_mix_call(rows, x, h2b, ea, eb, g, ub_all, vb_all, mod, l):
    T = rows.R
    D = D_MODEL
    NK = PEER_NKEYS
    n_half = 2
    ac = 4
    nch = NK // n_half // ac
    HK = PEER_HEADS * PEER_TOPK
    R, nPb, nSb, Ns = rows.R, rows.nPb, rows.nSb, rows.Ns
    once = pl.Buffered(1)
    g2s = pl.BlockSpec((None, R, D), lambda i, j: (l, jnp.clip(i - nPb, 0, nSb - 1), 5), pipeline_mode=once)
    g2p = pl.BlockSpec((None, 8, D), lambda i, j: (l, Ns // 8, 5))
    row = lambda i, j: (i, 0)

    def u_idx(i, j):
        return (l, (j // (2 * nch)) * nch + jnp.minimum(j % (2 * nch), nch - 1), 0)

    def v_idx(i, j):
        return (l, (j // (2 * nch)) * nch + jnp.maximum(j % (2 * nch) - nch, 0), 0)

    return pl.pallas_call(
        functools.partial(_peer_mix_kernel, rows=rows, T=T, nch=nch, ac=ac, n_half=n_half), name="peer_mix",
        grid=(rows.nb, n_half * 2 * nch),
        in_specs=[pl.BlockSpec((T, D), row, pipeline_mode=once), pl.BlockSpec((T, D), row),
                  pl.BlockSpec((T, HK), row), pl.BlockSpec((T, HK), row), pl.BlockSpec((T, HK), row),
                  pl.BlockSpec((None, ac * NK, D), u_idx),
                  pl.BlockSpec((None, ac * NK, D), v_idx),
                  g2s, g2p],
        out_specs=pl.BlockSpec((T, D), row),
        out_shape=jax.ShapeDtypeStruct((rows.N, D), f32),
        scratch_shapes=[pltpu.VMEM((NK // n_half * (T + PEER_PITCH_PAD), NK), f32),
                        pltpu.VMEM((T, D), f32)],
        compiler_params=_cparams("arbitrary", "arbitrary"),
    )(x, h2b, ea, eb, g, ub_all, vb_all, mod, mod)


def _final_kernel(x_ref, g_ref, o_ref):
    o_ref[...] = _rms(x_ref[...], g_ref[...])


def _final_call(rows, x, gain):
    R, D = rows.R, D_MODEL
    return pl.pallas_call(
        _final_kernel, name="final_norm", grid=(rows.nb,),
        in_specs=[pl.BlockSpec((R, D), lambda i: (i, 0)), pl.BlockSpec((1, D), lambda i: (0, 0))],
        out_specs=pl.BlockSpec((R, D), lambda i: (i, 0)),
        out_shape=jax.ShapeDtypeStruct((rows.N, D), f32),
        compiler_params=_cparams("arbitrary"),
    )(x, gain.reshape(1, D))


def _mla_constants():
    H, R = MLA_HEADS, MLA_ROPE
    half = R // 2
    rot = np.zeros((H * R, H * R), np.float32)
    for g in range(H):
        for jj in range(half):
            rot[g * R + half + jj, g * R + jj] = -1.0
            rot[g * R + jj, g * R + half + jj] = 1.0
    place = np.zeros((H * R, H * 256), np.float32)
    for g in range(H):
        for jj in range(R):
            place[g * R + jj, g * 256 + MLA_KV_RANK + jj] = 1.0
    return jnp.asarray(rot, bf16), jnp.asarray(place, bf16)


def kernel(x_prompt, x_sample, cache_mla_ckv, cache_mla_kr, state_gla, state_mlstm_C, state_mlstm_n, state_mlstm_m, page_table, c_prompt, c_sample, w_ada, b_ada, norm1, norm2, w_in, gla_wa2, gla_ba, gla_norm, mla_w_uq, mla_gq, mla_gkv, mla_w_uk, mla_w_uv, ml_bi, ml_bf, ml_norm, sg_gv, sg_ws, sg_bs, w_br, w_o, peer_wq, peer_k1, peer_k2, peer_u, peer_v, final_norm):
    D = D_MODEL
    B, T, _ = x_prompt.shape
    DB, TS, _ = x_sample.shape
    depth = w_ada.shape[0]
    n_pages = page_table.shape[1]
    page = cache_mla_ckv.shape[2]
    past_len = n_pages * page
    rows = _Rows(B, T, DB, TS)
    Np, Ns = rows.Np, rows.Ns
    assert Ns % 128 == 0 and T % 128 == 0 and TS == 4

    c_rows = jnp.concatenate([jnp.repeat(c_sample, TS, axis=0), c_prompt,
                              jnp.zeros((16 - B, D), f32)], axis=0)
    mod = _ada_call(c_rows, w_ada, b_ada)

    x = jnp.concatenate([x_prompt.reshape(Np, D), x_sample.reshape(Ns, D)], axis=0)

    inv = ROPE_THETA ** (-jnp.arange(MLA_ROPE // 2, dtype=f32) / (MLA_ROPE // 2))
    pos_p = jnp.arange(T, dtype=jnp.int32)
    pos_s = past_len + jnp.arange(TS, dtype=jnp.int32)

    def table(fn):
        tp = fn(pos_p.astype(f32)[:, None] * inv)
        ts = fn(pos_s.astype(f32)[:, None] * inv)
        rows_ = jnp.concatenate([jnp.tile(tp, (B, 1)), jnp.tile(ts, (DB, 1))], axis=0)
        return jnp.tile(rows_, (1, 2 * MLA_HEADS))

    cos_t, sin_t = table(jnp.cos), table(jnp.sin)
    rot, place = _mla_constants()
    eye_h = jnp.eye(MLA_HEADS, dtype=f32)
    Ls = math.gcd(TS, GLA_CHUNK)
    Lp_gla = math.gcd(T, GLA_CHUNK)
    Lp_ml = math.gcd(T, ML_CHUNK)
    Ls_ml = math.gcd(TS, ML_CHUNK)
    spb = SG_CHUNK // TS
    w_in_t = jnp.swapaxes(w_in, 1, 2)
    cache_krt = jnp.swapaxes(cache_mla_kr, 2, 3)
    ub_all = _bf(peer_u)
    vb_all = _bf(peer_v)
    rows_peer = _Rows(B, T, DB, TS, block_sizes=(512, 256, 128))

    new_p, new_s = [], []
    for l in range(depth):
        wl = w_in_t[l]
        g0 = N_BRANCH * D
        zr = lambda n: jnp.zeros((n, D), f32)
        wA = _bf(jnp.concatenate([wl[g0:g0 + 1552], zr(112)], axis=0))
        wB = _bf(jnp.concatenate([wl[g0 + 1552:g0 + 2096], zr(96)], axis=0))
        wC = _bf(jnp.concatenate([wl[g0 + 2096:g0 + 3632], wl[g0 + 3632:g0 + 3636], zr(124),
                                  wl[g0 + 3636:g0 + 3640], zr(124)], axis=0))
        wD = _bf(wl[g0 + 3640:g0 + 4664])
        wa2p = jnp.concatenate([gla_wa2[l], jnp.zeros((128 - GLA_RANK, GLA_HEADS * GLA_DK), f32)], axis=0)
        uq = mla_w_uq[l].reshape(MLA_Q_RANK, MLA_HEADS, MLA_NOPE + MLA_ROPE)
        wqn = _bf(uq[:, :, :MLA_NOPE].reshape(MLA_Q_RANK, MLA_HEADS * MLA_NOPE))
        wqr = _bf(uq[:, :, MLA_NOPE:].reshape(MLA_Q_RANK, MLA_HEADS * MLA_ROPE))
        uk = jnp.transpose(mla_w_uk[l], (1, 2, 0))
        wa = jnp.einsum('hnr,hg->hngr', uk, eye_h)
        wa = _bf(jnp.pad(wa, ((0, 0), (0, 0), (0, 0), (0, 256 - MLA_KV_RANK))).reshape(MLA_HEADS * MLA_NOPE, MLA_HEADS * 256))
        wuv = _bf(jnp.einsum('rhv,hg->hrgv', mla_w_uv[l], eye_h).reshape(MLA_HEADS, MLA_KV_RANK, MLA_HEADS * MLA_V))
        pad128 = lambda a: jnp.pad(a, (0, 128 - a.shape[0])).reshape(1, 128)
        bcol_p = jnp.pad(sg_bs[l].T, ((0, 0), (0, 128 - SG_GROUPS)))
        ws_s = jnp.stack([jnp.kron(jnp.eye(spb, dtype=f32), sg_ws[l, g, :TS, :TS]) for g in range(SG_GROUPS)])
        bcol_s = jnp.pad(jnp.tile(sg_bs[l][:, :TS].T, (spb, 1)), ((0, 0), (0, 128 - SG_GROUPS)))
        wo_bf = _bf(w_o[l])
        wq = peer_wq[l]
        wqh = _bf(wq)
        wql = _bf(wq - wqh.astype(f32))
        half = PEER_DKEY // 2
        k1p = jnp.pad(peer_k1[l], ((0, 0), (0, 0), (0, half)))
        k2p = jnp.pad(peer_k2[l], ((0, 0), (0, 0), (half, 0)))

        h1 = _modnorm_call(rows, x, norm1[l], mod, l, 0, 1)
        zA, zB, zC, zD = (_mm_call(h1, w) for w in (wA, wB, wC, wD))

        gn_a = gla_norm[l].reshape(1, -1)
        ba = gla_ba[l].reshape(1, -1)
        ya_p, S_p = _gla_call(zA, 0, B, T, Lp_gla, wa2p, ba, gn_a, None)
        ya_s, S_s = _gla_call(zA, Np, DB, TS, Ls, wa2p, ba, gn_a, state_gla[l])

        qcat, kcat = _mla_prep_call(rows, zB, cos_t, sin_t, mla_gq[l].reshape(1, -1), mla_gkv[l].reshape(1, -1),
                                    wqn, wqr, wa, place, rot)
        yb_p = _flash_call(qcat, kcat, wuv, B, T)
        q3 = qcat[Np:].reshape(DB, TS * MLA_HEADS, 256)
        kn3 = jnp.pad(kcat[Np:].reshape(DB, TS, 256), ((0, 0), (0, 8 - TS), (0, 0)))
        yb_s = _paged_call(page_table, q3, kn3, wuv, cache_mla_ckv, cache_krt, l).reshape(Ns, -1)

        bi, bfg, gn_c = pad128(ml_bi[l]), pad128(ml_bf[l]), ml_norm[l].reshape(1, -1)
        yc_p, C_p, n_p, m_p = _mlstm_call(zC, 0, B, T, Lp_ml, bi, bfg, gn_c, None)
        st = (state_mlstm_C[l], state_mlstm_n[l].reshape(DB, 1, -1),
              jnp.pad(state_mlstm_m[l], ((0, 0), (0, 128 - ML_HEADS))).reshape(DB, 1, 128))
        yc_s, C_s, n_s, m_s = _mlstm_call(zC, Np, DB, TS, Ls_ml, bi, bfg, gn_c, st)

        gv = sg_gv[l].reshape(1, -1)
        (yd_p,) = _sgu_call(zD, 0, Np, gv, sg_ws[l], bcol_p, False)
        yd_s, v_s = _sgu_call(zD, Np, Ns, gv, ws_s, bcol_s, True)

        ys = [jnp.concatenate([p_, s_], axis=0) for p_, s_ in ((ya_p, ya_s), (yb_p, yb_s), (yc_p, yc_s), (yd_p, yd_s))]
        merged = _merge_call(h1, ys, w_in_t, w_br, l)
        x, h2, h2b = _outproj_call(rows, x, merged, wo_bf, norm2[l], mod, l)

        ea, eb, g = _peer_score_call(rows, h2, wqh, wql, k1p, k2p)
        x = _peer_mix_call(rows_peer, x, h2b, ea, eb, g, ub_all, vb_all, mod, l)

        new_p.append(dict(ckv=kcat[:Np, :MLA_KV_RANK].reshape(B, T, -1),
                          kr=kcat[:Np, MLA_KV_RANK:MLA_KV_RANK + MLA_ROPE].reshape(B, T, -1),
                          S=S_p, C=C_p, n=n_p.reshape(B, ML_HEADS, ML_DK), m=m_p[:, 0, :ML_HEADS]))
        new_s.append(dict(ckv=kcat[Np:, :MLA_KV_RANK].reshape(DB, TS, -1),
                          kr=kcat[Np:, MLA_KV_RANK:MLA_KV_RANK + MLA_ROPE].reshape(DB, TS, -1),
                          S=S_s, C=C_s, n=n_s.reshape(DB, ML_HEADS, ML_DK), m=m_s[:, 0, :ML_HEADS],
                          v=v_s.reshape(DB, TS, -1)))

    y = _final_call(rows, x, final_norm)
    stk = lambda outs, name: jnp.stack([o[name] for o in outs], axis=0)
    return (y[:Np].reshape(B, T, D), y[Np:].reshape(DB, TS, D),
            stk(new_p, 'ckv'), stk(new_p, 'kr'), stk(new_p, 'S'), stk(new_p, 'C'), stk(new_p, 'n'), stk(new_p, 'm'),
            stk(new_s, 'ckv'), stk(new_s, 'kr'), stk(new_s, 'S'), stk(new_s, 'C'), stk(new_s, 'n'), stk(new_s, 'm'),
            stk(new_s, 'v'))
```

```python
import functools
import math

import numpy as np
import jax
import jax.numpy as jnp
from jax import lax
from jax.experimental import pallas as pl
from jax.experimental.pallas import tpu as pltpu

f32 = jnp.float32
bf16 = jnp.bfloat16

D_MODEL = 2048
EPS = 1e-6
ADA_CHUNKS = 6
GLA_HEADS, GLA_DK, GLA_DV, GLA_RANK, GLA_TAU, GLA_CHUNK = 4, 64, 128, 16, 16.0, 16
MLA_HEADS, MLA_Q_RANK, MLA_KV_RANK, MLA_NOPE, MLA_ROPE, MLA_V = 8, 384, 128, 64, 32, 64
MLA_SCALE = (MLA_NOPE + MLA_ROPE) ** -0.5
ROPE_THETA = 10000.0
ML_HEADS, ML_DK, ML_DV, ML_CHUNK = 4, 64, 128, 64
SG_GROUPS, SG_CHUNK, SG_WIDTH = 4, 128, 512
PEER_HEADS, PEER_NKEYS, PEER_DKEY, PEER_TOPK = 8, 128, 128, 16
PEER_EXPERTS = PEER_NKEYS * PEER_NKEYS
BRANCH_WIDTH = 512
N_BRANCH = 4

W_A = 1664
W_B = 640
W_C = 1792
W_D = 1024

NEG = -1e30
VMEM_LIMIT_V7X = 56 * 1024 * 1024
PEER_PITCH_PAD = 8
PEER_TOKEN_UNROLL = 16

def _cparams(*sem):
    return pltpu.CompilerParams(dimension_semantics=sem, vmem_limit_bytes=VMEM_LIMIT_V7X)


def _bf(x):
    return x.astype(bf16)


def _dot(a, b):
    return jnp.dot(a, b, preferred_element_type=f32)


def _dot_nt(a, b):
    return lax.dot_general(a, b, (((1,), (1,)), ((), ())), preferred_element_type=f32)


def _split2(x):
    hi = _bf(x)
    lo = _bf(x - hi.astype(f32))
    return hi, lo


def _dot3(a, b):
    ah, al = _split2(a)
    bh, bl = _split2(b)
    return _dot(ah, bh) + (_dot(ah, bl) + _dot(al, bh))


def _dot2b(a, b_bf):
    ah, al = _split2(a)
    return _dot(ah, b_bf) + _dot(al, b_bf)


def _sigmoid(x):
    return 1.0 / (1.0 + jnp.exp(-x))


def _log_sigmoid(x):
    return jnp.minimum(x, 0.0) - jnp.log1p(jnp.exp(-jnp.abs(x)))


def _gelu(x):
    return jax.nn.gelu(x)


def _rms(x, g):
    return x * lax.rsqrt(jnp.mean(x * x, axis=-1, keepdims=True) + EPS) * g


def _iota(shape, dim):
    return lax.broadcasted_iota(jnp.int32, shape, dim)


def _seg_cumsum(x, rowmod, L):
    s = 1
    while s < L:
        x = x + jnp.where(rowmod >= s, pltpu.roll(x, s, 0), 0.0)
        s *= 2
    return x


def _seg_last(x, rowmod, L):
    R = x.shape[0]
    s = 1
    while s < L:
        x = jnp.where(rowmod + s < L, pltpu.roll(x, R - s, 0), x)
        s *= 2
    return x


def _expand_heads(x, n_heads, width):
    R = x.shape[0]
    lane_head = _iota((R, n_heads * width), 1) // width
    out = jnp.zeros((R, n_heads * width), f32)
    for h in range(n_heads):
        out = jnp.where(lane_head == h, x[:, h:h + 1], out)
    return out


def _pick_block(n, cands):
    for c in cands:
        if n % c == 0:
            return c
    raise ValueError(f"no block size in {cands} divides {n}")


def _ada_kernel(c_ref, w_ref, b_ref, o_ref):
    c = c_ref[...]
    a = _bf(c * _sigmoid(c))
    o_ref[...] = _dot(a, _bf(w_ref[...])) + b_ref[...]


def _ada_call(c_rows, w_ada, b_ada):
    L, D, N6 = w_ada.shape
    Rc = c_rows.shape[0]
    tn = 1024
    return pl.pallas_call(
        _ada_kernel, name="ada_mod",
        grid=(L, N6 // tn),
        in_specs=[pl.BlockSpec((Rc, D), lambda l, j: (0, 0)),
                  pl.BlockSpec((None, D, tn), lambda l, j: (l, 0, j)),
                  pl.BlockSpec((None, 1, tn), lambda l, j: (l, 0, j))],
        out_specs=pl.BlockSpec((None, Rc, tn), lambda l, j: (l, 0, j)),
        out_shape=jax.ShapeDtypeStruct((L, Rc, N6), f32),
        compiler_params=_cparams("arbitrary", "arbitrary"),
    )(c_rows, w_ada, b_ada.reshape(L, 1, N6))


class _Rows:
    def __init__(self, B, T, DB, TS, block_sizes=(256, 128)):
        self.B, self.T, self.DB, self.TS = B, T, DB, TS
        self.Np, self.Ns = B * T, DB * TS
        self.N = self.Np + self.Ns
        self.R = _pick_block(math.gcd(T, self.Ns), block_sizes)
        self.nPb = self.Np // self.R
        self.nSb = self.Ns // self.R
        self.nb = self.nPb + self.nSb
        self.blocks_per_seq = T // self.R

    def mod_specs(self, l, chunk):
        R, nPb, nSb, Ns = self.R, self.nPb, self.nSb, self.Ns
        s = pl.BlockSpec((None, R, D_MODEL), lambda i: (l, jnp.clip(i - nPb, 0, nSb - 1), chunk))
        p = pl.BlockSpec((None, 8, D_MODEL), lambda i: (l, Ns // 8, chunk))
        return s, p

    def pick(self, i, s_ref, p_ref):
        b = jnp.minimum(i // self.blocks_per_seq, self.B - 1)
        return jnp.where(i >= self.nPb, s_ref[...], p_ref[pl.ds(b, 1), :])


def _modnorm_kernel(*refs, rows, with_peer):
    i = pl.program_id(0)
    if with_peer:
        x_ref, p_ref, g2s, g2p, g_ref, shs_ref, shp_ref, scs_ref, scp_ref, xo_ref, h_ref = refs
        x = x_ref[...] + rows.pick(i, g2s, g2p) * p_ref[...]
        xo_ref[...] = x
    else:
        x_ref, g_ref, shs_ref, shp_ref, scs_ref, scp_ref, h_ref = refs
        x = x_ref[...]
    sh = rows.pick(i, shs_ref, shp_ref)
    sc = rows.pick(i, scs_ref, scp_ref)
    h_ref[...] = _bf(_rms(x, g_ref[...]) * (1.0 + sc) + sh)


def _modnorm_call(rows, x, peer_out, gain, mod, l):
    R = rows.R
    shs, shp = rows.mod_specs(l, 0)
    scs, scp = rows.mod_specs(l, 1)
    row = pl.BlockSpec((R, D_MODEL), lambda i: (i, 0))
    vec = pl.BlockSpec((1, D_MODEL), lambda i: (0, 0))
    h_shape = jax.ShapeDtypeStruct((rows.N, D_MODEL), bf16)
    if peer_out is None:
        in_specs = [row, vec, shs, shp, scs, scp]
        args = (x, gain.reshape(1, D_MODEL), mod, mod, mod, mod)
        out_specs, out_shape = row, h_shape
    else:
        g2s, g2p = rows.mod_specs(l - 1, 5)
        in_specs = [row, row, g2s, g2p, vec, shs, shp, scs, scp]
        args = (x, peer_out, mod, mod, gain.reshape(1, D_MODEL), mod, mod, mod, mod)
        out_specs = [row, row]
        out_shape = [jax.ShapeDtypeStruct((rows.N, D_MODEL), f32), h_shape]
    return pl.pallas_call(
        functools.partial(_modnorm_kernel, rows=rows, with_peer=peer_out is not None), name="modnorm",
        grid=(rows.nb,), in_specs=in_specs, out_specs=out_specs, out_shape=out_shape,
        compiler_params=_cparams("arbitrary"),
    )(*args)


def _mm_kernel(x_ref, wt_ref, o_ref):
    o_ref[...] = _dot_nt(x_ref[...], wt_ref[...]).astype(o_ref.dtype)


def _mm_call(x, wt, out_dtype=f32):
    M, K = x.shape
    N = wt.shape[0]
    tm = _pick_block(M, (1088, 1024, 640, 512, 256, 128))
    return pl.pallas_call(
        _mm_kernel, name="mixer_proj",
        grid=(M // tm,),
        in_specs=[pl.BlockSpec((tm, K), lambda i: (i, 0)),
                  pl.BlockSpec((N, K), lambda i: (0, 0))],
        out_specs=pl.BlockSpec((tm, N), lambda i: (i, 0)),
        out_shape=jax.ShapeDtypeStruct((M, N), out_dtype),
        compiler_params=_cparams("arbitrary"),
    )(x, wt)


def _gla_kernel(*refs, R, L, carry):
    if carry:
        z_ref, wa2_ref, ba_ref, gn_ref, y_ref, so_ref, s_scr = refs
    else:
        z_ref, wa2_ref, ba_ref, gn_ref, s0_ref, y_ref, so_ref = refs
    H, DK, DV = GLA_HEADS, GLA_DK, GLA_DV
    HK, HV = H * DK, H * DV
    nchunk = R // L
    z = z_ref[...]
    q = z[:, 0:HK] * (DK ** -0.5)
    k = z[:, HK:2 * HK]
    v = z[:, 2 * HK:2 * HK + HV]
    r = z[:, 2 * HK + HV:2 * HK + 2 * HV]
    ag = z[:, 2 * HK + 2 * HV:2 * HK + 2 * HV + 128]
    la = _log_sigmoid(_dot3(ag, wa2_ref[...]) + ba_ref[...]) * (1.0 / GLA_TAU)
    rowmod = _iota((R, HK), 0) % L
    b = _seg_cumsum(la, rowmod, L)
    tot = _seg_last(b, rowmod, L)

    sel_row_head = _iota((HK, 128), 0) // DK
    sel_lane = _iota((HK, 128), 1)
    att = jnp.zeros((R, 128), f32)
    for d in range(L):
        ks = k if d == 0 else pltpu.roll(k, d, 0)
        bs = b if d == 0 else pltpu.roll(b, d, 0)
        p = q * ks * jnp.exp(jnp.where(rowmod >= d, b - bs, NEG))
        att = att + _dot2b(p, _bf(sel_lane == d * H + sel_row_head))
    ex_row = _iota((128, HV), 0)
    ex_lane_head = _iota((128, HV), 1) // DV
    o = jnp.zeros((R, HV), f32)
    for d in range(L):
        vs = v if d == 0 else pltpu.roll(v, d, 0)
        o = o + _dot2b(att, _bf(ex_row == d * H + ex_lane_head)) * vs

    qt = _bf(q * jnp.exp(b))
    kd = k * jnp.exp(tot - b)
    kdT = _bf(kd.T)
    totT = tot.T
    vb = _bf(v)
    blockdiag = (_iota((HK, HV), 0) // DK) == (_iota((HK, HV), 1) // DV)
    row_chunk = _iota((R, HK), 0) // L
    col_chunk = _iota((HK, R), 1) // L
    if carry:
        @pl.when(pl.program_id(1) == 0)
        def _():
            s_scr[...] = jnp.zeros((HK, HV), f32)
        S = s_scr[...]
    for c in range(nchunk):
        if not carry:
            rows_ = [jnp.concatenate([s0_ref[c, h] if g == h else jnp.zeros((DK, DV), f32) for g in range(H)], axis=1)
                     for h in range(H)]
            S = jnp.concatenate(rows_, axis=0)
        o = o + _dot(jnp.where(row_chunk == c, qt, jnp.zeros_like(qt)), _bf(S))
        U = _dot(jnp.where(col_chunk == c, kdT, jnp.zeros_like(kdT)), vb)
        dcol = jnp.exp(totT[:, c * L:c * L + 1])
        S = dcol * S + jnp.where(blockdiag, U, 0.0)
        if not carry:
            for h in range(H):
                so_ref[c, h] = S[h * DK:(h + 1) * DK, h * DV:(h + 1) * DV]
    if carry:
        s_scr[...] = S

        @pl.when(pl.program_id(1) == pl.num_programs(1) - 1)
        def _():
            for h in range(H):
                so_ref[h] = S[h * DK:(h + 1) * DK, h * DV:(h + 1) * DV]

    gn = gn_ref[...]
    outs = []
    for h in range(H):
        oh = o[:, h * DV:(h + 1) * DV]
        outs.append(_rms(oh, gn[:, h * DV:(h + 1) * DV]))
    y_ref[...] = _bf(jnp.concatenate(outs, axis=1) * (r * _sigmoid(r)))


def _gla_call(zA, row0, B, T, L, wa2p, ba, gn, s0):
    H, DK, DV = GLA_HEADS, GLA_DK, GLA_DV
    carry = s0 is None
    if carry:
        R = _pick_block(T, (256, 128))
        nt = T // R
        grid = (B, nt)
        rb0 = row0 // R
        in_specs = [pl.BlockSpec((R, W_A), lambda b, t: (rb0 + b * nt + t, 0)),
                    pl.BlockSpec((128, H * DK), lambda b, t: (0, 0)),
                    pl.BlockSpec((1, H * DK), lambda b, t: (0, 0)),
                    pl.BlockSpec((1, H * DV), lambda b, t: (0, 0))]
        out_specs = [pl.BlockSpec((R, H * DV), lambda b, t: (b * nt + t, 0)),
                     pl.BlockSpec((None, H, DK, DV), lambda b, t: (b, 0, 0, 0))]
        scratch = [pltpu.VMEM((H * DK, H * DV), f32)]
        args = (zA, wa2p, ba, gn)
        sem = ("arbitrary", "arbitrary")
    else:
        R = 128
        nb = (B * T) // R
        cpb = R // L
        grid = (nb,)
        rb0 = row0 // R
        in_specs = [pl.BlockSpec((R, W_A), lambda i: (rb0 + i, 0)),
                    pl.BlockSpec((128, H * DK), lambda i: (0, 0)),
                    pl.BlockSpec((1, H * DK), lambda i: (0, 0)),
                    pl.BlockSpec((1, H * DV), lambda i: (0, 0)),
                    pl.BlockSpec((cpb, H, DK, DV), lambda i: (i, 0, 0, 0))]
        out_specs = [pl.BlockSpec((R, H * DV), lambda i: (i, 0)),
                     pl.BlockSpec((cpb, H, DK, DV), lambda i: (i, 0, 0, 0))]
        scratch = []
        args = (zA, wa2p, ba, gn, s0)
        sem = ("arbitrary",)
    return pl.pallas_call(
        functools.partial(_gla_kernel, R=R, L=L, carry=carry), name="gla_prompt" if carry else "gla_sample",
        grid=grid, in_specs=in_specs, out_specs=out_specs,
        out_shape=[jax.ShapeDtypeStruct((B * T, H * DV), bf16),
                   jax.ShapeDtypeStruct((B, H, DK, DV), f32)],
        scratch_shapes=scratch,
        compiler_params=_cparams(*sem),
    )(*args)


def _mlstm_kernel(*refs, R, L, carry):
    if carry:
        z_ref, bi_ref, bf_ref, gn_ref, y_ref, co_ref, no_ref, mo_ref, c_scr, n_scr, m_scr = refs
    else:
        z_ref, bi_ref, bf_ref, gn_ref, c0_ref, n0_ref, m0_ref, y_ref, co_ref, no_ref, mo_ref = refs
    H, DK, DV = ML_HEADS, ML_DK, ML_DV
    HK, HV = H * DK, H * DV
    nchunk = R // L
    z = z_ref[...]
    q = z[:, 0:HK]
    k = z[:, HK:2 * HK] * (DK ** -0.5)
    v = z[:, 2 * HK:2 * HK + HV]
    og = z[:, 2 * HK + HV:2 * HK + 2 * HV]
    ig = z[:, 2 * HK + 2 * HV:2 * HK + 2 * HV + 128] + bi_ref[...]
    fg = z[:, 2 * HK + 2 * HV + 128:2 * HK + 2 * HV + 256] + bf_ref[...]
    rowmod = _iota((R, 128), 0) % L
    row_chunk128 = _iota((R, 128), 0) // L
    F = _seg_cumsum(_log_sigmoid(fg), rowmod, L)
    FT = F.T
    IT = ig.T

    if carry:
        @pl.when(pl.program_id(1) == 0)
        def _():
            c_scr[...] = jnp.zeros((HK, HV), f32)
            n_scr[...] = jnp.zeros((1, HK), f32)
            m_scr[...] = jnp.zeros((1, 128), f32)

    tpos = _iota((R, R), 0)
    spos = _iota((R, R), 1)
    causal = (tpos // L == spos // L) & (spos <= tpos)
    qb = _bf(q)
    kb = _bf(k)
    vb = _bf(v)
    lane_head_k = _iota((R, HK), 1) // DK

    Dms, qks, MX = [], [], jnp.zeros((R, 128), f32)
    lane128 = _iota((R, 128), 1)
    for h in range(H):
        Dm = jnp.where(causal, F[:, h:h + 1] - FT[h:h + 1, :] + IT[h:h + 1, :], NEG)
        Dms.append(Dm)
        MX = jnp.where(lane128 == h, jnp.max(Dm, axis=1, keepdims=True), MX)
        qks.append(_dot_nt(jnp.where(lane_head_k == h, qb, jnp.zeros_like(qb)), kb))

    if carry:
        Mprev = jnp.zeros((R, 128), f32)
        m_run = m_scr[...]
        for c in range(nchunk):
            Mprev = jnp.where(row_chunk128 == c, m_run, Mprev)
            last = c * L + L - 1
            m_run = jnp.maximum(F[last:last + 1, :] + m_run, MX[last:last + 1, :])
    else:
        Mprev = jnp.zeros((R, 128), f32)
        for c in range(nchunk):
            Mprev = jnp.where(row_chunk128 == c, m0_ref[c], Mprev)
    Mt = jnp.maximum(F + Mprev, MX)
    Mnew = _seg_last(Mt, rowmod, L)
    Fend = _seg_last(F, rowmod, L)
    Wprev = jnp.exp(F + Mprev - Mt)
    Wsrc = jnp.exp(Fend - F + ig - Mnew)
    Wold = jnp.exp(Fend + Mprev - Mnew)
    Emt = jnp.exp(-Mt)

    kw = k * _expand_heads(Wsrc, H, DK)
    kwT = _bf(kw.T)
    Wold_k = _expand_heads(Wold, H, DK)
    Wold_v = _expand_heads(Wold, H, DV)
    blockdiag = (_iota((HK, HV), 0) // DK) == (_iota((HK, HV), 1) // DV)
    row_chunk = _iota((R, HK), 0) // L
    col_chunk = _iota((HK, R), 1) // L

    qC = jnp.zeros((R, HV), f32)
    Nrows = jnp.zeros((R, HK), f32)
    if carry:
        C = c_scr[...]
        n = n_scr[...]
    for c in range(nchunk):
        if not carry:
            rows_ = [jnp.concatenate([c0_ref[c, h] if g == h else jnp.zeros((DK, DV), f32) for g in range(H)], axis=1)
                     for h in range(H)]
            C = jnp.concatenate(rows_, axis=0)
            n = n0_ref[c]
        in_chunk = row_chunk == c
        qC = qC + _dot(jnp.where(in_chunk, qb, jnp.zeros_like(qb)), _bf(C))
        Nrows = jnp.where(in_chunk, n, Nrows)
        last = c * L + L - 1
        U = _dot(jnp.where(col_chunk == c, kwT, jnp.zeros_like(kwT)), vb)
        C = Wold_v[last:last + 1, :] * C + jnp.where(blockdiag, U, 0.0)
        n = Wold_k[last:last + 1, :] * n + jnp.sum(jnp.where(in_chunk, kw, 0.0), axis=0, keepdims=True)
        if not carry:
            for h in range(H):
                co_ref[c, h] = C[h * DK:(h + 1) * DK, h * DV:(h + 1) * DV]
            no_ref[c] = n
            mo_ref[c] = Mnew[last:last + 1, :]
    if carry:
        c_scr[...] = C
        n_scr[...] = n
        m_scr[...] = Mnew[R - 1:R, :]

        @pl.when(pl.program_id(1) == pl.num_programs(1) - 1)
        def _():
            for h in range(H):
                co_ref[h] = C[h * DK:(h + 1) * DK, h * DV:(h + 1) * DV]
            no_ref[...] = n
            mo_ref[...] = Mnew[R - 1:R, :]

    qn = _dot2b(q * Nrows, _bf(_iota((HK, 128), 1) == _iota((HK, 128), 0) // DK))
    gn = gn_ref[...]
    outs = []
    for h in range(H):
        A = jnp.exp(Dms[h] - Mt[:, h:h + 1]) * qks[h]
        num = Wprev[:, h:h + 1] * qC[:, h * DV:(h + 1) * DV] + _dot(_bf(A), vb[:, h * DV:(h + 1) * DV])
        nq = Wprev[:, h:h + 1] * qn[:, h:h + 1] + jnp.sum(A, axis=1, keepdims=True)
        hh = num / jnp.maximum(jnp.abs(nq), Emt[:, h:h + 1])
        outs.append(_rms(hh, gn[:, h * DV:(h + 1) * DV]))
    y_ref[...] = _bf(_sigmoid(og) * jnp.concatenate(outs, axis=1))


def _mlstm_call(zC, row0, B, T, L, bi, bfg, gn, st):
    H, DK, DV = ML_HEADS, ML_DK, ML_DV
    carry = st is None
    R = 128
    out_shape = [jax.ShapeDtypeStruct((B * T, H * DV), bf16),
                 jax.ShapeDtypeStruct((B, H, DK, DV), f32),
                 jax.ShapeDtypeStruct((B, 1, H * DK), f32),
                 jax.ShapeDtypeStruct((B, 1, 128), f32)]
    rb0 = row0 // R
    if carry:
        nt = T // R
        grid = (B, nt)
        c2 = lambda b, t: (0, 0)
        in_specs = [pl.BlockSpec((R, W_C), lambda b, t: (rb0 + b * nt + t, 0)),
                    pl.BlockSpec((1, 128), c2), pl.BlockSpec((1, 128), c2), pl.BlockSpec((1, H * DV), c2)]
        out_specs = [pl.BlockSpec((R, H * DV), lambda b, t: (b * nt + t, 0)),
                     pl.BlockSpec((None, H, DK, DV), lambda b, t: (b, 0, 0, 0)),
                     pl.BlockSpec((None, 1, H * DK), lambda b, t: (b, 0, 0)),
                     pl.BlockSpec((None, 1, 128), lambda b, t: (b, 0, 0))]
        scratch = [pltpu.VMEM((H * DK, H * DV), f32), pltpu.VMEM((1, H * DK), f32), pltpu.VMEM((1, 128), f32)]
        args = (zC, bi, bfg, gn)
        sem = ("arbitrary", "arbitrary")
    else:
        c0, n0, m0 = st
        nb = (B * T) // R
        cpb = R // L
        grid = (nb,)
        c2 = lambda i: (0, 0)
        in_specs = [pl.BlockSpec((R, W_C), lambda i: (rb0 + i, 0)),
                    pl.BlockSpec((1, 128), c2), pl.BlockSpec((1, 128), c2), pl.BlockSpec((1, H * DV), c2),
                    pl.BlockSpec((cpb, H, DK, DV), lambda i: (i, 0, 0, 0)),
                    pl.BlockSpec((cpb, 1, H * DK), lambda i: (i, 0, 0)),
                    pl.BlockSpec((cpb, 1, 128), lambda i: (i, 0, 0))]
        out_specs = [pl.BlockSpec((R, H * DV), lambda i: (i, 0)),
                     pl.BlockSpec((cpb, H, DK, DV), lambda i: (i, 0, 0, 0)),
                     pl.BlockSpec((cpb, 1, H * DK), lambda i: (i, 0, 0)),
                     pl.BlockSpec((cpb, 1, 128), lambda i: (i, 0, 0))]
        scratch = []
        args = (zC, bi, bfg, gn, c0, n0, m0)
        sem = ("arbitrary",)
    return pl.pallas_call(
        functools.partial(_mlstm_kernel, R=R, L=L, carry=carry), name="mlstm_prompt" if carry else "mlstm_sample",
        grid=grid, in_specs=in_specs, out_specs=out_specs, out_shape=out_shape,
        scratch_shapes=scratch, compiler_params=_cparams(*sem),
    )(*args)


def _sgu_kernel(z_ref, gv_ref, ws_ref, bcol_ref, y_ref, *v_out, nch):
    G = SG_GROUPS
    W = SG_WIDTH
    cw = W // G
    tril = _iota((SG_CHUNK, SG_CHUNK), 1) <= _iota((SG_CHUNK, SG_CHUNK), 0)
    for c in range(nch):
        rs = slice(c * SG_CHUNK, (c + 1) * SG_CHUNK)
        u = z_ref[rs, 0:W]
        vn = _rms(_gelu(z_ref[rs, W:2 * W]), gv_ref[...])
        if v_out:
            v_out[0][rs, :] = vn
        vb = _bf(vn)
        bcol = bcol_ref[...]
        outs = []
        for g in range(G):
            wc = _bf(jnp.where(tril, ws_ref[g], 0.0))
            outs.append(_dot(wc, vb[:, g * cw:(g + 1) * cw]) + bcol[:, g:g + 1])
        y_ref[rs, :] = _bf(_gelu(u) * jnp.concatenate(outs, axis=1))


def _sgu_call(zD, row0, n_rows, gv, ws, bcol, want_v):
    nch = _pick_block(n_rows // SG_CHUNK, (4, 2, 1))
    Rb = nch * SG_CHUNK
    rb0 = row0 // Rb
    out_shape = [jax.ShapeDtypeStruct((n_rows, SG_WIDTH), bf16)]
    out_specs = [pl.BlockSpec((Rb, SG_WIDTH), lambda i: (i, 0))]
    if want_v:
        out_shape.append(jax.ShapeDtypeStruct((n_rows, SG_WIDTH), f32))
        out_specs.append(pl.BlockSpec((Rb, SG_WIDTH), lambda i: (i, 0)))
    return pl.pallas_call(
        functools.partial(_sgu_kernel, nch=nch), name="sgu",
        grid=(n_rows // Rb,),
        in_specs=[pl.BlockSpec((Rb, W_D), lambda i: (rb0 + i, 0)),
                  pl.BlockSpec((1, SG_WIDTH), lambda i: (0, 0)),
                  pl.BlockSpec((SG_GROUPS, SG_CHUNK, SG_CHUNK), lambda i: (0, 0, 0)),
                  pl.BlockSpec((SG_CHUNK, 128), lambda i: (0, 0))],
        out_specs=out_specs, out_shape=out_shape,
        compiler_params=_cparams("arbitrary"),
    )(zD, gv, ws, bcol)


def _mla_prep_kernel(z_ref, cos_ref, sin_ref, gq_ref, gkv_ref, wqn_ref, wqr_ref, wa_ref, pb_ref, rot_ref,
                     q_ref, k_ref):
    z = z_ref[...]
    dq = z[:, 0:MLA_Q_RANK]
    dkv = z[:, MLA_Q_RANK:MLA_Q_RANK + MLA_KV_RANK]
    kr = z[:, MLA_Q_RANK + MLA_KV_RANK:MLA_Q_RANK + MLA_KV_RANK + 128]
    cos = cos_ref[...]
    sin = sin_ref[...]
    rot = rot_ref[...]
    dqn = _bf(_rms(dq, gq_ref[...]))
    qn = _dot(dqn, wqn_ref[...])
    qr = _dot(dqn, wqr_ref[...])
    qrope = qr * cos + _dot2b(qr, rot) * sin
    q_ref[...] = _bf(_dot(_bf(qn), wa_ref[...]) + _dot(_bf(qrope), pb_ref[...]))
    ckv = _rms(dkv, gkv_ref[...])
    krope = kr * cos[:, 0:128] + _dot2b(kr, rot[0:128, 0:128]) * sin[:, 0:128]
    k_ref[...] = jnp.concatenate([ckv, krope], axis=1)


def _mla_prep_call(rows, zB, cos, sin, gq, gkv, wqn, wqr, wa, pb, rot):
    R = rows.R
    HC = MLA_HEADS * 256
    c2 = lambda i: (0, 0)
    return pl.pallas_call(
        _mla_prep_kernel, name="mla_prep",
        grid=(rows.nb,),
        in_specs=[pl.BlockSpec((R, W_B), lambda i: (i, 0)),
                  pl.BlockSpec((R, 256), lambda i: (i, 0)),
                  pl.BlockSpec((R, 256), lambda i: (i, 0)),
                  pl.BlockSpec((1, MLA_Q_RANK), c2), pl.BlockSpec((1, MLA_KV_RANK), c2),
                  pl.BlockSpec((MLA_Q_RANK, MLA_HEADS * MLA_NOPE), c2),
                  pl.BlockSpec((MLA_Q_RANK, MLA_HEADS * MLA_ROPE), c2),
                  pl.BlockSpec((MLA_HEADS * MLA_NOPE, HC), c2),
                  pl.BlockSpec((MLA_HEADS * MLA_ROPE, HC), c2),
                  pl.BlockSpec((256, 256), c2)],
        out_specs=[pl.BlockSpec((R, HC), lambda i: (i, 0)), pl.BlockSpec((R, 256), lambda i: (i, 0))],
        out_shape=[jax.ShapeDtypeStruct((rows.N, HC), bf16), jax.ShapeDtypeStruct((rows.N, 256), f32)],
        compiler_params=_cparams("arbitrary"),
    )(zB, cos, sin, gq, gkv, wqn, wqr, wa, pb, rot)


def _flash_kernel(q_ref, k_ref, wuv_ref, y_ref, qs_scr, m_scr, l_scr, acc_scr, *, tq, tk):
    qi = pl.program_id(1)
    ki = pl.program_id(2)
    H = MLA_HEADS
    NR = H * tq
    diag = (qi * tq + tq - 1) // tk

    @pl.when(ki == 0)
    def _():
        for h in range(H):
            qs_scr[h * tq:(h + 1) * tq, :] = q_ref[:, h * 256:(h + 1) * 256]
        m_scr[...] = jnp.full(m_scr.shape, NEG, f32)
        l_scr[...] = jnp.zeros(l_scr.shape, f32)
        acc_scr[...] = jnp.zeros(acc_scr.shape, f32)

    def step(masked):
        kb = _bf(k_ref[...])
        s = _dot_nt(qs_scr[...], kb) * MLA_SCALE
        if masked:
            qpos = qi * tq + _iota((NR, tk), 0) % tq
            s = jnp.where(ki * tk + _iota((NR, tk), 1) <= qpos, s, NEG)
        m_prev = m_scr[...]
        m_next = jnp.maximum(m_prev, jnp.max(s, axis=1, keepdims=True))
        alpha = jnp.exp(m_prev - m_next)
        p = jnp.exp(s - jnp.concatenate([m_next] * (tk // 128), axis=1))
        l_scr[...] = alpha * l_scr[...] + jnp.sum(p, axis=1, keepdims=True)
        acc_scr[...] = alpha * acc_scr[...] + _dot(_bf(p), kb[:, 0:MLA_KV_RANK])
        m_scr[...] = m_next

    pl.when(ki < diag)(functools.partial(step, False))
    pl.when(ki == diag)(functools.partial(step, True))

    @pl.when(ki == pl.num_programs(2) - 1)
    def _():
        o = _bf(acc_scr[...] / l_scr[...])
        y = jnp.zeros((tq, MLA_HEADS * MLA_V), f32)
        for h in range(H):
            y = y + _dot(o[h * tq:(h + 1) * tq, :], wuv_ref[h])
        y_ref[...] = _bf(y)


def _flash_call(qcat, kcat, wuv, B, T):
    tq = 128
    tk = _pick_block(T, (512, 256, 128))
    nq, nk = T // tq, T // tk
    H = MLA_HEADS
    return pl.pallas_call(
        functools.partial(_flash_kernel, tq=tq, tk=tk), name="mla_flash",
        grid=(B, nq, nk),
        in_specs=[pl.BlockSpec((tq, H * 256), lambda b, i, j: (b * nq + i, 0)),
                  pl.BlockSpec((tk, 256), lambda b, i, j: (b * nk + jnp.minimum(j, (i * tq + tq - 1) // tk), 0)),
                  pl.BlockSpec((H, MLA_KV_RANK, H * MLA_V), lambda b, i, j: (0, 0, 0))],
        out_specs=pl.BlockSpec((tq, H * MLA_V), lambda b, i, j: (b * nq + i, 0)),
        out_shape=jax.ShapeDtypeStruct((B * T, H * MLA_V), bf16),
        scratch_shapes=[pltpu.VMEM((H * tq, 256), bf16), pltpu.VMEM((H * tq, 128), f32),
                        pltpu.VMEM((H * tq, 128), f32), pltpu.VMEM((H * tq, MLA_KV_RANK), f32)],
        compiler_params=_cparams("arbitrary", "arbitrary", "arbitrary"),
    )(qcat, kcat, wuv)


def _paged_kernel(pt_ref, q_ref, kn_ref, wuv_ref, ck_hbm, krt_hbm, y_ref,
                  ck_buf, krt_buf, kb_scr, sem, *, l, n_pages, page, TS, chunk):
    b = pl.program_id(0)
    nb = pl.num_programs(0)
    H = MLA_HEADS
    NR = TS * H
    n_keys = n_pages * page
    slot = b % 2

    def page_copies(bb, sl, p):
        pg = pt_ref[bb * n_pages + p]
        keys_ = pl.ds(p * page, page)
        return (pltpu.make_async_copy(ck_hbm.at[l, pg], ck_buf.at[sl, keys_, :], sem.at[0, sl]),
                pltpu.make_async_copy(krt_hbm.at[l, pg], krt_buf.at[sl, :, keys_], sem.at[1, sl]))

    def start_all(bb, sl):
        for p in range(n_pages):
            for cp in page_copies(bb, sl, p):
                cp.start()

    def wait_all(bb, sl):
        for p in range(n_pages):
            for cp in page_copies(bb, sl, p):
                cp.wait()

    @pl.when(b == 0)
    def _():
        start_all(0, 0)

    @pl.when(b + 1 < nb)
    def _():
        start_all(b + 1, 1 - slot)

    wait_all(b, slot)

    q = q_ref[...]
    ql = q[:, 0:MLA_KV_RANK]
    qr = q[:, MLA_KV_RANK:MLA_KV_RANK + MLA_ROPE]
    kn = _bf(kn_ref[...])
    s_new = _dot_nt(q, kn) * MLA_SCALE
    ok = (_iota((NR, 8), 1) <= _iota((NR, 8), 0) // H) & (_iota((NR, 8), 1) < TS)
    s_new = jnp.where(ok, s_new, NEG)
    m = jnp.max(s_new, axis=1, keepdims=True)
    ss = []
    for c in range(n_keys // chunk):
        rows_ = pl.ds(c * chunk, chunk)
        kb_scr[rows_, :] = _bf(ck_buf[slot, rows_, :])
        s = (_dot_nt(ql, kb_scr[rows_, :]) + _dot(qr, _bf(krt_buf[slot, :, rows_]))) * MLA_SCALE
        m = jnp.maximum(m, jnp.max(s, axis=1, keepdims=True))
        ss.append(s)
    e = jnp.exp(s_new - m)
    lsum = jnp.sum(e, axis=1, keepdims=True)
    acc = _dot(_bf(e), kn[:, 0:MLA_KV_RANK])
    for c in range(n_keys // chunk):
        e = jnp.exp(ss[c] - m)
        lsum = lsum + jnp.sum(e, axis=1, keepdims=True)
        acc = acc + _dot(_bf(e), kb_scr[pl.ds(c * chunk, chunk), :])
    o = acc / lsum
    row_head = _iota((NR, MLA_KV_RANK), 0) % H
    y = jnp.zeros((NR, H * MLA_V), f32)
    for h in range(H):
        y = y + _dot(_bf(jnp.where(row_head == h, o, 0.0)), wuv_ref[h])
    y_ref[...] = _bf(jnp.sum(y.reshape(TS, H, H * MLA_V), axis=1))


def _paged_call(page_table, q3, kn3, wuv, cache_ckv, cache_krt, l):
    DB, n_pages = page_table.shape
    TS = q3.shape[1] // MLA_HEADS
    H = MLA_HEADS
    page = cache_ckv.shape[2]
    n_keys = n_pages * page
    chunk = _pick_block(n_keys, (2048, 1024, 512, 256, 128))
    grid_spec = pltpu.PrefetchScalarGridSpec(
        num_scalar_prefetch=1,
        grid=(DB,),
        in_specs=[pl.BlockSpec((None, TS * H, 256), lambda b, pt: (b, 0, 0)),
                  pl.BlockSpec((None, 8, 256), lambda b, pt: (b, 0, 0)),
                  pl.BlockSpec((H, MLA_KV_RANK, H * MLA_V), lambda b, pt: (0, 0, 0)),
                  pl.BlockSpec(memory_space=pl.ANY), pl.BlockSpec(memory_space=pl.ANY)],
        out_specs=pl.BlockSpec((None, TS, H * MLA_V), lambda b, pt: (b, 0, 0)),
        scratch_shapes=[pltpu.VMEM((2, n_keys, MLA_KV_RANK), f32),
                        pltpu.VMEM((2, MLA_ROPE, n_keys), f32),
                        pltpu.VMEM((n_keys, MLA_KV_RANK), bf16),
                        pltpu.SemaphoreType.DMA((2, 2))],
    )
    return pl.pallas_call(
        functools.partial(_paged_kernel, l=l, n_pages=n_pages, page=page, TS=TS, chunk=chunk), name="mla_paged",
        grid_spec=grid_spec,
        out_shape=jax.ShapeDtypeStruct((DB, TS, H * MLA_V), bf16),
        compiler_params=_cparams("arbitrary"),
    )(page_table.reshape(-1), q3, kn3, wuv, cache_ckv, cache_krt)


def _merge_kernel(h_ref, ya_ref, yb_ref, yc_ref, yd_ref, g0, g1, g2, g3, b0, b1, b2, b3, o_ref):
    h = h_ref[...]
    acc = None
    for y_ref, g_ref, b_ref in ((ya_ref, g0, b0), (yb_ref, g1, b1), (yc_ref, g2, b2), (yd_ref, g3, b3)):
        gate = _sigmoid(_dot_nt(h, _bf(g_ref[...])))
        term = gate * _dot(y_ref[...], _bf(b_ref[...]))
        acc = term if acc is None else acc + term
    o_ref[...] = _bf(acc)


def _merge_call(h1, ys, w_in_t, w_br, l):
    N = h1.shape[0]
    D = D_MODEL
    tm = _pick_block(N, (1088, 640, 512, 256, 128))
    tn = 256
    ncol = D // tn
    gate_specs = [pl.BlockSpec((None, tn, D), (lambda i, j, br=br: (l, br * ncol + j, 0))) for br in range(N_BRANCH)]
    br_specs = [pl.BlockSpec((None, None, BRANCH_WIDTH, tn), (lambda i, j, br=br: (l, br, 0, j))) for br in range(N_BRANCH)]
    y_specs = [pl.BlockSpec((tm, BRANCH_WIDTH), lambda i, j: (i, 0)) for _ in range(N_BRANCH)]
    return pl.pallas_call(
        _merge_kernel, name="merge",
        grid=(N // tm, ncol),
        in_specs=[pl.BlockSpec((tm, D), lambda i, j: (i, 0))] + y_specs + gate_specs + br_specs,
        out_specs=pl.BlockSpec((tm, tn), lambda i, j: (i, j)),
        out_shape=jax.ShapeDtypeStruct((N, D), bf16),
        compiler_params=_cparams("arbitrary", "arbitrary"),
    )(h1, *ys, w_in_t, w_in_t, w_in_t, w_in_t, w_br, w_br, w_br, w_br)


def _outproj_kernel(x_ref, mg_ref, wo_ref, g_ref, g1s, g1p, shs, shp, scs, scp, xo_ref, h_ref, hb_ref, *, rows):
    i = pl.program_id(0)
    x = x_ref[...] + rows.pick(i, g1s, g1p) * _dot(mg_ref[...], wo_ref[...])
    xo_ref[...] = x
    h = _rms(x, g_ref[...]) * (1.0 + rows.pick(i, scs, scp)) + rows.pick(i, shs, shp)
    h_ref[...] = h
    hb_ref[...] = _bf(h)


def _outproj_call(rows, x, merged, wo_bf, gain2, mod, l):
    R = rows.R
    D = D_MODEL
    g1s, g1p = rows.mod_specs(l, 2)
    shs, shp = rows.mod_specs(l, 3)
    scs, scp = rows.mod_specs(l, 4)
    return pl.pallas_call(
        functools.partial(_outproj_kernel, rows=rows), name="outproj",
        grid=(rows.nb,),
        in_specs=[pl.BlockSpec((R, D), lambda i: (i, 0)), pl.BlockSpec((R, D), lambda i: (i, 0)),
                  pl.BlockSpec((D, D), lambda i: (0, 0)), pl.BlockSpec((1, D), lambda i: (0, 0)),
                  g1s, g1p, shs, shp, scs, scp],
        out_specs=[pl.BlockSpec((R, D), lambda i: (i, 0))] * 3,
        out_shape=[jax.ShapeDtypeStruct((rows.N, D), f32), jax.ShapeDtypeStruct((rows.N, D), f32),
                   jax.ShapeDtypeStruct((rows.N, D), bf16)],
        compiler_params=_cparams("arbitrary"),
    )(x, merged, wo_bf, gain2.reshape(1, D), mod, mod, mod, mod, mod, mod)


def _peer_candidates():
    K = PEER_TOPK
    pairs = [(k1, k2) for k1 in range(K) for k2 in range(K) if (k1 + 1) * (k2 + 1) <= K]
    n = -(-len(pairs) // 8) * 8
    g1 = np.zeros((n, K), np.float32)
    g2 = np.zeros((n, K), np.float32)
    for r, (k1, k2) in enumerate(pairs):
        g1[r, k1] = 1.0
        g2[r, k2] = 1.0
    return len(pairs), jnp.asarray(g1, bf16), jnp.asarray(g2, bf16)


def _select_rows3(sel_bf, v):
    hi = _bf(v)
    r1 = v - hi.astype(f32)
    mid = _bf(r1)
    lo = _bf(r1 - mid.astype(f32))
    return (_dot(sel_bf, hi) + _dot(sel_bf, mid)) + _dot(sel_bf, lo)


def _peer_score_kernel(h_ref, wqh_ref, wql_ref, k1_ref, k2_ref, g1_ref, g2_ref, a_ref, b_ref, g_ref,
                       v1_scr, i1_scr, v2_scr, i2_scr, top_scr, ea_scr, eb_scr, *, n_cand):
    T = h_ref.shape[0]
    NK, K = PEER_NKEYS, PEER_TOPK
    LT = 128
    NC = g1_ref.shape[0]
    hh, hl = _split2(h_ref[...])
    q = _dot(hh, wqh_ref[...]) + (_dot(hh, wql_ref[...]) + _dot(hl, wqh_ref[...]))
    key_idx = _iota((NK, LT), 0).astype(f32)
    cand_idx = _iota((NC, LT), 0).astype(f32)
    neg_inf = -jnp.inf
    g1 = g1_ref[...]
    g2 = g2_ref[...]

    def topk_into(s, val_scr, idx_scr):
        for kk in range(K):
            m = jnp.max(s, axis=0, keepdims=True)
            pos = jnp.min(jnp.where(s == m, key_idx, float(NK)), axis=0, keepdims=True)
            val_scr[pl.ds(kk, 1), :] = m
            idx_scr[pl.ds(kk, 1), :] = pos
            s = jnp.where(key_idx == pos, neg_inf, s)

    for h in range(PEER_HEADS):
        for lt in range(T // LT):
            cols = slice(lt * LT, (lt + 1) * LT)
            qh, ql = _split2(q[lt * LT:(lt + 1) * LT, h * PEER_DKEY:(h + 1) * PEER_DKEY])
            for k_ref, val_scr, idx_scr in ((k1_ref, v1_scr, i1_scr), (k2_ref, v2_scr, i2_scr)):
                kh, kl = _split2(k_ref[h])
                sT = _dot_nt(kh, qh) + (_dot_nt(kh, ql) + _dot_nt(kl, qh))
                topk_into(sT, val_scr, idx_scr)
            cand = _select_rows3(g1, v1_scr[...]) + _select_rows3(g2, v2_scr[...])
            cand = jnp.where(cand_idx < float(n_cand), cand, neg_inf)
            ca = _dot(g1, _bf(i1_scr[...]))
            cb = _dot(g2, _bf(i2_scr[...]))
            for kk in range(K):
                m = jnp.max(cand, axis=0, keepdims=True)
                pos = jnp.min(jnp.where(cand == m, cand_idx, float(NC)), axis=0, keepdims=True)
                sel = cand_idx == pos
                r = h * K + kk
                top_scr[pl.ds(r, 1), cols] = m
                ea_scr[pl.ds(r, 1), cols] = jnp.sum(jnp.where(sel, ca, 0.0), axis=0, keepdims=True)
                eb_scr[pl.ds(r, 1), cols] = jnp.sum(jnp.where(sel, cb, 0.0), axis=0, keepdims=True)
                cand = jnp.where(sel, neg_inf, cand)
        tops = top_scr[pl.ds(h * K, K), :]
        e = jnp.exp(tops - jnp.max(tops, axis=0, keepdims=True))
        top_scr[pl.ds(h * K, K), :] = e / jnp.sum(e, axis=0, keepdims=True)
    g_ref[...] = top_scr[...].T
    a_ref[...] = ea_scr[...].T
    b_ref[...] = eb_scr[...].T


def _peer_score_call(rows, h2, wqh, wql, k1p, k2p):
    R = rows.R
    D = D_MODEL
    HK = PEER_HEADS * PEER_TOPK
    o_spec = pl.BlockSpec((R, HK), lambda i: (i, 0))
    o_shape = jax.ShapeDtypeStruct((rows.N, HK), f32)
    n_cand, g1, g2 = _peer_candidates()
    return pl.pallas_call(
        functools.partial(_peer_score_kernel, n_cand=n_cand), name="peer_score",
        grid=(rows.nb,),
        in_specs=[pl.BlockSpec((R, D), lambda i: (i, 0)),
                  pl.BlockSpec((D, PEER_HEADS * PEER_DKEY), lambda i: (0, 0)),
                  pl.BlockSpec((D, PEER_HEADS * PEER_DKEY), lambda i: (0, 0)),
                  pl.BlockSpec((PEER_HEADS, PEER_NKEYS, PEER_DKEY), lambda i: (0, 0, 0)),
                  pl.BlockSpec((PEER_HEADS, PEER_NKEYS, PEER_DKEY), lambda i: (0, 0, 0)),
                  pl.BlockSpec(g1.shape, lambda i: (0, 0)), pl.BlockSpec(g2.shape, lambda i: (0, 0))],
        out_specs=[o_spec, o_spec, o_spec],
        out_shape=[o_shape, o_shape, o_shape],
        scratch_shapes=[pltpu.VMEM((PEER_TOPK, 128), f32)] * 4 + [pltpu.VMEM((HK, R), f32)] * 3,
        compiler_params=_cparams("arbitrary"),
    )(h2, wqh, wql, k1p, k2p, g1, g2)


def _peer_mix_kernel(h_ref, ea_ref, eb_ref, g_ref, u_ref, v_ref, o_ref, a_scr, *, T, nch, ac, n_half):
    j = pl.program_id(1)
    NK = PEER_NKEYS
    NA = NK // n_half
    pitch = T + PEER_PITCH_PAD
    half = j // (2 * nch)
    jj = j % (2 * nch)

    @pl.when(j == 0)
    def _():
        o_ref[...] = jnp.zeros(o_ref.shape, f32)

    @pl.when(jj < nch)
    def _():
        res = _gelu(_dot_nt(h_ref[...], u_ref[...]))
        for al in range(ac):
            start = pl.multiple_of((jj * ac + al) * pitch, 8)
            a_scr[pl.ds(start, T), :] = res[:, al * NK:(al + 1) * NK]

    @pl.when(jj == nch)
    def _():
        sub_a = (_iota((NA, NK), 0) + half * NA).astype(f32).astype(bf16)
        sub_b = _iota((NK, NK), 0).astype(f32).astype(bf16)
        one = jnp.ones((NA, NK), bf16)
        zero_a = jnp.zeros((NA, NK), bf16)
        zero_b = jnp.zeros((NK, NK), bf16)

        def tok(t):
            ia = jnp.broadcast_to(_bf(ea_ref[pl.ds(t, 1), :]), (NA, NK))
            ib = jnp.broadcast_to(_bf(eb_ref[pl.ds(t, 1), :]), (NK, NK))
            gg = jnp.broadcast_to(_bf(g_ref[pl.ds(t, 1), :]), (NK, NK))
            onehot_a = jnp.where(sub_a == ia, one, zero_a)
            gate_b = jnp.where(sub_b == ib, gg, zero_b)
            w = _dot_nt(onehot_a, gate_b)
            a_scr[pl.ds(t, NA, stride=pitch), :] = w * a_scr[pl.ds(t, NA, stride=pitch), :]

        def tok_group(tg, carry):
            for u in range(PEER_TOKEN_UNROLL):
                tok(tg * PEER_TOKEN_UNROLL + u)
            return carry

        lax.fori_loop(0, T // PEER_TOKEN_UNROLL, tok_group, 0)

    @pl.when(jj >= nch)
    def _():
        parts = []
        for al in range(ac):
            start = pl.multiple_of(((jj - nch) * ac + al) * pitch, 8)
            parts.append(_bf(a_scr[pl.ds(start, T), :]))
        o_ref[...] += _dot(jnp.concatenate(parts, axis=1), v_ref[...])


def _peer_mix_call(rows, h2b, ea, eb, g, ub_all, vb_all, l):
    T = rows.R
    D = D_MODEL
    NK = PEER_NKEYS
    n_half = 2
    ac = 8
    nch = NK // n_half // ac
    HK = PEER_HEADS * PEER_TOPK
    row = lambda i, j: (i, 0)

    def u_idx(i, j):
        return (l, (j // (2 * nch)) * nch + jnp.minimum(j % (2 * nch), nch - 1), 0)

    def v_idx(i, j):
        return (l, (j // (2 * nch)) * nch + jnp.maximum(j % (2 * nch) - nch, 0), 0)

    return pl.pallas_call(
        functools.partial(_peer_mix_kernel, T=T, nch=nch, ac=ac, n_half=n_half), name="peer_mix",
        grid=(rows.nb, n_half * 2 * nch),
        in_specs=[pl.BlockSpec((T, D), row),
                  pl.BlockSpec((T, HK), row), pl.BlockSpec((T, HK), row), pl.BlockSpec((T, HK), row),
                  pl.BlockSpec((None, ac * NK, D), u_idx),
                  pl.BlockSpec((None, ac * NK, D), v_idx)],
        out_specs=pl.BlockSpec((T, D), row),
        out_shape=jax.ShapeDtypeStruct((rows.N, D), f32),
        scratch_shapes=[pltpu.VMEM((NK // n_half * (T + PEER_PITCH_PAD), NK), f32)],
        compiler_params=_cparams("arbitrary", "arbitrary"),
    )(h2b, ea, eb, g, ub_all, vb_all)


def _final_kernel(x_ref, p_ref, g2s, g2p, g_ref, o_ref, *, rows):
    x = x_ref[...] + rows.pick(pl.program_id(0), g2s, g2p) * p_ref[...]
    o_ref[...] = _rms(x, g_ref[...])


def _final_call(rows, x, peer_out, gain, mod, l_last):
    R, D = rows.R, D_MODEL
    g2s, g2p = rows.mod_specs(l_last, 5)
    row = pl.BlockSpec((R, D), lambda i: (i, 0))
    return pl.pallas_call(
        functools.partial(_final_kernel, rows=rows), name="final_norm", grid=(rows.nb,),
        in_specs=[row, row, g2s, g2p, pl.BlockSpec((1, D), lambda i: (0, 0))],
        out_specs=row,
        out_shape=jax.ShapeDtypeStruct((rows.N, D), f32),
        compiler_params=_cparams("arbitrary"),
    )(x, peer_out, mod, mod, gain.reshape(1, D))


def _mla_constants():
    H, R = MLA_HEADS, MLA_ROPE
    half = R // 2
    rot = np.zeros((H * R, H * R), np.float32)
    for g in range(H):
        for jj in range(half):
            rot[g * R + half + jj, g * R + jj] = -1.0
            rot[g * R + jj, g * R + half + jj] = 1.0
    place = np.zeros((H * R, H * 256), np.float32)
    for g in range(H):
        for jj in range(R):
            place[g * R + jj, g * 256 + MLA_KV_RANK + jj] = 1.0
    return jnp.asarray(rot, bf16), jnp.asarray(place, bf16)


def kernel(x_prompt, x_sample, cache_mla_ckv, cache_mla_kr, state_gla, state_mlstm_C, state_mlstm_n, state_mlstm_m, page_table, c_prompt, c_sample, w_ada, b_ada, norm1, norm2, w_in, gla_wa2, gla_ba, gla_norm, mla_w_uq, mla_gq, mla_gkv, mla_w_uk, mla_w_uv, ml_bi, ml_bf, ml_norm, sg_gv, sg_ws, sg_bs, w_br, w_o, peer_wq, peer_k1, peer_k2, peer_u, peer_v, final_norm):
    D = D_MODEL
    B, T, _ = x_prompt.shape
    DB, TS, _ = x_sample.shape
    depth = w_ada.shape[0]
    n_pages = page_table.shape[1]
    page = cache_mla_ckv.shape[2]
    past_len = n_pages * page
    rows = _Rows(B, T, DB, TS)
    Np, Ns = rows.Np, rows.Ns
    assert Ns % 128 == 0 and T % 128 == 0 and TS == 4

    c_rows = jnp.concatenate([jnp.repeat(c_sample, TS, axis=0), c_prompt,
                              jnp.zeros((16 - B, D), f32)], axis=0)
    mod = _ada_call(c_rows, w_ada, b_ada)

    x = jnp.concatenate([x_prompt.reshape(Np, D), x_sample.reshape(Ns, D)], axis=0)

    inv = ROPE_THETA ** (-jnp.arange(MLA_ROPE // 2, dtype=f32) / (MLA_ROPE // 2))
    pos_p = jnp.arange(T, dtype=jnp.int32)
    pos_s = past_len + jnp.arange(TS, dtype=jnp.int32)

    def table(fn):
        tp = fn(pos_p.astype(f32)[:, None] * inv)
        ts = fn(pos_s.astype(f32)[:, None] * inv)
        rows_ = jnp.concatenate([jnp.tile(tp, (B, 1)), jnp.tile(ts, (DB, 1))], axis=0)
        return jnp.tile(rows_, (1, 2 * MLA_HEADS))

    cos_t, sin_t = table(jnp.cos), table(jnp.sin)
    rot, place = _mla_constants()
    eye_h = jnp.eye(MLA_HEADS, dtype=f32)
    Ls = math.gcd(TS, GLA_CHUNK)
    Lp_gla = math.gcd(T, GLA_CHUNK)
    Lp_ml = math.gcd(T, ML_CHUNK)
    Ls_ml = math.gcd(TS, ML_CHUNK)
    spb = SG_CHUNK // TS
    w_in_t = jnp.swapaxes(w_in, 1, 2)
    cache_krt = jnp.swapaxes(cache_mla_kr, 2, 3)
    ub_all = _bf(peer_u)
    vb_all = _bf(peer_v)
    rows_peer = _Rows(B, T, DB, TS, block_sizes=(512, 256, 128))

    new_p, new_s = [], []
    for l in range(depth):
        wl = w_in_t[l]
        g0 = N_BRANCH * D
        zr = lambda n: jnp.zeros((n, D), f32)
        wA = _bf(jnp.concatenate([wl[g0:g0 + 1552], zr(112)], axis=0))
        wB = _bf(jnp.concatenate([wl[g0 + 1552:g0 + 2096], zr(96)], axis=0))
        wC = _bf(jnp.concatenate([wl[g0 + 2096:g0 + 3632], wl[g0 + 3632:g0 + 3636], zr(124),
                                  wl[g0 + 3636:g0 + 3640], zr(124)], axis=0))
        wD = _bf(wl[g0 + 3640:g0 + 4664])
        wa2p = jnp.concatenate([gla_wa2[l], jnp.zeros((128 - GLA_RANK, GLA_HEADS * GLA_DK), f32)], axis=0)
        uq = mla_w_uq[l].reshape(MLA_Q_RANK, MLA_HEADS, MLA_NOPE + MLA_ROPE)
        wqn = _bf(uq[:, :, :MLA_NOPE].reshape(MLA_Q_RANK, MLA_HEADS * MLA_NOPE))
        wqr = _bf(uq[:, :, MLA_NOPE:].reshape(MLA_Q_RANK, MLA_HEADS * MLA_ROPE))
        uk = jnp.transpose(mla_w_uk[l], (1, 2, 0))
        wa = jnp.einsum('hnr,hg->hngr', uk, eye_h)
        wa = _bf(jnp.pad(wa, ((0, 0), (0, 0), (0, 0), (0, 256 - MLA_KV_RANK))).reshape(MLA_HEADS * MLA_NOPE, MLA_HEADS * 256))
        wuv = _bf(jnp.einsum('rhv,hg->hrgv', mla_w_uv[l], eye_h).reshape(MLA_HEADS, MLA_KV_RANK, MLA_HEADS * MLA_V))
        pad128 = lambda a: jnp.pad(a, (0, 128 - a.shape[0])).reshape(1, 128)
        bcol_p = jnp.pad(sg_bs[l].T, ((0, 0), (0, 128 - SG_GROUPS)))
        ws_s = jnp.stack([jnp.kron(jnp.eye(spb, dtype=f32), sg_ws[l, g, :TS, :TS]) for g in range(SG_GROUPS)])
        bcol_s = jnp.pad(jnp.tile(sg_bs[l][:, :TS].T, (spb, 1)), ((0, 0), (0, 128 - SG_GROUPS)))
        wo_bf = _bf(w_o[l])
        wq = peer_wq[l]
        wqh = _bf(wq)
        wql = _bf(wq - wqh.astype(f32))
        half = PEER_DKEY // 2
        k1p = jnp.pad(peer_k1[l], ((0, 0), (0, 0), (0, half)))
        k2p = jnp.pad(peer_k2[l], ((0, 0), (0, 0), (half, 0)))

        if l == 0:
            h1 = _modnorm_call(rows, x, None, norm1[l], mod, l)
        else:
            x, h1 = _modnorm_call(rows, x, peer_out, norm1[l], mod, l)
        zA, zB, zC, zD = (_mm_call(h1, w) for w in (wA, wB, wC, wD))

        gn_a = gla_norm[l].reshape(1, -1)
        ba = gla_ba[l].reshape(1, -1)
        ya_p, S_p = _gla_call(zA, 0, B, T, Lp_gla, wa2p, ba, gn_a, None)
        ya_s, S_s = _gla_call(zA, Np, DB, TS, Ls, wa2p, ba, gn_a, state_gla[l])

        qcat, kcat = _mla_prep_call(rows, zB, cos_t, sin_t, mla_gq[l].reshape(1, -1), mla_gkv[l].reshape(1, -1),
                                    wqn, wqr, wa, place, rot)
        yb_p = _flash_call(qcat, kcat, wuv, B, T)
        q3 = qcat[Np:].reshape(DB, TS * MLA_HEADS, 256)
        kn3 = jnp.pad(kcat[Np:].reshape(DB, TS, 256), ((0, 0), (0, 8 - TS), (0, 0)))
        yb_s = _paged_call(page_table, q3, kn3, wuv, cache_mla_ckv, cache_krt, l).reshape(Ns, -1)

        bi, bfg, gn_c = pad128(ml_bi[l]), pad128(ml_bf[l]), ml_norm[l].reshape(1, -1)
        yc_p, C_p, n_p, m_p = _mlstm_call(zC, 0, B, T, Lp_ml, bi, bfg, gn_c, None)
        st = (state_mlstm_C[l], state_mlstm_n[l].reshape(DB, 1, -1),
              jnp.pad(state_mlstm_m[l], ((0, 0), (0, 128 - ML_HEADS))).reshape(DB, 1, 128))
        yc_s, C_s, n_s, m_s = _mlstm_call(zC, Np, DB, TS, Ls_ml, bi, bfg, gn_c, st)

        gv = sg_gv[l].reshape(1, -1)
        (yd_p,) = _sgu_call(zD, 0, Np, gv, sg_ws[l], bcol_p, False)
        yd_s, v_s = _sgu_call(zD, Np, Ns, gv, ws_s, bcol_s, True)

        ys = [jnp.concatenate([p_, s_], axis=0) for p_, s_ in ((ya_p, ya_s), (yb_p, yb_s), (yc_p, yc_s), (yd_p, yd_s))]
        merged = _merge_call(h1, ys, w_in_t, w_br, l)
        x, h2, h2b = _outproj_call(rows, x, merged, wo_bf, norm2[l], mod, l)

        ea, eb, g = _peer_score_call(rows, h2, wqh, wql, k1p, k2p)
        peer_out = _peer_mix_call(rows_peer, h2b, ea, eb, g, ub_all, vb_all, l)

        new_p.append(dict(ckv=kcat[:Np, :MLA_KV_RANK].reshape(B, T, -1),
                          kr=kcat[:Np, MLA_KV_RANK:MLA_KV_RANK + MLA_ROPE].reshape(B, T, -1),
                          S=S_p, C=C_p, n=n_p.reshape(B, ML_HEADS, ML_DK), m=m_p[:, 0, :ML_HEADS]))
        new_s.append(dict(ckv=kcat[Np:, :MLA_KV_RANK].reshape(DB, TS, -1),
                          kr=kcat[Np:, MLA_KV_RANK:MLA_KV_RANK + MLA_ROPE].reshape(DB, TS, -1),
                          S=S_s, C=C_s, n=n_s.reshape(DB, ML_HEADS, ML_DK), m=m_s[:, 0, :ML_HEADS],
                          v=v_s.reshape(DB, TS, -1)))

    y = _final_call(rows, x, peer_out, final_norm, mod, depth - 1)
    stk = lambda outs, name: jnp.stack([o[name] for o in outs], axis=0)
    return (y[:Np].reshape(B, T, D), y[Np:].reshape(DB, TS, D),
            stk(new_p, 'ckv'), stk(new_p, 'kr'), stk(new_p, 'S'), stk(new_p, 'C'), stk(new_p, 'n'), stk(new_p, 'm'),
            stk(new_s, 'ckv'), stk(new_s, 'kr'), stk(new_s, 'S'), stk(new_s, 'C'), stk(new_s, 'n'), stk(new_s, 'm'),
            stk(new_s, 'v'))
```

```python
import functools
import math

import numpy as np
import jax
import jax.numpy as jnp
from jax import lax
from jax.experimental import pallas as pl
from jax.experimental.pallas import tpu as pltpu

f32 = jnp.float32
bf16 = jnp.bfloat16

D_MODEL = 2048
EPS = 1e-6
ADA_CHUNKS = 6
GLA_HEADS, GLA_DK, GLA_DV, GLA_RANK, GLA_TAU, GLA_CHUNK = 4, 64, 128, 16, 16.0, 16
MLA_HEADS, MLA_Q_RANK, MLA_KV_RANK, MLA_NOPE, MLA_ROPE, MLA_V = 8, 384, 128, 64, 32, 64
MLA_SCALE = (MLA_NOPE + MLA_ROPE) ** -0.5
ROPE_THETA = 10000.0
ML_HEADS, ML_DK, ML_DV, ML_CHUNK = 4, 64, 128, 64
SG_GROUPS, SG_CHUNK, SG_WIDTH = 4, 128, 512
PEER_HEADS, PEER_NKEYS, PEER_DKEY, PEER_TOPK = 8, 128, 128, 16
PEER_EXPERTS = PEER_NKEYS * PEER_NKEYS
BRANCH_WIDTH = 512
N_BRANCH = 4

W_A = 1664
W_B = 640
W_C = 1792
W_D = 1024

NEG = -1e30
VMEM_LIMIT_V7X = 56 * 1024 * 1024
PEER_PITCH_PAD = 8
PEER_TOKEN_UNROLL = 32

def _cparams(*sem):
    return pltpu.CompilerParams(dimension_semantics=sem, vmem_limit_bytes=VMEM_LIMIT_V7X)


def _bf(x):
    return x.astype(bf16)


def _dot(a, b):
    return jnp.dot(a, b, preferred_element_type=f32)


def _dot_nt(a, b):
    return lax.dot_general(a, b, (((1,), (1,)), ((), ())), preferred_element_type=f32)


def _split2(x):
    hi = _bf(x)
    lo = _bf(x - hi.astype(f32))
    return hi, lo


def _dot3(a, b):
    ah, al = _split2(a)
    bh, bl = _split2(b)
    return _dot(ah, bh) + (_dot(ah, bl) + _dot(al, bh))


def _dot2b(a, b_bf):
    ah, al = _split2(a)
    return _dot(ah, b_bf) + _dot(al, b_bf)


def _sigmoid(x):
    return 1.0 / (1.0 + jnp.exp(-x))


def _log_sigmoid(x):
    return jnp.minimum(x, 0.0) - jnp.log1p(jnp.exp(-jnp.abs(x)))


def _gelu(x):
    return jax.nn.gelu(x)


def _rms(x, g):
    return x * lax.rsqrt(jnp.mean(x * x, axis=-1, keepdims=True) + EPS) * g


def _iota(shape, dim):
    return lax.broadcasted_iota(jnp.int32, shape, dim)


def _seg_cumsum(x, rowmod, L):
    s = 1
    while s < L:
        x = x + jnp.where(rowmod >= s, pltpu.roll(x, s, 0), 0.0)
        s *= 2
    return x


def _seg_last(x, rowmod, L):
    R = x.shape[0]
    s = 1
    while s < L:
        x = jnp.where(rowmod + s < L, pltpu.roll(x, R - s, 0), x)
        s *= 2
    return x


def _expand_heads(x, n_heads, width):
    R = x.shape[0]
    lane_head = _iota((R, n_heads * width), 1) // width
    out = jnp.zeros((R, n_heads * width), f32)
    for h in range(n_heads):
        out = jnp.where(lane_head == h, x[:, h:h + 1], out)
    return out


def _pick_block(n, cands):
    for c in cands:
        if n % c == 0:
            return c
    raise ValueError(f"no block size in {cands} divides {n}")


def _ada_kernel(c_ref, w_ref, b_ref, o_ref):
    c = c_ref[...]
    a = _bf(c * _sigmoid(c))
    o_ref[...] = _dot(a, _bf(w_ref[...])) + b_ref[...]


def _ada_call(c_rows, w_ada, b_ada):
    L, D, N6 = w_ada.shape
    Rc = c_rows.shape[0]
    tn = 1024
    return pl.pallas_call(
        _ada_kernel, name="ada_mod",
        grid=(L, N6 // tn),
        in_specs=[pl.BlockSpec((Rc, D), lambda l, j: (0, 0)),
                  pl.BlockSpec((None, D, tn), lambda l, j: (l, 0, j)),
                  pl.BlockSpec((None, 1, tn), lambda l, j: (l, 0, j))],
        out_specs=pl.BlockSpec((None, Rc, tn), lambda l, j: (l, 0, j)),
        out_shape=jax.ShapeDtypeStruct((L, Rc, N6), f32),
        compiler_params=_cparams("arbitrary", "arbitrary"),
    )(c_rows, w_ada, b_ada.reshape(L, 1, N6))


class _Rows:
    def __init__(self, B, T, DB, TS, block_sizes=(256, 128)):
        self.B, self.T, self.DB, self.TS = B, T, DB, TS
        self.Np, self.Ns = B * T, DB * TS
        self.N = self.Np + self.Ns
        self.R = _pick_block(math.gcd(T, self.Ns), block_sizes)
        self.nPb = self.Np // self.R
        self.nSb = self.Ns // self.R
        self.nb = self.nPb + self.nSb
        self.blocks_per_seq = T // self.R

    def mod_specs(self, l, chunk):
        R, nPb, nSb, Ns = self.R, self.nPb, self.nSb, self.Ns
        s = pl.BlockSpec((None, R, D_MODEL), lambda i: (l, jnp.clip(i - nPb, 0, nSb - 1), chunk))
        p = pl.BlockSpec((None, 8, D_MODEL), lambda i: (l, Ns // 8, chunk))
        return s, p

    def pick(self, i, s_ref, p_ref):
        b = jnp.minimum(i // self.blocks_per_seq, self.B - 1)
        return jnp.where(i >= self.nPb, s_ref[...], p_ref[pl.ds(b, 1), :])


def _modnorm_kernel(*refs, rows, with_peer):
    i = pl.program_id(0)
    if with_peer:
        x_ref, p_ref, g2s, g2p, g_ref, shs_ref, shp_ref, scs_ref, scp_ref, xo_ref, h_ref = refs
        x = x_ref[...] + rows.pick(i, g2s, g2p) * p_ref[...]
        xo_ref[...] = x
    else:
        x_ref, g_ref, shs_ref, shp_ref, scs_ref, scp_ref, h_ref = refs
        x = x_ref[...]
    sh = rows.pick(i, shs_ref, shp_ref)
    sc = rows.pick(i, scs_ref, scp_ref)
    h_ref[...] = _bf(_rms(x, g_ref[...]) * (1.0 + sc) + sh)


def _modnorm_call(rows, x, peer_out, gain, mod, l):
    R = rows.R
    shs, shp = rows.mod_specs(l, 0)
    scs, scp = rows.mod_specs(l, 1)
    row = pl.BlockSpec((R, D_MODEL), lambda i: (i, 0))
    vec = pl.BlockSpec((1, D_MODEL), lambda i: (0, 0))
    h_shape = jax.ShapeDtypeStruct((rows.N, D_MODEL), bf16)
    if peer_out is None:
        in_specs = [row, vec, shs, shp, scs, scp]
        args = (x, gain.reshape(1, D_MODEL), mod, mod, mod, mod)
        out_specs, out_shape = row, h_shape
    else:
        g2s, g2p = rows.mod_specs(l - 1, 5)
        in_specs = [row, row, g2s, g2p, vec, shs, shp, scs, scp]
        args = (x, peer_out, mod, mod, gain.reshape(1, D_MODEL), mod, mod, mod, mod)
        out_specs = [row, row]
        out_shape = [jax.ShapeDtypeStruct((rows.N, D_MODEL), f32), h_shape]
    return pl.pallas_call(
        functools.partial(_modnorm_kernel, rows=rows, with_peer=peer_out is not None), name="modnorm",
        grid=(rows.nb,), in_specs=in_specs, out_specs=out_specs, out_shape=out_shape,
        compiler_params=_cparams("arbitrary"),
    )(*args)


def _mm_kernel(x_ref, wt_ref, o_ref):
    o_ref[...] = _dot_nt(x_ref[...], wt_ref[...]).astype(o_ref.dtype)


def _mm_call(x, wt, out_dtype=f32):
    M, K = x.shape
    N = wt.shape[0]
    tm = _pick_block(M, (1088, 1024, 640, 512, 256, 128))
    return pl.pallas_call(
        _mm_kernel, name="mixer_proj",
        grid=(M // tm,),
        in_specs=[pl.BlockSpec((tm, K), lambda i: (i, 0)),
                  pl.BlockSpec((N, K), lambda i: (0, 0))],
        out_specs=pl.BlockSpec((tm, N), lambda i: (i, 0)),
        out_shape=jax.ShapeDtypeStruct((M, N), out_dtype),
        compiler_params=_cparams("arbitrary"),
    )(x, wt)


def _gla_kernel(*refs, R, L, carry):
    if carry:
        z_ref, wa2_ref, ba_ref, gn_ref, y_ref, so_ref, s_scr = refs
    else:
        z_ref, wa2_ref, ba_ref, gn_ref, s0_ref, y_ref, so_ref = refs
    H, DK, DV = GLA_HEADS, GLA_DK, GLA_DV
    HK, HV = H * DK, H * DV
    nchunk = R // L
    z = z_ref[...]
    q = z[:, 0:HK] * (DK ** -0.5)
    k = z[:, HK:2 * HK]
    v = z[:, 2 * HK:2 * HK + HV]
    r = z[:, 2 * HK + HV:2 * HK + 2 * HV]
    ag = z[:, 2 * HK + 2 * HV:2 * HK + 2 * HV + 128]
    la = _log_sigmoid(_dot3(ag, wa2_ref[...]) + ba_ref[...]) * (1.0 / GLA_TAU)
    rowmod = _iota((R, HK), 0) % L
    b = _seg_cumsum(la, rowmod, L)
    tot = _seg_last(b, rowmod, L)

    sel_row_head = _iota((HK, 128), 0) // DK
    sel_lane = _iota((HK, 128), 1)
    att = jnp.zeros((R, 128), f32)
    for d in range(L):
        ks = k if d == 0 else pltpu.roll(k, d, 0)
        bs = b if d == 0 else pltpu.roll(b, d, 0)
        p = q * ks * jnp.exp(jnp.where(rowmod >= d, b - bs, NEG))
        att = att + _dot2b(p, _bf(sel_lane == d * H + sel_row_head))
    ex_row = _iota((128, HV), 0)
    ex_lane_head = _iota((128, HV), 1) // DV
    o = jnp.zeros((R, HV), f32)
    for d in range(L):
        vs = v if d == 0 else pltpu.roll(v, d, 0)
        o = o + _dot2b(att, _bf(ex_row == d * H + ex_lane_head)) * vs

    qt = _bf(q * jnp.exp(b))
    kd = k * jnp.exp(tot - b)
    kdT = _bf(kd.T)
    totT = tot.T
    vb = _bf(v)
    blockdiag = (_iota((HK, HV), 0) // DK) == (_iota((HK, HV), 1) // DV)
    row_chunk = _iota((R, HK), 0) // L
    col_chunk = _iota((HK, R), 1) // L
    if carry:
        @pl.when(pl.program_id(1) == 0)
        def _():
            s_scr[...] = jnp.zeros((HK, HV), f32)
        S = s_scr[...]
    for c in range(nchunk):
        if not carry:
            rows_ = [jnp.concatenate([s0_ref[c, h] if g == h else jnp.zeros((DK, DV), f32) for g in range(H)], axis=1)
                     for h in range(H)]
            S = jnp.concatenate(rows_, axis=0)
        o = o + _dot(jnp.where(row_chunk == c, qt, jnp.zeros_like(qt)), _bf(S))
        U = _dot(jnp.where(col_chunk == c, kdT, jnp.zeros_like(kdT)), vb)
        dcol = jnp.exp(totT[:, c * L:c * L + 1])
        S = dcol * S + jnp.where(blockdiag, U, 0.0)
        if not carry:
            for h in range(H):
                so_ref[c, h] = S[h * DK:(h + 1) * DK, h * DV:(h + 1) * DV]
    if carry:
        s_scr[...] = S

        @pl.when(pl.program_id(1) == pl.num_programs(1) - 1)
        def _():
            for h in range(H):
                so_ref[h] = S[h * DK:(h + 1) * DK, h * DV:(h + 1) * DV]

    gn = gn_ref[...]
    outs = []
    for h in range(H):
        oh = o[:, h * DV:(h + 1) * DV]
        outs.append(_rms(oh, gn[:, h * DV:(h + 1) * DV]))
    y_ref[...] = _bf(jnp.concatenate(outs, axis=1) * (r * _sigmoid(r)))


def _gla_call(zA, row0, B, T, L, wa2p, ba, gn, s0):
    H, DK, DV = GLA_HEADS, GLA_DK, GLA_DV
    carry = s0 is None
    if carry:
        R = _pick_block(T, (256, 128))
        nt = T // R
        grid = (B, nt)
        rb0 = row0 // R
        in_specs = [pl.BlockSpec((R, W_A), lambda b, t: (rb0 + b * nt + t, 0)),
                    pl.BlockSpec((128, H * DK), lambda b, t: (0, 0)),
                    pl.BlockSpec((1, H * DK), lambda b, t: (0, 0)),
                    pl.BlockSpec((1, H * DV), lambda b, t: (0, 0))]
        out_specs = [pl.BlockSpec((R, H * DV), lambda b, t: (b * nt + t, 0)),
                     pl.BlockSpec((None, H, DK, DV), lambda b, t: (b, 0, 0, 0))]
        scratch = [pltpu.VMEM((H * DK, H * DV), f32)]
        args = (zA, wa2p, ba, gn)
        sem = ("arbitrary", "arbitrary")
    else:
        R = 128
        nb = (B * T) // R
        cpb = R // L
        grid = (nb,)
        rb0 = row0 // R
        in_specs = [pl.BlockSpec((R, W_A), lambda i: (rb0 + i, 0)),
                    pl.BlockSpec((128, H * DK), lambda i: (0, 0)),
                    pl.BlockSpec((1, H * DK), lambda i: (0, 0)),
                    pl.BlockSpec((1, H * DV), lambda i: (0, 0)),
                    pl.BlockSpec((cpb, H, DK, DV), lambda i: (i, 0, 0, 0))]
        out_specs = [pl.BlockSpec((R, H * DV), lambda i: (i, 0)),
                     pl.BlockSpec((cpb, H, DK, DV), lambda i: (i, 0, 0, 0))]
        scratch = []
        args = (zA, wa2p, ba, gn, s0)
        sem = ("arbitrary",)
    return pl.pallas_call(
        functools.partial(_gla_kernel, R=R, L=L, carry=carry), name="gla_prompt" if carry else "gla_sample",
        grid=grid, in_specs=in_specs, out_specs=out_specs,
        out_shape=[jax.ShapeDtypeStruct((B * T, H * DV), bf16),
                   jax.ShapeDtypeStruct((B, H, DK, DV), f32)],
        scratch_shapes=scratch,
        compiler_params=_cparams(*sem),
    )(*args)


def _mlstm_kernel(*refs, R, L, carry):
    if carry:
        z_ref, bi_ref, bf_ref, gn_ref, y_ref, co_ref, no_ref, mo_ref, c_scr, n_scr, m_scr = refs
    else:
        z_ref, bi_ref, bf_ref, gn_ref, c0_ref, n0_ref, m0_ref, y_ref, co_ref, no_ref, mo_ref = refs
    H, DK, DV = ML_HEADS, ML_DK, ML_DV
    HK, HV = H * DK, H * DV
    nchunk = R // L
    z = z_ref[...]
    q = z[:, 0:HK]
    k = z[:, HK:2 * HK] * (DK ** -0.5)
    v = z[:, 2 * HK:2 * HK + HV]
    og = z[:, 2 * HK + HV:2 * HK + 2 * HV]
    ig = z[:, 2 * HK + 2 * HV:2 * HK + 2 * HV + 128] + bi_ref[...]
    fg = z[:, 2 * HK + 2 * HV + 128:2 * HK + 2 * HV + 256] + bf_ref[...]
    rowmod = _iota((R, 128), 0) % L
    row_chunk128 = _iota((R, 128), 0) // L
    F = _seg_cumsum(_log_sigmoid(fg), rowmod, L)
    FT = F.T
    IT = ig.T

    if carry:
        @pl.when(pl.program_id(1) == 0)
        def _():
            c_scr[...] = jnp.zeros((HK, HV), f32)
            n_scr[...] = jnp.zeros((1, HK), f32)
            m_scr[...] = jnp.zeros((1, 128), f32)

    tpos = _iota((R, R), 0)
    spos = _iota((R, R), 1)
    causal = (tpos // L == spos // L) & (spos <= tpos)
    qb = _bf(q)
    kb = _bf(k)
    vb = _bf(v)
    lane_head_k = _iota((R, HK), 1) // DK

    Dms, qks, MX = [], [], jnp.zeros((R, 128), f32)
    lane128 = _iota((R, 128), 1)
    for h in range(H):
        Dm = jnp.where(causal, F[:, h:h + 1] - FT[h:h + 1, :] + IT[h:h + 1, :], NEG)
        Dms.append(Dm)
        MX = jnp.where(lane128 == h, jnp.max(Dm, axis=1, keepdims=True), MX)
        qks.append(_dot_nt(jnp.where(lane_head_k == h, qb, jnp.zeros_like(qb)), kb))

    if carry:
        Mprev = jnp.zeros((R, 128), f32)
        m_run = m_scr[...]
        for c in range(nchunk):
            Mprev = jnp.where(row_chunk128 == c, m_run, Mprev)
            last = c * L + L - 1
            m_run = jnp.maximum(F[last:last + 1, :] + m_run, MX[last:last + 1, :])
    else:
        Mprev = jnp.zeros((R, 128), f32)
        for c in range(nchunk):
            Mprev = jnp.where(row_chunk128 == c, m0_ref[c], Mprev)
    Mt = jnp.maximum(F + Mprev, MX)
    Mnew = _seg_last(Mt, rowmod, L)
    Fend = _seg_last(F, rowmod, L)
    Wprev = jnp.exp(F + Mprev - Mt)
    Wsrc = jnp.exp(Fend - F + ig - Mnew)
    Wold = jnp.exp(Fend + Mprev - Mnew)
    Emt = jnp.exp(-Mt)

    kw = k * _expand_heads(Wsrc, H, DK)
    kwT = _bf(kw.T)
    Wold_k = _expand_heads(Wold, H, DK)
    Wold_v = _expand_heads(Wold, H, DV)
    blockdiag = (_iota((HK, HV), 0) // DK) == (_iota((HK, HV), 1) // DV)
    row_chunk = _iota((R, HK), 0) // L
    col_chunk = _iota((HK, R), 1) // L

    qC = jnp.zeros((R, HV), f32)
    Nrows = jnp.zeros((R, HK), f32)
    if carry:
        C = c_scr[...]
        n = n_scr[...]
    for c in range(nchunk):
        if not carry:
            rows_ = [jnp.concatenate([c0_ref[c, h] if g == h else jnp.zeros((DK, DV), f32) for g in range(H)], axis=1)
                     for h in range(H)]
            C = jnp.concatenate(rows_, axis=0)
            n = n0_ref[c]
        in_chunk = row_chunk == c
        qC = qC + _dot(jnp.where(in_chunk, qb, jnp.zeros_like(qb)), _bf(C))
        Nrows = jnp.where(in_chunk, n, Nrows)
        last = c * L + L - 1
        U = _dot(jnp.where(col_chunk == c, kwT, jnp.zeros_like(kwT)), vb)
        C = Wold_v[last:last + 1, :] * C + jnp.where(blockdiag, U, 0.0)
        n = Wold_k[last:last + 1, :] * n + jnp.sum(jnp.where(in_chunk, kw, 0.0), axis=0, keepdims=True)
        if not carry:
            for h in range(H):
                co_ref[c, h] = C[h * DK:(h + 1) * DK, h * DV:(h + 1) * DV]
            no_ref[c] = n
            mo_ref[c] = Mnew[last:last + 1, :]
    if carry:
        c_scr[...] = C
        n_scr[...] = n
        m_scr[...] = Mnew[R - 1:R, :]

        @pl.when(pl.program_id(1) == pl.num_programs(1) - 1)
        def _():
            for h in range(H):
                co_ref[h] = C[h * DK:(h + 1) * DK, h * DV:(h + 1) * DV]
            no_ref[...] = n
            mo_ref[...] = Mnew[R - 1:R, :]

    qn = _dot2b(q * Nrows, _bf(_iota((HK, 128), 1) == _iota((HK, 128), 0) // DK))
    gn = gn_ref[...]
    outs = []
    for h in range(H):
        A = jnp.exp(Dms[h] - Mt[:, h:h + 1]) * qks[h]
        num = Wprev[:, h:h + 1] * qC[:, h * DV:(h + 1) * DV] + _dot(_bf(A), vb[:, h * DV:(h + 1) * DV])
        nq = Wprev[:, h:h + 1] * qn[:, h:h + 1] + jnp.sum(A, axis=1, keepdims=True)
        hh = num / jnp.maximum(jnp.abs(nq), Emt[:, h:h + 1])
        outs.append(_rms(hh, gn[:, h * DV:(h + 1) * DV]))
    y_ref[...] = _bf(_sigmoid(og) * jnp.concatenate(outs, axis=1))


def _mlstm_call(zC, row0, B, T, L, bi, bfg, gn, st):
    H, DK, DV = ML_HEADS, ML_DK, ML_DV
    carry = st is None
    R = 128
    out_shape = [jax.ShapeDtypeStruct((B * T, H * DV), bf16),
                 jax.ShapeDtypeStruct((B, H, DK, DV), f32),
                 jax.ShapeDtypeStruct((B, 1, H * DK), f32),
                 jax.ShapeDtypeStruct((B, 1, 128), f32)]
    rb0 = row0 // R
    if carry:
        nt = T // R
        grid = (B, nt)
        c2 = lambda b, t: (0, 0)
        in_specs = [pl.BlockSpec((R, W_C), lambda b, t: (rb0 + b * nt + t, 0)),
                    pl.BlockSpec((1, 128), c2), pl.BlockSpec((1, 128), c2), pl.BlockSpec((1, H * DV), c2)]
        out_specs = [pl.BlockSpec((R, H * DV), lambda b, t: (b * nt + t, 0)),
                     pl.BlockSpec((None, H, DK, DV), lambda b, t: (b, 0, 0, 0)),
                     pl.BlockSpec((None, 1, H * DK), lambda b, t: (b, 0, 0)),
                     pl.BlockSpec((None, 1, 128), lambda b, t: (b, 0, 0))]
        scratch = [pltpu.VMEM((H * DK, H * DV), f32), pltpu.VMEM((1, H * DK), f32), pltpu.VMEM((1, 128), f32)]
        args = (zC, bi, bfg, gn)
        sem = ("arbitrary", "arbitrary")
    else:
        c0, n0, m0 = st
        nb = (B * T) // R
        cpb = R // L
        grid = (nb,)
        c2 = lambda i: (0, 0)
        in_specs = [pl.BlockSpec((R, W_C), lambda i: (rb0 + i, 0)),
                    pl.BlockSpec((1, 128), c2), pl.BlockSpec((1, 128), c2), pl.BlockSpec((1, H * DV), c2),
                    pl.BlockSpec((cpb, H, DK, DV), lambda i: (i, 0, 0, 0)),
                    pl.BlockSpec((cpb, 1, H * DK), lambda i: (i, 0, 0)),
                    pl.BlockSpec((cpb, 1, 128), lambda i: (i, 0, 0))]
        out_specs = [pl.BlockSpec((R, H * DV), lambda i: (i, 0)),
                     pl.BlockSpec((cpb, H, DK, DV), lambda i: (i, 0, 0, 0)),
                     pl.BlockSpec((cpb, 1, H * DK), lambda i: (i, 0, 0)),
                     pl.BlockSpec((cpb, 1, 128), lambda i: (i, 0, 0))]
        scratch = []
        args = (zC, bi, bfg, gn, c0, n0, m0)
        sem = ("arbitrary",)
    return pl.pallas_call(
        functools.partial(_mlstm_kernel, R=R, L=L, carry=carry), name="mlstm_prompt" if carry else "mlstm_sample",
        grid=grid, in_specs=in_specs, out_specs=out_specs, out_shape=out_shape,
        scratch_shapes=scratch, compiler_params=_cparams(*sem),
    )(*args)


def _sgu_kernel(z_ref, gv_ref, ws_ref, bcol_ref, y_ref, *v_out, nch):
    G = SG_GROUPS
    W = SG_WIDTH
    cw = W // G
    tril = _iota((SG_CHUNK, SG_CHUNK), 1) <= _iota((SG_CHUNK, SG_CHUNK), 0)
    for c in range(nch):
        rs = slice(c * SG_CHUNK, (c + 1) * SG_CHUNK)
        u = z_ref[rs, 0:W]
        vn = _rms(_gelu(z_ref[rs, W:2 * W]), gv_ref[...])
        if v_out:
            v_out[0][rs, :] = vn
        vb = _bf(vn)
        bcol = bcol_ref[...]
        outs = []
        for g in range(G):
            wc = _bf(jnp.where(tril, ws_ref[g], 0.0))
            outs.append(_dot(wc, vb[:, g * cw:(g + 1) * cw]) + bcol[:, g:g + 1])
        y_ref[rs, :] = _bf(_gelu(u) * jnp.concatenate(outs, axis=1))


def _sgu_call(zD, row0, n_rows, gv, ws, bcol, want_v):
    nch = _pick_block(n_rows // SG_CHUNK, (4, 2, 1))
    Rb = nch * SG_CHUNK
    rb0 = row0 // Rb
    out_shape = [jax.ShapeDtypeStruct((n_rows, SG_WIDTH), bf16)]
    out_specs = [pl.BlockSpec((Rb, SG_WIDTH), lambda i: (i, 0))]
    if want_v:
        out_shape.append(jax.ShapeDtypeStruct((n_rows, SG_WIDTH), f32))
        out_specs.append(pl.BlockSpec((Rb, SG_WIDTH), lambda i: (i, 0)))
    return pl.pallas_call(
        functools.partial(_sgu_kernel, nch=nch), name="sgu",
        grid=(n_rows // Rb,),
        in_specs=[pl.BlockSpec((Rb, W_D), lambda i: (rb0 + i, 0)),
                  pl.BlockSpec((1, SG_WIDTH), lambda i: (0, 0)),
                  pl.BlockSpec((SG_GROUPS, SG_CHUNK, SG_CHUNK), lambda i: (0, 0, 0)),
                  pl.BlockSpec((SG_CHUNK, 128), lambda i: (0, 0))],
        out_specs=out_specs, out_shape=out_shape,
        compiler_params=_cparams("arbitrary"),
    )(zD, gv, ws, bcol)


def _mla_prep_kernel(z_ref, cos_ref, sin_ref, gq_ref, gkv_ref, wqn_ref, wqr_ref, wa_ref, pb_ref, rot_ref,
                     q_ref, k_ref):
    z = z_ref[...]
    dq = z[:, 0:MLA_Q_RANK]
    dkv = z[:, MLA_Q_RANK:MLA_Q_RANK + MLA_KV_RANK]
    kr = z[:, MLA_Q_RANK + MLA_KV_RANK:MLA_Q_RANK + MLA_KV_RANK + 128]
    cos = cos_ref[...]
    sin = sin_ref[...]
    rot = rot_ref[...]
    dqn = _bf(_rms(dq, gq_ref[...]))
    qn = _dot(dqn, wqn_ref[...])
    qr = _dot(dqn, wqr_ref[...])
    qrope = qr * cos + _dot2b(qr, rot) * sin
    q_ref[...] = _bf(_dot(_bf(qn), wa_ref[...]) + _dot(_bf(qrope), pb_ref[...]))
    ckv = _rms(dkv, gkv_ref[...])
    krope = kr * cos[:, 0:128] + _dot2b(kr, rot[0:128, 0:128]) * sin[:, 0:128]
    k_ref[...] = jnp.concatenate([ckv, krope], axis=1)


def _mla_prep_call(rows, zB, cos, sin, gq, gkv, wqn, wqr, wa, pb, rot):
    R = rows.R
    HC = MLA_HEADS * 256
    c2 = lambda i: (0, 0)
    return pl.pallas_call(
        _mla_prep_kernel, name="mla_prep",
        grid=(rows.nb,),
        in_specs=[pl.BlockSpec((R, W_B), lambda i: (i, 0)),
                  pl.BlockSpec((R, 256), lambda i: (i, 0)),
                  pl.BlockSpec((R, 256), lambda i: (i, 0)),
                  pl.BlockSpec((1, MLA_Q_RANK), c2), pl.BlockSpec((1, MLA_KV_RANK), c2),
                  pl.BlockSpec((MLA_Q_RANK, MLA_HEADS * MLA_NOPE), c2),
                  pl.BlockSpec((MLA_Q_RANK, MLA_HEADS * MLA_ROPE), c2),
                  pl.BlockSpec((MLA_HEADS * MLA_NOPE, HC), c2),
                  pl.BlockSpec((MLA_HEADS * MLA_ROPE, HC), c2),
                  pl.BlockSpec((256, 256), c2)],
        out_specs=[pl.BlockSpec((R, HC), lambda i: (i, 0)), pl.BlockSpec((R, 256), lambda i: (i, 0))],
        out_shape=[jax.ShapeDtypeStruct((rows.N, HC), bf16), jax.ShapeDtypeStruct((rows.N, 256), f32)],
        compiler_params=_cparams("arbitrary"),
    )(zB, cos, sin, gq, gkv, wqn, wqr, wa, pb, rot)


def _flash_kernel(qtab_ref, ktab_ref, q_ref, k_ref, wuv_ref, y_ref, qs_scr, m_scr, l_scr, acc_scr, *, tq, tk):
    qi = qtab_ref[pl.program_id(1)]
    ki = ktab_ref[pl.program_id(1)]
    H = MLA_HEADS
    NR = H * tq
    diag = (qi * tq + tq - 1) // tk

    @pl.when(ki == 0)
    def _():
        for h in range(H):
            qs_scr[h * tq:(h + 1) * tq, :] = q_ref[:, h * 256:(h + 1) * 256]
        m_scr[...] = jnp.full(m_scr.shape, NEG, f32)
        l_scr[...] = jnp.zeros(l_scr.shape, f32)
        acc_scr[...] = jnp.zeros(acc_scr.shape, f32)

    def step(masked):
        kb = _bf(k_ref[...])
        s = _dot_nt(qs_scr[...], kb) * MLA_SCALE
        if masked:
            qpos = qi * tq + _iota((NR, tk), 0) % tq
            s = jnp.where(ki * tk + _iota((NR, tk), 1) <= qpos, s, NEG)
        m_prev = m_scr[...]
        m_next = jnp.maximum(m_prev, jnp.max(s, axis=1, keepdims=True))
        alpha = jnp.exp(m_prev - m_next)
        p = jnp.exp(s - jnp.concatenate([m_next] * (tk // 128), axis=1))
        l_scr[...] = alpha * l_scr[...] + jnp.sum(p, axis=1, keepdims=True)
        acc_scr[...] = alpha * acc_scr[...] + _dot(_bf(p), kb[:, 0:MLA_KV_RANK])
        m_scr[...] = m_next

    pl.when(ki < diag)(functools.partial(step, False))
    pl.when(ki == diag)(functools.partial(step, True))

    @pl.when(ki == diag)
    def _():
        o = _bf(acc_scr[...] / l_scr[...])
        y = jnp.zeros((tq, MLA_HEADS * MLA_V), f32)
        for h in range(H):
            y = y + _dot(o[h * tq:(h + 1) * tq, :], wuv_ref[h])
        y_ref[...] = _bf(y)


def _flash_call(qcat, kcat, wuv, B, T):
    tq = 128
    tk = _pick_block(T, (512, 256, 128))
    nq, nk = T // tq, T // tk
    H = MLA_HEADS
    pairs = [(i, j) for i in range(nq) for j in range((i * tq + tq - 1) // tk + 1)]
    qtab = jnp.asarray([p[0] for p in pairs], jnp.int32)
    ktab = jnp.asarray([p[1] for p in pairs], jnp.int32)
    grid_spec = pltpu.PrefetchScalarGridSpec(
        num_scalar_prefetch=2,
        grid=(B, len(pairs)),
        in_specs=[pl.BlockSpec((tq, H * 256), lambda b, p, qt, kt: (b * nq + qt[p], 0)),
                  pl.BlockSpec((tk, 256), lambda b, p, qt, kt: (b * nk + kt[p], 0)),
                  pl.BlockSpec((H, MLA_KV_RANK, H * MLA_V), lambda b, p, qt, kt: (0, 0, 0))],
        out_specs=pl.BlockSpec((tq, H * MLA_V), lambda b, p, qt, kt: (b * nq + qt[p], 0)),
        scratch_shapes=[pltpu.VMEM((H * tq, 256), bf16), pltpu.VMEM((H * tq, 128), f32),
                        pltpu.VMEM((H * tq, 128), f32), pltpu.VMEM((H * tq, MLA_KV_RANK), f32)],
    )
    return pl.pallas_call(
        functools.partial(_flash_kernel, tq=tq, tk=tk), name="mla_flash",
        grid_spec=grid_spec,
        out_shape=jax.ShapeDtypeStruct((B * T, H * MLA_V), bf16),
        compiler_params=_cparams("arbitrary", "arbitrary"),
    )(qtab, ktab, qcat, kcat, wuv)


def _paged_kernel(pt_ref, q_ref, kn_ref, wuv_ref, ck_hbm, krt_hbm, y_ref,
                  ck_buf, krt_buf, kb_scr, sem, *, l, n_pages, page, TS, chunk):
    b = pl.program_id(0)
    nb = pl.num_programs(0)
    H = MLA_HEADS
    NR = TS * H
    n_keys = n_pages * page
    slot = b % 2

    def page_copies(bb, sl, p):
        pg = pt_ref[bb * n_pages + p]
        keys_ = pl.ds(p * page, page)
        return (pltpu.make_async_copy(ck_hbm.at[l, pg], ck_buf.at[sl, keys_, :], sem.at[0, sl]),
                pltpu.make_async_copy(krt_hbm.at[l, pg], krt_buf.at[sl, :, keys_], sem.at[1, sl]))

    def start_all(bb, sl):
        for p in range(n_pages):
            for cp in page_copies(bb, sl, p):
                cp.start()

    def wait_all(bb, sl):
        for p in range(n_pages):
            for cp in page_copies(bb, sl, p):
                cp.wait()

    @pl.when(b == 0)
    def _():
        start_all(0, 0)

    @pl.when(b + 1 < nb)
    def _():
        start_all(b + 1, 1 - slot)

    wait_all(b, slot)

    q = q_ref[...]
    ql = q[:, 0:MLA_KV_RANK]
    qr = q[:, MLA_KV_RANK:MLA_KV_RANK + MLA_ROPE]
    kn = _bf(kn_ref[...])
    s_new = _dot_nt(q, kn) * MLA_SCALE
    ok = (_iota((NR, 8), 1) <= _iota((NR, 8), 0) // H) & (_iota((NR, 8), 1) < TS)
    s_new = jnp.where(ok, s_new, NEG)
    m = jnp.max(s_new, axis=1, keepdims=True)
    ss = []
    for c in range(n_keys // chunk):
        rows_ = pl.ds(c * chunk, chunk)
        kb_scr[rows_, :] = _bf(ck_buf[slot, rows_, :])
        s = (_dot_nt(ql, kb_scr[rows_, :]) + _dot(qr, _bf(krt_buf[slot, :, rows_]))) * MLA_SCALE
        m = jnp.maximum(m, jnp.max(s, axis=1, keepdims=True))
        ss.append(s)
    e = jnp.exp(s_new - m)
    lsum = jnp.sum(e, axis=1, keepdims=True)
    acc = _dot(_bf(e), kn[:, 0:MLA_KV_RANK])
    for c in range(n_keys // chunk):
        e = jnp.exp(ss[c] - m)
        lsum = lsum + jnp.sum(e, axis=1, keepdims=True)
        acc = acc + _dot(_bf(e), kb_scr[pl.ds(c * chunk, chunk), :])
    o = acc / lsum
    row_head = _iota((NR, MLA_KV_RANK), 0) % H
    y = jnp.zeros((NR, H * MLA_V), f32)
    for h in range(H):
        y = y + _dot(_bf(jnp.where(row_head == h, o, 0.0)), wuv_ref[h])
    y_ref[...] = _bf(jnp.sum(y.reshape(TS, H, H * MLA_V), axis=1))


def _paged_call(page_table, q3, kn3, wuv, cache_ckv, cache_krt, l):
    DB, n_pages = page_table.shape
    TS = q3.shape[1] // MLA_HEADS
    H = MLA_HEADS
    page = cache_ckv.shape[2]
    n_keys = n_pages * page
    chunk = _pick_block(n_keys, (2048, 1024, 512, 256, 128))
    grid_spec = pltpu.PrefetchScalarGridSpec(
        num_scalar_prefetch=1,
        grid=(DB,),
        in_specs=[pl.BlockSpec((None, TS * H, 256), lambda b, pt: (b, 0, 0)),
                  pl.BlockSpec((None, 8, 256), lambda b, pt: (b, 0, 0)),
                  pl.BlockSpec((H, MLA_KV_RANK, H * MLA_V), lambda b, pt: (0, 0, 0)),
                  pl.BlockSpec(memory_space=pl.ANY), pl.BlockSpec(memory_space=pl.ANY)],
        out_specs=pl.BlockSpec((None, TS, H * MLA_V), lambda b, pt: (b, 0, 0)),
        scratch_shapes=[pltpu.VMEM((2, n_keys, MLA_KV_RANK), f32),
                        pltpu.VMEM((2, MLA_ROPE, n_keys), f32),
                        pltpu.VMEM((n_keys, MLA_KV_RANK), bf16),
                        pltpu.SemaphoreType.DMA((2, 2))],
    )
    return pl.pallas_call(
        functools.partial(_paged_kernel, l=l, n_pages=n_pages, page=page, TS=TS, chunk=chunk), name="mla_paged",
        grid_spec=grid_spec,
        out_shape=jax.ShapeDtypeStruct((DB, TS, H * MLA_V), bf16),
        compiler_params=_cparams("arbitrary"),
    )(page_table.reshape(-1), q3, kn3, wuv, cache_ckv, cache_krt)


def _merge_kernel(h_ref, ya_ref, yb_ref, yc_ref, yd_ref, g0, g1, g2, g3, b0, b1, b2, b3, o_ref):
    h = h_ref[...]
    acc = None
    for y_ref, g_ref, b_ref in ((ya_ref, g0, b0), (yb_ref, g1, b1), (yc_ref, g2, b2), (yd_ref, g3, b3)):
        gate = _sigmoid(_dot_nt(h, _bf(g_ref[...])))
        term = gate * _dot(y_ref[...], _bf(b_ref[...]))
        acc = term if acc is None else acc + term
    o_ref[...] = _bf(acc)


def _merge_call(h1, ys, w_in_t, w_br, l):
    N = h1.shape[0]
    D = D_MODEL
    tm = _pick_block(N, (1088, 640, 512, 256, 128))
    tn = 256
    ncol = D // tn
    gate_specs = [pl.BlockSpec((None, tn, D), (lambda i, j, br=br: (l, br * ncol + j, 0))) for br in range(N_BRANCH)]
    br_specs = [pl.BlockSpec((None, None, BRANCH_WIDTH, tn), (lambda i, j, br=br: (l, br, 0, j))) for br in range(N_BRANCH)]
    y_specs = [pl.BlockSpec((tm, BRANCH_WIDTH), lambda i, j: (i, 0)) for _ in range(N_BRANCH)]
    return pl.pallas_call(
        _merge_kernel, name="merge",
        grid=(N // tm, ncol),
        in_specs=[pl.BlockSpec((tm, D), lambda i, j: (i, 0))] + y_specs + gate_specs + br_specs,
        out_specs=pl.BlockSpec((tm, tn), lambda i, j: (i, j)),
        out_shape=jax.ShapeDtypeStruct((N, D), bf16),
        compiler_params=_cparams("arbitrary", "arbitrary"),
    )(h1, *ys, w_in_t, w_in_t, w_in_t, w_in_t, w_br, w_br, w_br, w_br)


def _outproj_kernel(x_ref, mg_ref, wo_ref, g_ref, g1s, g1p, shs, shp, scs, scp, xo_ref, h_ref, hb_ref, *, rows):
    i = pl.program_id(0)
    x = x_ref[...] + rows.pick(i, g1s, g1p) * _dot(mg_ref[...], wo_ref[...])
    xo_ref[...] = x
    h = _rms(x, g_ref[...]) * (1.0 + rows.pick(i, scs, scp)) + rows.pick(i, shs, shp)
    h_ref[...] = h
    hb_ref[...] = _bf(h)


def _outproj_call(rows, x, merged, wo_bf, gain2, mod, l):
    R = rows.R
    D = D_MODEL
    g1s, g1p = rows.mod_specs(l, 2)
    shs, shp = rows.mod_specs(l, 3)
    scs, scp = rows.mod_specs(l, 4)
    return pl.pallas_call(
        functools.partial(_outproj_kernel, rows=rows), name="outproj",
        grid=(rows.nb,),
        in_specs=[pl.BlockSpec((R, D), lambda i: (i, 0)), pl.BlockSpec((R, D), lambda i: (i, 0)),
                  pl.BlockSpec((D, D), lambda i: (0, 0)), pl.BlockSpec((1, D), lambda i: (0, 0)),
                  g1s, g1p, shs, shp, scs, scp],
        out_specs=[pl.BlockSpec((R, D), lambda i: (i, 0))] * 3,
        out_shape=[jax.ShapeDtypeStruct((rows.N, D), f32), jax.ShapeDtypeStruct((rows.N, D), f32),
                   jax.ShapeDtypeStruct((rows.N, D), bf16)],
        compiler_params=_cparams("arbitrary"),
    )(x, merged, wo_bf, gain2.reshape(1, D), mod, mod, mod, mod, mod, mod)


def _peer_candidates():
    K = PEER_TOPK
    pairs = [(k1, k2) for k1 in range(K) for k2 in range(K) if (k1 + 1) * (k2 + 1) <= K]
    n = -(-len(pairs) // 8) * 8
    g1 = np.zeros((n, K), np.float32)
    g2 = np.zeros((n, K), np.float32)
    for r, (k1, k2) in enumerate(pairs):
        g1[r, k1] = 1.0
        g2[r, k2] = 1.0
    return len(pairs), jnp.asarray(g1, bf16), jnp.asarray(g2, bf16)


def _select_rows3(sel_bf, v):
    hi = _bf(v)
    r1 = v - hi.astype(f32)
    mid = _bf(r1)
    lo = _bf(r1 - mid.astype(f32))
    return (_dot(sel_bf, hi) + _dot(sel_bf, mid)) + _dot(sel_bf, lo)


def _peer_score_kernel(h_ref, wqh_ref, wql_ref, k1_ref, k2_ref, g1_ref, g2_ref, a_ref, b_ref, g_ref,
                       v1_scr, i1_scr, v2_scr, i2_scr, top_scr, ea_scr, eb_scr, *, n_cand):
    T = h_ref.shape[0]
    NK, K = PEER_NKEYS, PEER_TOPK
    LT = 128
    NC = g1_ref.shape[0]
    hh, hl = _split2(h_ref[...])
    q = _dot(hh, wqh_ref[...]) + (_dot(hh, wql_ref[...]) + _dot(hl, wqh_ref[...]))
    key_idx = _iota((NK, LT), 0).astype(f32)
    cand_idx = _iota((NC, LT), 0).astype(f32)
    neg_inf = -jnp.inf
    g1 = g1_ref[...]
    g2 = g2_ref[...]

    def topk_into(s, val_scr, idx_scr):
        for kk in range(K):
            m = jnp.max(s, axis=0, keepdims=True)
            pos = jnp.min(jnp.where(s == m, key_idx, float(NK)), axis=0, keepdims=True)
            val_scr[pl.ds(kk, 1), :] = m
            idx_scr[pl.ds(kk, 1), :] = pos
            s = jnp.where(key_idx == pos, neg_inf, s)

    for h in range(PEER_HEADS):
        qh, ql = _split2(q[:, h * PEER_DKEY:(h + 1) * PEER_DKEY])
        sTs = []
        for k_ref in (k1_ref, k2_ref):
            kh, kl = _split2(k_ref[h])
            sTs.append(_dot_nt(kh, qh) + (_dot_nt(kh, ql) + _dot_nt(kl, qh)))
        for lt in range(T // LT):
            cols = slice(lt * LT, (lt + 1) * LT)
            topk_into(sTs[0][:, cols], v1_scr, i1_scr)
            topk_into(sTs[1][:, cols], v2_scr, i2_scr)
            cand = _select_rows3(g1, v1_scr[...]) + _select_rows3(g2, v2_scr[...])
            cand = jnp.where(cand_idx < float(n_cand), cand, neg_inf)
            ca = _dot(g1, _bf(i1_scr[...]))
            cb = _dot(g2, _bf(i2_scr[...]))
            for kk in range(K):
                m = jnp.max(cand, axis=0, keepdims=True)
                pos = jnp.min(jnp.where(cand == m, cand_idx, float(NC)), axis=0, keepdims=True)
                sel = cand_idx == pos
                r = h * K + kk
                top_scr[pl.ds(r, 1), cols] = m
                ea_scr[pl.ds(r, 1), cols] = jnp.sum(jnp.where(sel, ca, 0.0), axis=0, keepdims=True)
                eb_scr[pl.ds(r, 1), cols] = jnp.sum(jnp.where(sel, cb, 0.0), axis=0, keepdims=True)
                cand = jnp.where(sel, neg_inf, cand)
        tops = top_scr[pl.ds(h * K, K), :]
        e = jnp.exp(tops - jnp.max(tops, axis=0, keepdims=True))
        top_scr[pl.ds(h * K, K), :] = e / jnp.sum(e, axis=0, keepdims=True)
    g_ref[...] = top_scr[...].T
    a_ref[...] = ea_scr[...].T
    b_ref[...] = eb_scr[...].T


def _peer_score_call(rows, h2, wqh, wql, k1p, k2p):
    R = rows.R
    D = D_MODEL
    HK = PEER_HEADS * PEER_TOPK
    o_spec = pl.BlockSpec((R, HK), lambda i: (i, 0))
    o_shape = jax.ShapeDtypeStruct((rows.N, HK), f32)
    n_cand, g1, g2 = _peer_candidates()
    return pl.pallas_call(
        functools.partial(_peer_score_kernel, n_cand=n_cand), name="peer_score",
        grid=(rows.nb,),
        in_specs=[pl.BlockSpec((R, D), lambda i: (i, 0)),
                  pl.BlockSpec((D, PEER_HEADS * PEER_DKEY), lambda i: (0, 0)),
                  pl.BlockSpec((D, PEER_HEADS * PEER_DKEY), lambda i: (0, 0)),
                  pl.BlockSpec((PEER_HEADS, PEER_NKEYS, PEER_DKEY), lambda i: (0, 0, 0)),
                  pl.BlockSpec((PEER_HEADS, PEER_NKEYS, PEER_DKEY), lambda i: (0, 0, 0)),
                  pl.BlockSpec(g1.shape, lambda i: (0, 0)), pl.BlockSpec(g2.shape, lambda i: (0, 0))],
        out_specs=[o_spec, o_spec, o_spec],
        out_shape=[o_shape, o_shape, o_shape],
        scratch_shapes=[pltpu.VMEM((PEER_TOPK, 128), f32)] * 4 + [pltpu.VMEM((HK, R), f32)] * 3,
        compiler_params=_cparams("arbitrary"),
    )(h2, wqh, wql, k1p, k2p, g1, g2)


def _peer_mix_kernel(h_ref, ea_ref, eb_ref, g_ref, u_ref, v_ref, o_ref, a_scr, *, T, nch, ac, n_half):
    j = pl.program_id(1)
    NK = PEER_NKEYS
    NA = NK // n_half
    pitch = T + PEER_PITCH_PAD
    half = j // (2 * nch)
    jj = j % (2 * nch)

    @pl.when(j == 0)
    def _():
        o_ref[...] = jnp.zeros(o_ref.shape, f32)

    @pl.when(jj < nch)
    def _():
        res = _gelu(_dot_nt(h_ref[...], u_ref[...]))
        for al in range(ac):
            start = pl.multiple_of((jj * ac + al) * pitch, 8)
            a_scr[pl.ds(start, T), :] = res[:, al * NK:(al + 1) * NK]

    @pl.when(jj == nch)
    def _():
        sub_a = (_iota((NA, NK), 0) + half * NA).astype(f32).astype(bf16)
        sub_b = _iota((NK, NK), 0).astype(f32).astype(bf16)
        one = jnp.ones((NA, NK), bf16)
        zero_a = jnp.zeros((NA, NK), bf16)
        zero_b = jnp.zeros((NK, NK), bf16)

        def gate_matrix(t):
            ia = jnp.broadcast_to(_bf(ea_ref[pl.ds(t, 1), :]), (NA, NK))
            ib = jnp.broadcast_to(_bf(eb_ref[pl.ds(t, 1), :]), (NK, NK))
            gg = jnp.broadcast_to(_bf(g_ref[pl.ds(t, 1), :]), (NK, NK))
            onehot_a = jnp.where(sub_a == ia, one, zero_a)
            gate_b = jnp.where(sub_b == ib, gg, zero_b)
            return _dot_nt(onehot_a, gate_b)

        def tok_group(tg, carry):
            ts = [tg * PEER_TOKEN_UNROLL + u for u in range(PEER_TOKEN_UNROLL)]
            ws = [gate_matrix(t) for t in ts]
            for t, w in zip(ts, ws):
                a_scr[pl.ds(t, NA, stride=pitch), :] = w * a_scr[pl.ds(t, NA, stride=pitch), :]
            return carry

        lax.fori_loop(0, T // PEER_TOKEN_UNROLL, tok_group, 0)

    @pl.when(jj >= nch)
    def _():
        parts = []
        for al in range(ac):
            start = pl.multiple_of(((jj - nch) * ac + al) * pitch, 8)
            parts.append(_bf(a_scr[pl.ds(start, T), :]))
        o_ref[...] += _dot(jnp.concatenate(parts, axis=1), v_ref[...])


def _peer_mix_call(rows, h2b, ea, eb, g, ub_all, vb_all, l):
    T = rows.R
    D = D_MODEL
    NK = PEER_NKEYS
    n_half = 2
    ac = 8
    nch = NK // n_half // ac
    HK = PEER_HEADS * PEER_TOPK
    row = lambda i, j: (i, 0)

    def u_idx(i, j):
        return (l, (j // (2 * nch)) * nch + jnp.minimum(j % (2 * nch), nch - 1), 0)

    def v_idx(i, j):
        return (l, (j // (2 * nch)) * nch + jnp.maximum(j % (2 * nch) - nch, 0), 0)

    return pl.pallas_call(
        functools.partial(_peer_mix_kernel, T=T, nch=nch, ac=ac, n_half=n_half), name="peer_mix",
        grid=(rows.nb, n_half * 2 * nch),
        in_specs=[pl.BlockSpec((T, D), row),
                  pl.BlockSpec((T, HK), row), pl.BlockSpec((T, HK), row), pl.BlockSpec((T, HK), row),
                  pl.BlockSpec((None, ac * NK, D), u_idx),
                  pl.BlockSpec((None, ac * NK, D), v_idx)],
        out_specs=pl.BlockSpec((T, D), row),
        out_shape=jax.ShapeDtypeStruct((rows.N, D), f32),
        scratch_shapes=[pltpu.VMEM((NK // n_half * (T + PEER_PITCH_PAD), NK), f32)],
        compiler_params=_cparams("arbitrary", "arbitrary"),
    )(h2b, ea, eb, g, ub_all, vb_all)


def _final_kernel(x_ref, p_ref, g2s, g2p, g_ref, o_ref, *, rows):
    x = x_ref[...] + rows.pick(pl.program_id(0), g2s, g2p) * p_ref[...]
    o_ref[...] = _rms(x, g_ref[...])


def _final_call(rows, x, peer_out, gain, mod, l_last):
    R, D = rows.R, D_MODEL
    g2s, g2p = rows.mod_specs(l_last, 5)
    row = pl.BlockSpec((R, D), lambda i: (i, 0))
    return pl.pallas_call(
        functools.partial(_final_kernel, rows=rows), name="final_norm", grid=(rows.nb,),
        in_specs=[row, row, g2s, g2p, pl.BlockSpec((1, D), lambda i: (0, 0))],
        out_specs=row,
        out_shape=jax.ShapeDtypeStruct((rows.N, D), f32),
        compiler_params=_cparams("arbitrary"),
    )(x, peer_out, mod, mod, gain.reshape(1, D))


def _mla_constants():
    H, R = MLA_HEADS, MLA_ROPE
    half = R // 2
    rot = np.zeros((H * R, H * R), np.float32)
    for g in range(H):
        for jj in range(half):
            rot[g * R + half + jj, g * R + jj] = -1.0
            rot[g * R + jj, g * R + half + jj] = 1.0
    place = np.zeros((H * R, H * 256), np.float32)
    for g in range(H):
        for jj in range(R):
            place[g * R + jj, g * 256 + MLA_KV_RANK + jj] = 1.0
    return jnp.asarray(rot, bf16), jnp.asarray(place, bf16)


def kernel(x_prompt, x_sample, cache_mla_ckv, cache_mla_kr, state_gla, state_mlstm_C, state_mlstm_n, state_mlstm_m, page_table, c_prompt, c_sample, w_ada, b_ada, norm1, norm2, w_in, gla_wa2, gla_ba, gla_norm, mla_w_uq, mla_gq, mla_gkv, mla_w_uk, mla_w_uv, ml_bi, ml_bf, ml_norm, sg_gv, sg_ws, sg_bs, w_br, w_o, peer_wq, peer_k1, peer_k2, peer_u, peer_v, final_norm):
    D = D_MODEL
    B, T, _ = x_prompt.shape
    DB, TS, _ = x_sample.shape
    depth = w_ada.shape[0]
    n_pages = page_table.shape[1]
    page = cache_mla_ckv.shape[2]
    past_len = n_pages * page
    rows = _Rows(B, T, DB, TS)
    Np, Ns = rows.Np, rows.Ns
    assert Ns % 128 == 0 and T % 128 == 0 and TS == 4

    c_rows = jnp.concatenate([jnp.repeat(c_sample, TS, axis=0), c_prompt,
                              jnp.zeros((16 - B, D), f32)], axis=0)
    mod = _ada_call(c_rows, w_ada, b_ada)

    x = jnp.concatenate([x_prompt.reshape(Np, D), x_sample.reshape(Ns, D)], axis=0)

    inv = ROPE_THETA ** (-jnp.arange(MLA_ROPE // 2, dtype=f32) / (MLA_ROPE // 2))
    pos_p = jnp.arange(T, dtype=jnp.int32)
    pos_s = past_len + jnp.arange(TS, dtype=jnp.int32)

    def table(fn):
        tp = fn(pos_p.astype(f32)[:, None] * inv)
        ts = fn(pos_s.astype(f32)[:, None] * inv)
        rows_ = jnp.concatenate([jnp.tile(tp, (B, 1)), jnp.tile(ts, (DB, 1))], axis=0)
        return jnp.tile(rows_, (1, 2 * MLA_HEADS))

    cos_t, sin_t = table(jnp.cos), table(jnp.sin)
    rot, place = _mla_constants()
    eye_h = jnp.eye(MLA_HEADS, dtype=f32)
    Ls = math.gcd(TS, GLA_CHUNK)
    Lp_gla = math.gcd(T, GLA_CHUNK)
    Lp_ml = math.gcd(T, ML_CHUNK)
    Ls_ml = math.gcd(TS, ML_CHUNK)
    spb = SG_CHUNK // TS
    w_in_t = jnp.swapaxes(w_in, 1, 2)
    cache_krt = jnp.swapaxes(cache_mla_kr, 2, 3)
    w_mix_all = _bf(w_in_t[:, N_BRANCH * D:, :])
    ub_all = _bf(peer_u)
    vb_all = _bf(peer_v)
    rows_peer = _Rows(B, T, DB, TS, block_sizes=(512, 256, 128))

    new_p, new_s = [], []
    for l in range(depth):
        wl = w_mix_all[l]
        zr = lambda n: jnp.zeros((n, D), bf16)
        wA = jnp.concatenate([wl[0:1552], zr(112)], axis=0)
        wB = jnp.concatenate([wl[1552:2096], zr(96)], axis=0)
        wC = jnp.concatenate([wl[2096:3632], wl[3632:3636], zr(124), wl[3636:3640], zr(124)], axis=0)
        wD = wl[3640:4664]
        wa2p = jnp.concatenate([gla_wa2[l], jnp.zeros((128 - GLA_RANK, GLA_HEADS * GLA_DK), f32)], axis=0)
        uq = mla_w_uq[l].reshape(MLA_Q_RANK, MLA_HEADS, MLA_NOPE + MLA_ROPE)
        wqn = _bf(uq[:, :, :MLA_NOPE].reshape(MLA_Q_RANK, MLA_HEADS * MLA_NOPE))
        wqr = _bf(uq[:, :, MLA_NOPE:].reshape(MLA_Q_RANK, MLA_HEADS * MLA_ROPE))
        uk = jnp.transpose(mla_w_uk[l], (1, 2, 0))
        wa = jnp.einsum('hnr,hg->hngr', uk, eye_h)
        wa = _bf(jnp.pad(wa, ((0, 0), (0, 0), (0, 0), (0, 256 - MLA_KV_RANK))).reshape(MLA_HEADS * MLA_NOPE, MLA_HEADS * 256))
        wuv = _bf(jnp.einsum('rhv,hg->hrgv', mla_w_uv[l], eye_h).reshape(MLA_HEADS, MLA_KV_RANK, MLA_HEADS * MLA_V))
        pad128 = lambda a: jnp.pad(a, (0, 128 - a.shape[0])).reshape(1, 128)
        bcol_p = jnp.pad(sg_bs[l].T, ((0, 0), (0, 128 - SG_GROUPS)))
        ws_s = jnp.stack([jnp.kron(jnp.eye(spb, dtype=f32), sg_ws[l, g, :TS, :TS]) for g in range(SG_GROUPS)])
        bcol_s = jnp.pad(jnp.tile(sg_bs[l][:, :TS].T, (spb, 1)), ((0, 0), (0, 128 - SG_GROUPS)))
        wo_bf = _bf(w_o[l])
        wq = peer_wq[l]
        wqh = _bf(wq)
        wql = _bf(wq - wqh.astype(f32))
        half = PEER_DKEY // 2
        k1p = jnp.pad(peer_k1[l], ((0, 0), (0, 0), (0, half)))
        k2p = jnp.pad(peer_k2[l], ((0, 0), (0, 0), (half, 0)))

        if l == 0:
            h1 = _modnorm_call(rows, x, None, norm1[l], mod, l)
        else:
            x, h1 = _modnorm_call(rows, x, peer_out, norm1[l], mod, l)
        zA, zB, zC, zD = (_mm_call(h1, w) for w in (wA, wB, wC, wD))

        gn_a = gla_norm[l].reshape(1, -1)
        ba = gla_ba[l].reshape(1, -1)
        ya_p, S_p = _gla_call(zA, 0, B, T, Lp_gla, wa2p, ba, gn_a, None)
        ya_s, S_s = _gla_call(zA, Np, DB, TS, Ls, wa2p, ba, gn_a, state_gla[l])

        qcat, kcat = _mla_prep_call(rows, zB, cos_t, sin_t, mla_gq[l].reshape(1, -1), mla_gkv[l].reshape(1, -1),
                                    wqn, wqr, wa, place, rot)
        yb_p = _flash_call(qcat, kcat, wuv, B, T)
        q3 = qcat[Np:].reshape(DB, TS * MLA_HEADS, 256)
        kn3 = jnp.pad(kcat[Np:].reshape(DB, TS, 256), ((0, 0), (0, 8 - TS), (0, 0)))
        yb_s = _paged_call(page_table, q3, kn3, wuv, cache_mla_ckv, cache_krt, l).reshape(Ns, -1)

        bi, bfg, gn_c = pad128(ml_bi[l]), pad128(ml_bf[l]), ml_norm[l].reshape(1, -1)
        yc_p, C_p, n_p, m_p = _mlstm_call(zC, 0, B, T, Lp_ml, bi, bfg, gn_c, None)
        st = (state_mlstm_C[l], state_mlstm_n[l].reshape(DB, 1, -1),
              jnp.pad(state_mlstm_m[l], ((0, 0), (0, 128 - ML_HEADS))).reshape(DB, 1, 128))
        yc_s, C_s, n_s, m_s = _mlstm_call(zC, Np, DB, TS, Ls_ml, bi, bfg, gn_c, st)

        gv = sg_gv[l].reshape(1, -1)
        (yd_p,) = _sgu_call(zD, 0, Np, gv, sg_ws[l], bcol_p, False)
        yd_s, v_s = _sgu_call(zD, Np, Ns, gv, ws_s, bcol_s, True)

        ys = [jnp.concatenate([p_, s_], axis=0) for p_, s_ in ((ya_p, ya_s), (yb_p, yb_s), (yc_p, yc_s), (yd_p, yd_s))]
        merged = _merge_call(h1, ys, w_in_t, w_br, l)
        x, h2, h2b = _outproj_call(rows, x, merged, wo_bf, norm2[l], mod, l)

        ea, eb, g = _peer_score_call(rows, h2, wqh, wql, k1p, k2p)
        peer_out = _peer_mix_call(rows_peer, h2b, ea, eb, g, ub_all, vb_all, l)

        new_p.append(dict(ckv=kcat[:Np, :MLA_KV_RANK].reshape(B, T, -1),
                          kr=kcat[:Np, MLA_KV_RANK:MLA_KV_RANK + MLA_ROPE].reshape(B, T, -1),
                          S=S_p, C=C_p, n=n_p.reshape(B, ML_HEADS, ML_DK), m=m_p[:, 0, :ML_HEADS]))
        new_s.append(dict(ckv=kcat[Np:, :MLA_KV_RANK].reshape(DB, TS, -1),
                          kr=kcat[Np:, MLA_KV_RANK:MLA_KV_RANK + MLA_ROPE].reshape(DB, TS, -1),
                          S=S_s, C=C_s, n=n_s.reshape(DB, ML_HEADS, ML_DK), m=m_s[:, 0, :ML_HEADS],
                          v=v_s.reshape(DB, TS, -1)))

    y = _final_call(rows, x, peer_out, final_norm, mod, depth - 1)
    stk = lambda outs, name: jnp.stack([o[name] for o in outs], axis=0)
    return (y[:Np].reshape(B, T, D), y[Np:].reshape(DB, TS, D),
            stk(new_p, 'ckv'), stk(new_p, 'kr'), stk(new_p, 'S'), stk(new_p, 'C'), stk(new_p, 'n'), stk(new_p, 'm'),
            stk(new_s, 'ckv'), stk(new_s, 'kr'), stk(new_s, 'S'), stk(new_s, 'C'), stk(new_s, 'n'), stk(new_s, 'm'),
            stk(new_s, 'v'))
```

```python
import functools
import math

import numpy as np
import jax
import jax.numpy as jnp
from jax import lax
from jax.experimental import pallas as pl
from jax.experimental.pallas import tpu as pltpu

f32 = jnp.float32
bf16 = jnp.bfloat16

D_MODEL = 2048
EPS = 1e-6
ADA_CHUNKS = 6
GLA_HEADS, GLA_DK, GLA_DV, GLA_RANK, GLA_TAU, GLA_CHUNK = 4, 64, 128, 16, 16.0, 16
MLA_HEADS, MLA_Q_RANK, MLA_KV_RANK, MLA_NOPE, MLA_ROPE, MLA_V = 8, 384, 128, 64, 32, 64
MLA_SCALE = (MLA_NOPE + MLA_ROPE) ** -0.5
ROPE_THETA = 10000.0
ML_HEADS, ML_DK, ML_DV, ML_CHUNK = 4, 64, 128, 64
SG_GROUPS, SG_CHUNK, SG_WIDTH = 4, 128, 512
PEER_HEADS, PEER_NKEYS, PEER_DKEY, PEER_TOPK = 8, 128, 128, 16
PEER_EXPERTS = PEER_NKEYS * PEER_NKEYS
BRANCH_WIDTH = 512
N_BRANCH = 4

W_A = 1664
W_B = 640
W_C = 1792
W_D = 1024

NEG = -1e30
VMEM_LIMIT_V7X = 56 * 1024 * 1024
PEER_PITCH_PAD = 8
PEER_TOKEN_UNROLL = 32

def _cparams(*sem):
    return pltpu.CompilerParams(dimension_semantics=sem, vmem_limit_bytes=VMEM_LIMIT_V7X)


def _bf(x):
    return x.astype(bf16)


def _dot(a, b):
    return jnp.dot(a, b, preferred_element_type=f32)


def _dot_nt(a, b):
    return lax.dot_general(a, b, (((1,), (1,)), ((), ())), preferred_element_type=f32)


def _split2(x):
    hi = _bf(x)
    lo = _bf(x - hi.astype(f32))
    return hi, lo


def _dot3(a, b):
    ah, al = _split2(a)
    bh, bl = _split2(b)
    return _dot(ah, bh) + (_dot(ah, bl) + _dot(al, bh))


def _dot2b(a, b_bf):
    ah, al = _split2(a)
    return _dot(ah, b_bf) + _dot(al, b_bf)


def _sigmoid(x):
    return 1.0 / (1.0 + jnp.exp(-x))


def _log_sigmoid(x):
    return jnp.minimum(x, 0.0) - jnp.log1p(jnp.exp(-jnp.abs(x)))


def _gelu(x):
    return jax.nn.gelu(x)


def _rms(x, g):
    return x * lax.rsqrt(jnp.mean(x * x, axis=-1, keepdims=True) + EPS) * g


def _iota(shape, dim):
    return lax.broadcasted_iota(jnp.int32, shape, dim)


def _seg_cumsum(x, rowmod, L):
    s = 1
    while s < L:
        x = x + jnp.where(rowmod >= s, pltpu.roll(x, s, 0), 0.0)
        s *= 2
    return x


def _seg_last(x, rowmod, L):
    R = x.shape[0]
    s = 1
    while s < L:
        x = jnp.where(rowmod + s < L, pltpu.roll(x, R - s, 0), x)
        s *= 2
    return x


def _expand_heads(x, n_heads, width):
    R = x.shape[0]
    lane_head = _iota((R, n_heads * width), 1) // width
    out = jnp.zeros((R, n_heads * width), f32)
    for h in range(n_heads):
        out = jnp.where(lane_head == h, x[:, h:h + 1], out)
    return out


def _pick_block(n, cands):
    for c in cands:
        if n % c == 0:
            return c
    raise ValueError(f"no block size in {cands} divides {n}")


def _ada_kernel(c_ref, w_ref, b_ref, o_ref):
    c = c_ref[...]
    a = _bf(c * _sigmoid(c))
    o_ref[...] = _dot(a, _bf(w_ref[...])) + b_ref[...]


def _ada_call(c_rows, w_ada, b_ada):
    L, D, N6 = w_ada.shape
    Rc = c_rows.shape[0]
    tn = 1024
    return pl.pallas_call(
        _ada_kernel, name="ada_mod",
        grid=(L, N6 // tn),
        in_specs=[pl.BlockSpec((Rc, D), lambda l, j: (0, 0)),
                  pl.BlockSpec((None, D, tn), lambda l, j: (l, 0, j)),
                  pl.BlockSpec((None, 1, tn), lambda l, j: (l, 0, j))],
        out_specs=pl.BlockSpec((None, Rc, tn), lambda l, j: (l, 0, j)),
        out_shape=jax.ShapeDtypeStruct((L, Rc, N6), f32),
        compiler_params=_cparams("arbitrary", "arbitrary"),
    )(c_rows, w_ada, b_ada.reshape(L, 1, N6))


class _Rows:
    def __init__(self, B, T, DB, TS, block_sizes=(256, 128)):
        self.B, self.T, self.DB, self.TS = B, T, DB, TS
        self.Np, self.Ns = B * T, DB * TS
        self.N = self.Np + self.Ns
        self.R = _pick_block(math.gcd(T, self.Ns), block_sizes)
        self.nPb = self.Np // self.R
        self.nSb = self.Ns // self.R
        self.nb = self.nPb + self.nSb
        self.blocks_per_seq = T // self.R

    def mod_specs(self, l, chunk):
        R, nPb, nSb, Ns = self.R, self.nPb, self.nSb, self.Ns
        s = pl.BlockSpec((None, R, D_MODEL), lambda i: (l, jnp.clip(i - nPb, 0, nSb - 1), chunk))
        p = pl.BlockSpec((None, 8, D_MODEL), lambda i: (l, Ns // 8, chunk))
        return s, p

    def pick(self, i, s_ref, p_ref):
        b = jnp.minimum(i // self.blocks_per_seq, self.B - 1)
        return jnp.where(i >= self.nPb, s_ref[...], p_ref[pl.ds(b, 1), :])


def _modnorm_kernel(*refs, rows, with_peer):
    i = pl.program_id(0)
    if with_peer:
        x_ref, p_ref, g2s, g2p, g_ref, shs_ref, shp_ref, scs_ref, scp_ref, xo_ref, h_ref = refs
        x = x_ref[...] + rows.pick(i, g2s, g2p) * p_ref[...]
        xo_ref[...] = x
    else:
        x_ref, g_ref, shs_ref, shp_ref, scs_ref, scp_ref, h_ref = refs
        x = x_ref[...]
    sh = rows.pick(i, shs_ref, shp_ref)
    sc = rows.pick(i, scs_ref, scp_ref)
    h_ref[...] = _bf(_rms(x, g_ref[...]) * (1.0 + sc) + sh)


def _modnorm_call(rows, x, peer_out, gain, mod, l):
    R = rows.R
    shs, shp = rows.mod_specs(l, 0)
    scs, scp = rows.mod_specs(l, 1)
    row = pl.BlockSpec((R, D_MODEL), lambda i: (i, 0))
    vec = pl.BlockSpec((1, D_MODEL), lambda i: (0, 0))
    h_shape = jax.ShapeDtypeStruct((rows.N, D_MODEL), bf16)
    if peer_out is None:
        in_specs = [row, vec, shs, shp, scs, scp]
        args = (x, gain.reshape(1, D_MODEL), mod, mod, mod, mod)
        out_specs, out_shape = row, h_shape
    else:
        g2s, g2p = rows.mod_specs(l - 1, 5)
        in_specs = [row, row, g2s, g2p, vec, shs, shp, scs, scp]
        args = (x, peer_out, mod, mod, gain.reshape(1, D_MODEL), mod, mod, mod, mod)
        out_specs = [row, row]
        out_shape = [jax.ShapeDtypeStruct((rows.N, D_MODEL), f32), h_shape]
    return pl.pallas_call(
        functools.partial(_modnorm_kernel, rows=rows, with_peer=peer_out is not None), name="modnorm",
        grid=(rows.nb,), in_specs=in_specs, out_specs=out_specs, out_shape=out_shape,
        compiler_params=_cparams("arbitrary"),
    )(*args)


def _mm_kernel(x_ref, wt_ref, o_ref):
    o_ref[...] = _dot_nt(x_ref[...], wt_ref[...]).astype(o_ref.dtype)


def _mm_call(x, wt, out_dtype=f32):
    M, K = x.shape
    N = wt.shape[0]
    tm = _pick_block(M, (1088, 1024, 640, 512, 256, 128))
    return pl.pallas_call(
        _mm_kernel, name="mixer_proj",
        grid=(M // tm,),
        in_specs=[pl.BlockSpec((tm, K), lambda i: (i, 0)),
                  pl.BlockSpec((N, K), lambda i: (0, 0))],
        out_specs=pl.BlockSpec((tm, N), lambda i: (i, 0)),
        out_shape=jax.ShapeDtypeStruct((M, N), out_dtype),
        compiler_params=_cparams("arbitrary"),
    )(x, wt)


def _gla_kernel(*refs, R, L, carry):
    if carry:
        z_ref, wa2_ref, ba_ref, gn_ref, y_ref, so_ref, s_scr = refs
    else:
        z_ref, wa2_ref, ba_ref, gn_ref, s0_ref, y_ref, so_ref = refs
    H, DK, DV = GLA_HEADS, GLA_DK, GLA_DV
    HK, HV = H * DK, H * DV
    nchunk = R // L
    z = z_ref[...]
    q = z[:, 0:HK] * (DK ** -0.5)
    k = z[:, HK:2 * HK]
    v = z[:, 2 * HK:2 * HK + HV]
    r = z[:, 2 * HK + HV:2 * HK + 2 * HV]
    ag = z[:, 2 * HK + 2 * HV:2 * HK + 2 * HV + 128]
    la = _log_sigmoid(_dot3(ag, wa2_ref[...]) + ba_ref[...]) * (1.0 / GLA_TAU)
    rowmod = _iota((R, HK), 0) % L
    b = _seg_cumsum(la, rowmod, L)
    tot = _seg_last(b, rowmod, L)

    sel_row_head = _iota((HK, 128), 0) // DK
    sel_lane = _iota((HK, 128), 1)
    att = jnp.zeros((R, 128), f32)
    for d in range(L):
        ks = k if d == 0 else pltpu.roll(k, d, 0)
        bs = b if d == 0 else pltpu.roll(b, d, 0)
        p = q * ks * jnp.exp(jnp.where(rowmod >= d, b - bs, NEG))
        att = att + _dot2b(p, _bf(sel_lane == d * H + sel_row_head))
    ex_row = _iota((128, HV), 0)
    ex_lane_head = _iota((128, HV), 1) // DV
    o = jnp.zeros((R, HV), f32)
    for d in range(L):
        vs = v if d == 0 else pltpu.roll(v, d, 0)
        o = o + _dot2b(att, _bf(ex_row == d * H + ex_lane_head)) * vs

    qt = _bf(q * jnp.exp(b))
    kd = k * jnp.exp(tot - b)
    kdT = _bf(kd.T)
    totT = tot.T
    vb = _bf(v)
    blockdiag = (_iota((HK, HV), 0) // DK) == (_iota((HK, HV), 1) // DV)
    row_chunk = _iota((R, HK), 0) // L
    col_chunk = _iota((HK, R), 1) // L
    if carry:
        @pl.when(pl.program_id(1) == 0)
        def _():
            s_scr[...] = jnp.zeros((HK, HV), f32)
        S = s_scr[...]
    for c in range(nchunk):
        if not carry:
            rows_ = [jnp.concatenate([s0_ref[c, h] if g == h else jnp.zeros((DK, DV), f32) for g in range(H)], axis=1)
                     for h in range(H)]
            S = jnp.concatenate(rows_, axis=0)
        o = o + _dot(jnp.where(row_chunk == c, qt, jnp.zeros_like(qt)), _bf(S))
        U = _dot(jnp.where(col_chunk == c, kdT, jnp.zeros_like(kdT)), vb)
        dcol = jnp.exp(totT[:, c * L:c * L + 1])
        S = dcol * S + jnp.where(blockdiag, U, 0.0)
        if not carry:
            for h in range(H):
                so_ref[c, h] = S[h * DK:(h + 1) * DK, h * DV:(h + 1) * DV]
    if carry:
        s_scr[...] = S

        @pl.when(pl.program_id(1) == pl.num_programs(1) - 1)
        def _():
            for h in range(H):
                so_ref[h] = S[h * DK:(h + 1) * DK, h * DV:(h + 1) * DV]

    gn = gn_ref[...]
    outs = []
    for h in range(H):
        oh = o[:, h * DV:(h + 1) * DV]
        outs.append(_rms(oh, gn[:, h * DV:(h + 1) * DV]))
    y_ref[...] = _bf(jnp.concatenate(outs, axis=1) * (r * _sigmoid(r)))


def _gla_call(zA, row0, B, T, L, wa2p, ba, gn, s0):
    H, DK, DV = GLA_HEADS, GLA_DK, GLA_DV
    carry = s0 is None
    if carry:
        R = _pick_block(T, (256, 128))
        nt = T // R
        grid = (B, nt)
        rb0 = row0 // R
        in_specs = [pl.BlockSpec((R, W_A), lambda b, t: (rb0 + b * nt + t, 0)),
                    pl.BlockSpec((128, H * DK), lambda b, t: (0, 0)),
                    pl.BlockSpec((1, H * DK), lambda b, t: (0, 0)),
                    pl.BlockSpec((1, H * DV), lambda b, t: (0, 0))]
        out_specs = [pl.BlockSpec((R, H * DV), lambda b, t: (b * nt + t, 0)),
                     pl.BlockSpec((None, H, DK, DV), lambda b, t: (b, 0, 0, 0))]
        scratch = [pltpu.VMEM((H * DK, H * DV), f32)]
        args = (zA, wa2p, ba, gn)
        sem = ("arbitrary", "arbitrary")
    else:
        R = 128
        nb = (B * T) // R
        cpb = R // L
        grid = (nb,)
        rb0 = row0 // R
        in_specs = [pl.BlockSpec((R, W_A), lambda i: (rb0 + i, 0)),
                    pl.BlockSpec((128, H * DK), lambda i: (0, 0)),
                    pl.BlockSpec((1, H * DK), lambda i: (0, 0)),
                    pl.BlockSpec((1, H * DV), lambda i: (0, 0)),
                    pl.BlockSpec((cpb, H, DK, DV), lambda i: (i, 0, 0, 0))]
        out_specs = [pl.BlockSpec((R, H * DV), lambda i: (i, 0)),
                     pl.BlockSpec((cpb, H, DK, DV), lambda i: (i, 0, 0, 0))]
        scratch = []
        args = (zA, wa2p, ba, gn, s0)
        sem = ("arbitrary",)
    return pl.pallas_call(
        functools.partial(_gla_kernel, R=R, L=L, carry=carry), name="gla_prompt" if carry else "gla_sample",
        grid=grid, in_specs=in_specs, out_specs=out_specs,
        out_shape=[jax.ShapeDtypeStruct((B * T, H * DV), bf16),
                   jax.ShapeDtypeStruct((B, H, DK, DV), f32)],
        scratch_shapes=scratch,
        compiler_params=_cparams(*sem),
    )(*args)


def _mlstm_kernel(*refs, R, L, carry):
    if carry:
        z_ref, bi_ref, bf_ref, gn_ref, y_ref, co_ref, no_ref, mo_ref, c_scr, n_scr, m_scr = refs
    else:
        z_ref, bi_ref, bf_ref, gn_ref, c0_ref, n0_ref, m0_ref, y_ref, co_ref, no_ref, mo_ref = refs
    H, DK, DV = ML_HEADS, ML_DK, ML_DV
    HK, HV = H * DK, H * DV
    nchunk = R // L
    z = z_ref[...]
    q = z[:, 0:HK]
    k = z[:, HK:2 * HK] * (DK ** -0.5)
    v = z[:, 2 * HK:2 * HK + HV]
    og = z[:, 2 * HK + HV:2 * HK + 2 * HV]
    ig = z[:, 2 * HK + 2 * HV:2 * HK + 2 * HV + 128] + bi_ref[...]
    fg = z[:, 2 * HK + 2 * HV + 128:2 * HK + 2 * HV + 256] + bf_ref[...]
    rowmod = _iota((R, 128), 0) % L
    row_chunk128 = _iota((R, 128), 0) // L
    F = _seg_cumsum(_log_sigmoid(fg), rowmod, L)
    FT = F.T
    IT = ig.T

    if carry:
        @pl.when(pl.program_id(1) == 0)
        def _():
            c_scr[...] = jnp.zeros((HK, HV), f32)
            n_scr[...] = jnp.zeros((1, HK), f32)
            m_scr[...] = jnp.zeros((1, 128), f32)

    tpos = _iota((R, R), 0)
    spos = _iota((R, R), 1)
    causal = (tpos // L == spos // L) & (spos <= tpos)
    qb = _bf(q)
    kb = _bf(k)
    vb = _bf(v)
    lane_head_k = _iota((R, HK), 1) // DK

    Dms, qks, MX = [], [], jnp.zeros((R, 128), f32)
    lane128 = _iota((R, 128), 1)
    for h in range(H):
        Dm = jnp.where(causal, F[:, h:h + 1] - FT[h:h + 1, :] + IT[h:h + 1, :], NEG)
        Dms.append(Dm)
        MX = jnp.where(lane128 == h, jnp.max(Dm, axis=1, keepdims=True), MX)
        qks.append(_dot_nt(jnp.where(lane_head_k == h, qb, jnp.zeros_like(qb)), kb))

    if carry:
        Mprev = jnp.zeros((R, 128), f32)
        m_run = m_scr[...]
        for c in range(nchunk):
            Mprev = jnp.where(row_chunk128 == c, m_run, Mprev)
            last = c * L + L - 1
            m_run = jnp.maximum(F[last:last + 1, :] + m_run, MX[last:last + 1, :])
    else:
        Mprev = jnp.zeros((R, 128), f32)
        for c in range(nchunk):
            Mprev = jnp.where(row_chunk128 == c, m0_ref[c], Mprev)
    Mt = jnp.maximum(F + Mprev, MX)
    Mnew = _seg_last(Mt, rowmod, L)
    Fend = _seg_last(F, rowmod, L)
    Wprev = jnp.exp(F + Mprev - Mt)
    Wsrc = jnp.exp(Fend - F + ig - Mnew)
    Wold = jnp.exp(Fend + Mprev - Mnew)
    Emt = jnp.exp(-Mt)

    kw = k * _expand_heads(Wsrc, H, DK)
    kwT = _bf(kw.T)
    Wold_k = _expand_heads(Wold, H, DK)
    Wold_v = _expand_heads(Wold, H, DV)
    blockdiag = (_iota((HK, HV), 0) // DK) == (_iota((HK, HV), 1) // DV)
    row_chunk = _iota((R, HK), 0) // L
    col_chunk = _iota((HK, R), 1) // L

    qC = jnp.zeros((R, HV), f32)
    Nrows = jnp.zeros((R, HK), f32)
    if carry:
        C = c_scr[...]
        n = n_scr[...]
    for c in range(nchunk):
        if not carry:
            rows_ = [jnp.concatenate([c0_ref[c, h] if g == h else jnp.zeros((DK, DV), f32) for g in range(H)], axis=1)
                     for h in range(H)]
            C = jnp.concatenate(rows_, axis=0)
            n = n0_ref[c]
        in_chunk = row_chunk == c
        qC = qC + _dot(jnp.where(in_chunk, qb, jnp.zeros_like(qb)), _bf(C))
        Nrows = jnp.where(in_chunk, n, Nrows)
        last = c * L + L - 1
        U = _dot(jnp.where(col_chunk == c, kwT, jnp.zeros_like(kwT)), vb)
        C = Wold_v[last:last + 1, :] * C + jnp.where(blockdiag, U, 0.0)
        n = Wold_k[last:last + 1, :] * n + jnp.sum(jnp.where(in_chunk, kw, 0.0), axis=0, keepdims=True)
        if not carry:
            for h in range(H):
                co_ref[c, h] = C[h * DK:(h + 1) * DK, h * DV:(h + 1) * DV]
            no_ref[c] = n
            mo_ref[c] = Mnew[last:last + 1, :]
    if carry:
        c_scr[...] = C
        n_scr[...] = n
        m_scr[...] = Mnew[R - 1:R, :]

        @pl.when(pl.program_id(1) == pl.num_programs(1) - 1)
        def _():
            for h in range(H):
                co_ref[h] = C[h * DK:(h + 1) * DK, h * DV:(h + 1) * DV]
            no_ref[...] = n
            mo_ref[...] = Mnew[R - 1:R, :]

    qn = _dot2b(q * Nrows, _bf(_iota((HK, 128), 1) == _iota((HK, 128), 0) // DK))
    gn = gn_ref[...]
    outs = []
    for h in range(H):
        A = jnp.exp(Dms[h] - Mt[:, h:h + 1]) * qks[h]
        num = Wprev[:, h:h + 1] * qC[:, h * DV:(h + 1) * DV] + _dot(_bf(A), vb[:, h * DV:(h + 1) * DV])
        nq = Wprev[:, h:h + 1] * qn[:, h:h + 1] + jnp.sum(A, axis=1, keepdims=True)
        hh = num / jnp.maximum(jnp.abs(nq), Emt[:, h:h + 1])
        outs.append(_rms(hh, gn[:, h * DV:(h + 1) * DV]))
    y_ref[...] = _bf(_sigmoid(og) * jnp.concatenate(outs, axis=1))


def _mlstm_call(zC, row0, B, T, L, bi, bfg, gn, st):
    H, DK, DV = ML_HEADS, ML_DK, ML_DV
    carry = st is None
    R = 128
    out_shape = [jax.ShapeDtypeStruct((B * T, H * DV), bf16),
                 jax.ShapeDtypeStruct((B, H, DK, DV), f32),
                 jax.ShapeDtypeStruct((B, 1, H * DK), f32),
                 jax.ShapeDtypeStruct((B, 1, 128), f32)]
    rb0 = row0 // R
    if carry:
        nt = T // R
        grid = (B, nt)
        c2 = lambda b, t: (0, 0)
        in_specs = [pl.BlockSpec((R, W_C), lambda b, t: (rb0 + b * nt + t, 0)),
                    pl.BlockSpec((1, 128), c2), pl.BlockSpec((1, 128), c2), pl.BlockSpec((1, H * DV), c2)]
        out_specs = [pl.BlockSpec((R, H * DV), lambda b, t: (b * nt + t, 0)),
                     pl.BlockSpec((None, H, DK, DV), lambda b, t: (b, 0, 0, 0)),
                     pl.BlockSpec((None, 1, H * DK), lambda b, t: (b, 0, 0)),
                     pl.BlockSpec((None, 1, 128), lambda b, t: (b, 0, 0))]
        scratch = [pltpu.VMEM((H * DK, H * DV), f32), pltpu.VMEM((1, H * DK), f32), pltpu.VMEM((1, 128), f32)]
        args = (zC, bi, bfg, gn)
        sem = ("arbitrary", "arbitrary")
    else:
        c0, n0, m0 = st
        nb = (B * T) // R
        cpb = R // L
        grid = (nb,)
        c2 = lambda i: (0, 0)
        in_specs = [pl.BlockSpec((R, W_C), lambda i: (rb0 + i, 0)),
                    pl.BlockSpec((1, 128), c2), pl.BlockSpec((1, 128), c2), pl.BlockSpec((1, H * DV), c2),
                    pl.BlockSpec((cpb, H, DK, DV), lambda i: (i, 0, 0, 0)),
                    pl.BlockSpec((cpb, 1, H * DK), lambda i: (i, 0, 0)),
                    pl.BlockSpec((cpb, 1, 128), lambda i: (i, 0, 0))]
        out_specs = [pl.BlockSpec((R, H * DV), lambda i: (i, 0)),
                     pl.BlockSpec((cpb, H, DK, DV), lambda i: (i, 0, 0, 0)),
                     pl.BlockSpec((cpb, 1, H * DK), lambda i: (i, 0, 0)),
                     pl.BlockSpec((cpb, 1, 128), lambda i: (i, 0, 0))]
        scratch = []
        args = (zC, bi, bfg, gn, c0, n0, m0)
        sem = ("arbitrary",)
    return pl.pallas_call(
        functools.partial(_mlstm_kernel, R=R, L=L, carry=carry), name="mlstm_prompt" if carry else "mlstm_sample",
        grid=grid, in_specs=in_specs, out_specs=out_specs, out_shape=out_shape,
        scratch_shapes=scratch, compiler_params=_cparams(*sem),
    )(*args)


def _sgu_kernel(z_ref, gv_ref, ws_ref, bcol_ref, y_ref, *v_out, nch):
    G = SG_GROUPS
    W = SG_WIDTH
    cw = W // G
    tril = _iota((SG_CHUNK, SG_CHUNK), 1) <= _iota((SG_CHUNK, SG_CHUNK), 0)
    for c in range(nch):
        rs = slice(c * SG_CHUNK, (c + 1) * SG_CHUNK)
        u = z_ref[rs, 0:W]
        vn = _rms(_gelu(z_ref[rs, W:2 * W]), gv_ref[...])
        if v_out:
            v_out[0][rs, :] = vn
        vb = _bf(vn)
        bcol = bcol_ref[...]
        outs = []
        for g in range(G):
            wc = _bf(jnp.where(tril, ws_ref[g], 0.0))
            outs.append(_dot(wc, vb[:, g * cw:(g + 1) * cw]) + bcol[:, g:g + 1])
        y_ref[rs, :] = _bf(_gelu(u) * jnp.concatenate(outs, axis=1))


def _sgu_call(zD, row0, n_rows, gv, ws, bcol, want_v):
    nch = _pick_block(n_rows // SG_CHUNK, (4, 2, 1))
    Rb = nch * SG_CHUNK
    rb0 = row0 // Rb
    out_shape = [jax.ShapeDtypeStruct((n_rows, SG_WIDTH), bf16)]
    out_specs = [pl.BlockSpec((Rb, SG_WIDTH), lambda i: (i, 0))]
    if want_v:
        out_shape.append(jax.ShapeDtypeStruct((n_rows, SG_WIDTH), f32))
        out_specs.append(pl.BlockSpec((Rb, SG_WIDTH), lambda i: (i, 0)))
    return pl.pallas_call(
        functools.partial(_sgu_kernel, nch=nch), name="sgu",
        grid=(n_rows // Rb,),
        in_specs=[pl.BlockSpec((Rb, W_D), lambda i: (rb0 + i, 0)),
                  pl.BlockSpec((1, SG_WIDTH), lambda i: (0, 0)),
                  pl.BlockSpec((SG_GROUPS, SG_CHUNK, SG_CHUNK), lambda i: (0, 0, 0)),
                  pl.BlockSpec((SG_CHUNK, 128), lambda i: (0, 0))],
        out_specs=out_specs, out_shape=out_shape,
        compiler_params=_cparams("arbitrary"),
    )(zD, gv, ws, bcol)


def _mla_prep_kernel(z_ref, cos_ref, sin_ref, gq_ref, gkv_ref, wqn_ref, wqr_ref, wa_ref, pb_ref, rot_ref,
                     q_ref, k_ref):
    z = z_ref[...]
    dq = z[:, 0:MLA_Q_RANK]
    dkv = z[:, MLA_Q_RANK:MLA_Q_RANK + MLA_KV_RANK]
    kr = z[:, MLA_Q_RANK + MLA_KV_RANK:MLA_Q_RANK + MLA_KV_RANK + 128]
    cos = cos_ref[...]
    sin = sin_ref[...]
    rot = rot_ref[...]
    dqn = _bf(_rms(dq, gq_ref[...]))
    qn = _dot(dqn, wqn_ref[...])
    qr = _dot(dqn, wqr_ref[...])
    qrope = qr * cos + _dot2b(qr, rot) * sin
    q_ref[...] = _bf(_dot(_bf(qn), wa_ref[...]) + _dot(_bf(qrope), pb_ref[...]))
    ckv = _rms(dkv, gkv_ref[...])
    krope = kr * cos[:, 0:128] + _dot2b(kr, rot[0:128, 0:128]) * sin[:, 0:128]
    k_ref[...] = jnp.concatenate([ckv, krope], axis=1)


def _mla_prep_call(rows, zB, cos, sin, gq, gkv, wqn, wqr, wa, pb, rot):
    R = rows.R
    HC = MLA_HEADS * 256
    c2 = lambda i: (0, 0)
    return pl.pallas_call(
        _mla_prep_kernel, name="mla_prep",
        grid=(rows.nb,),
        in_specs=[pl.BlockSpec((R, W_B), lambda i: (i, 0)),
                  pl.BlockSpec((R, 256), lambda i: (i, 0)),
                  pl.BlockSpec((R, 256), lambda i: (i, 0)),
                  pl.BlockSpec((1, MLA_Q_RANK), c2), pl.BlockSpec((1, MLA_KV_RANK), c2),
                  pl.BlockSpec((MLA_Q_RANK, MLA_HEADS * MLA_NOPE), c2),
                  pl.BlockSpec((MLA_Q_RANK, MLA_HEADS * MLA_ROPE), c2),
                  pl.BlockSpec((MLA_HEADS * MLA_NOPE, HC), c2),
                  pl.BlockSpec((MLA_HEADS * MLA_ROPE, HC), c2),
                  pl.BlockSpec((256, 256), c2)],
        out_specs=[pl.BlockSpec((R, HC), lambda i: (i, 0)), pl.BlockSpec((R, 256), lambda i: (i, 0))],
        out_shape=[jax.ShapeDtypeStruct((rows.N, HC), bf16), jax.ShapeDtypeStruct((rows.N, 256), f32)],
        compiler_params=_cparams("arbitrary"),
    )(zB, cos, sin, gq, gkv, wqn, wqr, wa, pb, rot)


def _flash_kernel(qtab_ref, ktab_ref, q_ref, k_ref, wuv_ref, y_ref, qs_scr, m_scr, l_scr, acc_scr, *, tq, tk):
    qi = qtab_ref[pl.program_id(1)]
    ki = ktab_ref[pl.program_id(1)]
    H = MLA_HEADS
    NR = H * tq
    diag = (qi * tq + tq - 1) // tk

    @pl.when(ki == 0)
    def _():
        for h in range(H):
            qs_scr[h * tq:(h + 1) * tq, :] = q_ref[:, h * 256:(h + 1) * 256]
        m_scr[...] = jnp.full(m_scr.shape, NEG, f32)
        l_scr[...] = jnp.zeros(l_scr.shape, f32)
        acc_scr[...] = jnp.zeros(acc_scr.shape, f32)

    def step(masked):
        kb = _bf(k_ref[...])
        s = _dot_nt(qs_scr[...], kb) * MLA_SCALE
        if masked:
            qpos = qi * tq + _iota((NR, tk), 0) % tq
            s = jnp.where(ki * tk + _iota((NR, tk), 1) <= qpos, s, NEG)
        m_prev = m_scr[...]
        m_next = jnp.maximum(m_prev, jnp.max(s, axis=1, keepdims=True))
        alpha = jnp.exp(m_prev - m_next)
        p = jnp.exp(s - jnp.concatenate([m_next] * (tk // 128), axis=1))
        l_scr[...] = alpha * l_scr[...] + jnp.sum(p, axis=1, keepdims=True)
        acc_scr[...] = alpha * acc_scr[...] + _dot(_bf(p), kb[:, 0:MLA_KV_RANK])
        m_scr[...] = m_next

    pl.when(ki < diag)(functools.partial(step, False))
    pl.when(ki == diag)(functools.partial(step, True))

    @pl.when(ki == diag)
    def _():
        o = _bf(acc_scr[...] / l_scr[...])
        y = jnp.zeros((tq, MLA_HEADS * MLA_V), f32)
        for h in range(H):
            y = y + _dot(o[h * tq:(h + 1) * tq, :], wuv_ref[h])
        y_ref[...] = _bf(y)


def _flash_call(qcat, kcat, wuv, B, T):
    tq = 128
    tk = _pick_block(T, (512, 256, 128))
    nq, nk = T // tq, T // tk
    H = MLA_HEADS
    pairs = [(i, j) for i in range(nq) for j in range((i * tq + tq - 1) // tk + 1)]
    qtab = jnp.asarray([p[0] for p in pairs], jnp.int32)
    ktab = jnp.asarray([p[1] for p in pairs], jnp.int32)
    grid_spec = pltpu.PrefetchScalarGridSpec(
        num_scalar_prefetch=2,
        grid=(B, len(pairs)),
        in_specs=[pl.BlockSpec((tq, H * 256), lambda b, p, qt, kt: (b * nq + qt[p], 0)),
                  pl.BlockSpec((tk, 256), lambda b, p, qt, kt: (b * nk + kt[p], 0)),
                  pl.BlockSpec((H, MLA_KV_RANK, H * MLA_V), lambda b, p, qt, kt: (0, 0, 0))],
        out_specs=pl.BlockSpec((tq, H * MLA_V), lambda b, p, qt, kt: (b * nq + qt[p], 0)),
        scratch_shapes=[pltpu.VMEM((H * tq, 256), bf16), pltpu.VMEM((H * tq, 128), f32),
                        pltpu.VMEM((H * tq, 128), f32), pltpu.VMEM((H * tq, MLA_KV_RANK), f32)],
    )
    return pl.pallas_call(
        functools.partial(_flash_kernel, tq=tq, tk=tk), name="mla_flash",
        grid_spec=grid_spec,
        out_shape=jax.ShapeDtypeStruct((B * T, H * MLA_V), bf16),
        compiler_params=_cparams("arbitrary", "arbitrary"),
    )(qtab, ktab, qcat, kcat, wuv)


def _paged_kernel(pt_ref, q_ref, kn_ref, wuv_ref, ck_hbm, krt_hbm, y_ref,
                  ck_buf, krt_buf, kb_scr, sem, *, l, n_pages, page, TS, chunk):
    b = pl.program_id(0)
    nb = pl.num_programs(0)
    H = MLA_HEADS
    NR = TS * H
    n_keys = n_pages * page
    slot = b % 2

    def page_copies(bb, sl, p):
        pg = pt_ref[bb * n_pages + p]
        keys_ = pl.ds(p * page, page)
        return (pltpu.make_async_copy(ck_hbm.at[l, pg], ck_buf.at[sl, keys_, :], sem.at[0, sl]),
                pltpu.make_async_copy(krt_hbm.at[l, pg], krt_buf.at[sl, :, keys_], sem.at[1, sl]))

    def start_all(bb, sl):
        for p in range(n_pages):
            for cp in page_copies(bb, sl, p):
                cp.start()

    def wait_all(bb, sl):
        for p in range(n_pages):
            for cp in page_copies(bb, sl, p):
                cp.wait()

    @pl.when(b == 0)
    def _():
        start_all(0, 0)

    @pl.when(b + 1 < nb)
    def _():
        start_all(b + 1, 1 - slot)

    wait_all(b, slot)

    q = q_ref[...]
    ql = q[:, 0:MLA_KV_RANK]
    qr = q[:, MLA_KV_RANK:MLA_KV_RANK + MLA_ROPE]
    kn = _bf(kn_ref[...])
    s_new = _dot_nt(q, kn) * MLA_SCALE
    ok = (_iota((NR, 8), 1) <= _iota((NR, 8), 0) // H) & (_iota((NR, 8), 1) < TS)
    s_new = jnp.where(ok, s_new, NEG)
    m = jnp.max(s_new, axis=1, keepdims=True)
    ss = []
    for c in range(n_keys // chunk):
        rows_ = pl.ds(c * chunk, chunk)
        kb_scr[rows_, :] = _bf(ck_buf[slot, rows_, :])
        s = (_dot_nt(ql, kb_scr[rows_, :]) + _dot(qr, _bf(krt_buf[slot, :, rows_]))) * MLA_SCALE
        m = jnp.maximum(m, jnp.max(s, axis=1, keepdims=True))
        ss.append(s)
    e = jnp.exp(s_new - m)
    lsum = jnp.sum(e, axis=1, keepdims=True)
    acc = _dot(_bf(e), kn[:, 0:MLA_KV_RANK])
    for c in range(n_keys // chunk):
        e = jnp.exp(ss[c] - m)
        lsum = lsum + jnp.sum(e, axis=1, keepdims=True)
        acc = acc + _dot(_bf(e), kb_scr[pl.ds(c * chunk, chunk), :])
    o = acc / lsum
    row_head = _iota((NR, MLA_KV_RANK), 0) % H
    y = jnp.zeros((NR, H * MLA_V), f32)
    for h in range(H):
        y = y + _dot(_bf(jnp.where(row_head == h, o, 0.0)), wuv_ref[h])
    y_ref[...] = _bf(jnp.sum(y.reshape(TS, H, H * MLA_V), axis=1))


def _paged_call(page_table, q3, kn3, wuv, cache_ckv, cache_krt, l):
    DB, n_pages = page_table.shape
    TS = q3.shape[1] // MLA_HEADS
    H = MLA_HEADS
    page = cache_ckv.shape[2]
    n_keys = n_pages * page
    chunk = _pick_block(n_keys, (2048, 1024, 512, 256, 128))
    grid_spec = pltpu.PrefetchScalarGridSpec(
        num_scalar_prefetch=1,
        grid=(DB,),
        in_specs=[pl.BlockSpec((None, TS * H, 256), lambda b, pt: (b, 0, 0)),
                  pl.BlockSpec((None, 8, 256), lambda b, pt: (b, 0, 0)),
                  pl.BlockSpec((H, MLA_KV_RANK, H * MLA_V), lambda b, pt: (0, 0, 0)),
                  pl.BlockSpec(memory_space=pl.ANY), pl.BlockSpec(memory_space=pl.ANY)],
        out_specs=pl.BlockSpec((None, TS, H * MLA_V), lambda b, pt: (b, 0, 0)),
        scratch_shapes=[pltpu.VMEM((2, n_keys, MLA_KV_RANK), f32),
                        pltpu.VMEM((2, MLA_ROPE, n_keys), f32),
                        pltpu.VMEM((n_keys, MLA_KV_RANK), bf16),
                        pltpu.SemaphoreType.DMA((2, 2))],
    )
    return pl.pallas_call(
        functools.partial(_paged_kernel, l=l, n_pages=n_pages, page=page, TS=TS, chunk=chunk), name="mla_paged",
        grid_spec=grid_spec,
        out_shape=jax.ShapeDtypeStruct((DB, TS, H * MLA_V), bf16),
        compiler_params=_cparams("arbitrary"),
    )(page_table.reshape(-1), q3, kn3, wuv, cache_ckv, cache_krt)


def _merge_kernel(h_ref, ya_ref, yb_ref, yc_ref, yd_ref, g0, g1, g2, g3, b0, b1, b2, b3, o_ref):
    h = h_ref[...]
    acc = None
    for y_ref, g_ref, b_ref in ((ya_ref, g0, b0), (yb_ref, g1, b1), (yc_ref, g2, b2), (yd_ref, g3, b3)):
        gate = _sigmoid(_dot_nt(h, _bf(g_ref[...])))
        term = gate * _dot(y_ref[...], _bf(b_ref[...]))
        acc = term if acc is None else acc + term
    o_ref[...] = _bf(acc)


def _merge_call(h1, ys, w_in_t, w_br, l):
    N = h1.shape[0]
    D = D_MODEL
    tm = _pick_block(N, (1088, 640, 512, 256, 128))
    tn = 256
    ncol = D // tn
    gate_specs = [pl.BlockSpec((None, tn, D), (lambda i, j, br=br: (l, br * ncol + j, 0))) for br in range(N_BRANCH)]
    br_specs = [pl.BlockSpec((None, None, BRANCH_WIDTH, tn), (lambda i, j, br=br: (l, br, 0, j))) for br in range(N_BRANCH)]
    y_specs = [pl.BlockSpec((tm, BRANCH_WIDTH), lambda i, j: (i, 0)) for _ in range(N_BRANCH)]
    return pl.pallas_call(
        _merge_kernel, name="merge",
        grid=(N // tm, ncol),
        in_specs=[pl.BlockSpec((tm, D), lambda i, j: (i, 0))] + y_specs + gate_specs + br_specs,
        out_specs=pl.BlockSpec((tm, tn), lambda i, j: (i, j)),
        out_shape=jax.ShapeDtypeStruct((N, D), bf16),
        compiler_params=_cparams("arbitrary", "arbitrary"),
    )(h1, *ys, w_in_t, w_in_t, w_in_t, w_in_t, w_br, w_br, w_br, w_br)


def _outproj_kernel(x_ref, mg_ref, wo_ref, g_ref, g1s, g1p, shs, shp, scs, scp, xo_ref, h_ref, hb_ref, *, rows):
    i = pl.program_id(0)
    x = x_ref[...] + rows.pick(i, g1s, g1p) * _dot(mg_ref[...], wo_ref[...])
    xo_ref[...] = x
    h = _rms(x, g_ref[...]) * (1.0 + rows.pick(i, scs, scp)) + rows.pick(i, shs, shp)
    h_ref[...] = h
    hb_ref[...] = _bf(h)


def _outproj_call(rows, x, merged, wo_bf, gain2, mod, l):
    R = rows.R
    D = D_MODEL
    g1s, g1p = rows.mod_specs(l, 2)
    shs, shp = rows.mod_specs(l, 3)
    scs, scp = rows.mod_specs(l, 4)
    return pl.pallas_call(
        functools.partial(_outproj_kernel, rows=rows), name="outproj",
        grid=(rows.nb,),
        in_specs=[pl.BlockSpec((R, D), lambda i: (i, 0)), pl.BlockSpec((R, D), lambda i: (i, 0)),
                  pl.BlockSpec((D, D), lambda i: (0, 0)), pl.BlockSpec((1, D), lambda i: (0, 0)),
                  g1s, g1p, shs, shp, scs, scp],
        out_specs=[pl.BlockSpec((R, D), lambda i: (i, 0))] * 3,
        out_shape=[jax.ShapeDtypeStruct((rows.N, D), f32), jax.ShapeDtypeStruct((rows.N, D), f32),
                   jax.ShapeDtypeStruct((rows.N, D), bf16)],
        compiler_params=_cparams("arbitrary"),
    )(x, merged, wo_bf, gain2.reshape(1, D), mod, mod, mod, mod, mod, mod)


def _peer_candidates():
    K = PEER_TOPK
    pairs = [(k1, k2) for k1 in range(K) for k2 in range(K) if (k1 + 1) * (k2 + 1) <= K]
    n = -(-len(pairs) // 8) * 8
    g1 = np.zeros((n, K), np.float32)
    g2 = np.zeros((n, K), np.float32)
    for r, (k1, k2) in enumerate(pairs):
        g1[r, k1] = 1.0
        g2[r, k2] = 1.0
    return len(pairs), jnp.asarray(g1, bf16), jnp.asarray(g2, bf16)


def _select_rows3(sel_bf, v):
    hi = _bf(v)
    r1 = v - hi.astype(f32)
    mid = _bf(r1)
    lo = _bf(r1 - mid.astype(f32))
    return (_dot(sel_bf, hi) + _dot(sel_bf, mid)) + _dot(sel_bf, lo)


def _peer_score_kernel(h_ref, wqh_ref, wql_ref, k1_ref, k2_ref, g1_ref, g2_ref, a_ref, b_ref, g_ref,
                       v1_scr, i1_scr, v2_scr, i2_scr, top_scr, ea_scr, eb_scr, *, n_cand):
    T = h_ref.shape[0]
    NK, K = PEER_NKEYS, PEER_TOPK
    LT = 128
    NC = g1_ref.shape[0]
    hh, hl = _split2(h_ref[...])
    q = _dot(hh, wqh_ref[...]) + (_dot(hh, wql_ref[...]) + _dot(hl, wqh_ref[...]))
    key_idx = _iota((NK, LT), 0).astype(f32)
    cand_idx = _iota((NC, LT), 0).astype(f32)
    neg_inf = -jnp.inf
    g1 = g1_ref[...]
    g2 = g2_ref[...]

    def topk_into(s, val_scr, idx_scr):
        for kk in range(K):
            m = jnp.max(s, axis=0, keepdims=True)
            pos = jnp.min(jnp.where(s == m, key_idx, float(NK)), axis=0, keepdims=True)
            val_scr[pl.ds(kk, 1), :] = m
            idx_scr[pl.ds(kk, 1), :] = pos
            s = jnp.where(key_idx == pos, neg_inf, s)

    for h in range(PEER_HEADS):
        qh, ql = _split2(q[:, h * PEER_DKEY:(h + 1) * PEER_DKEY])
        sTs = []
        for k_ref in (k1_ref, k2_ref):
            kh, kl = _split2(k_ref[h])
            sTs.append(_dot_nt(kh, qh) + (_dot_nt(kh, ql) + _dot_nt(kl, qh)))
        for lt in range(T // LT):
            cols = slice(lt * LT, (lt + 1) * LT)
            topk_into(sTs[0][:, cols], v1_scr, i1_scr)
            topk_into(sTs[1][:, cols], v2_scr, i2_scr)
            cand = _select_rows3(g1, v1_scr[...]) + _select_rows3(g2, v2_scr[...])
            cand = jnp.where(cand_idx < float(n_cand), cand, neg_inf)
            ca = _dot(g1, _bf(i1_scr[...]))
            cb = _dot(g2, _bf(i2_scr[...]))
            for kk in range(K):
                m = jnp.max(cand, axis=0, keepdims=True)
                pos = jnp.min(jnp.where(cand == m, cand_idx, float(NC)), axis=0, keepdims=True)
                sel = cand_idx == pos
                r = h * K + kk
                top_scr[pl.ds(r, 1), cols] = m
                ea_scr[pl.ds(r, 1), cols] = jnp.sum(jnp.where(sel, ca, 0.0), axis=0, keepdims=True)
                eb_scr[pl.ds(r, 1), cols] = jnp.sum(jnp.where(sel, cb, 0.0), axis=0, keepdims=True)
                cand = jnp.where(sel, neg_inf, cand)
        tops = top_scr[pl.ds(h * K, K), :]
        e = jnp.exp(tops - jnp.max(tops, axis=0, keepdims=True))
        top_scr[pl.ds(h * K, K), :] = e / jnp.sum(e, axis=0, keepdims=True)
    g_ref[...] = top_scr[...].T
    a_ref[...] = ea_scr[...].T
    b_ref[...] = eb_scr[...].T


def _peer_score_call(rows, h2, wqh, wql, k1p, k2p):
    R = rows.R
    D = D_MODEL
    HK = PEER_HEADS * PEER_TOPK
    o_spec = pl.BlockSpec((R, HK), lambda i: (i, 0))
    o_shape = jax.ShapeDtypeStruct((rows.N, HK), f32)
    n_cand, g1, g2 = _peer_candidates()
    return pl.pallas_call(
        functools.partial(_peer_score_kernel, n_cand=n_cand), name="peer_score",
        grid=(rows.nb,),
        in_specs=[pl.BlockSpec((R, D), lambda i: (i, 0)),
                  pl.BlockSpec((D, PEER_HEADS * PEER_DKEY), lambda i: (0, 0)),
                  pl.BlockSpec((D, PEER_HEADS * PEER_DKEY), lambda i: (0, 0)),
                  pl.BlockSpec((PEER_HEADS, PEER_NKEYS, PEER_DKEY), lambda i: (0, 0, 0)),
                  pl.BlockSpec((PEER_HEADS, PEER_NKEYS, PEER_DKEY), lambda i: (0, 0, 0)),
                  pl.BlockSpec(g1.shape, lambda i: (0, 0)), pl.BlockSpec(g2.shape, lambda i: (0, 0))],
        out_specs=[o_spec, o_spec, o_spec],
        out_shape=[o_shape, o_shape, o_shape],
        scratch_shapes=[pltpu.VMEM((PEER_TOPK, 128), f32)] * 4 + [pltpu.VMEM((HK, R), f32)] * 3,
        compiler_params=_cparams("arbitrary"),
    )(h2, wqh, wql, k1p, k2p, g1, g2)


def _peer_mix_kernel(h_ref, ea_ref, eb_ref, g_ref, u_ref, v_ref, o_ref, a_scr, *, T, nch, ac, n_half):
    j = pl.program_id(1)
    NK = PEER_NKEYS
    NA = NK // n_half
    pitch = T + PEER_PITCH_PAD
    half = j // (2 * nch)
    jj = j % (2 * nch)

    @pl.when(j == 0)
    def _():
        o_ref[...] = jnp.zeros(o_ref.shape, f32)

    @pl.when(jj < nch)
    def _():
        res = _dot_nt(h_ref[...], u_ref[...])
        for al in range(ac):
            start = pl.multiple_of((jj * ac + al) * pitch, 8)
            a_scr[pl.ds(start, T), :] = res[:, al * NK:(al + 1) * NK]

    @pl.when(jj == nch)
    def _():
        sub_a = (_iota((NA, NK), 0) + half * NA).astype(f32).astype(bf16)
        sub_b = _iota((NK, NK), 0).astype(f32).astype(bf16)
        one = jnp.ones((NA, NK), bf16)
        zero_a = jnp.zeros((NA, NK), bf16)
        zero_b = jnp.zeros((NK, NK), bf16)

        def gate_matrix(t):
            ia = jnp.broadcast_to(_bf(ea_ref[pl.ds(t, 1), :]), (NA, NK))
            ib = jnp.broadcast_to(_bf(eb_ref[pl.ds(t, 1), :]), (NK, NK))
            gg = jnp.broadcast_to(_bf(g_ref[pl.ds(t, 1), :]), (NK, NK))
            onehot_a = jnp.where(sub_a == ia, one, zero_a)
            gate_b = jnp.where(sub_b == ib, gg, zero_b)
            return _dot_nt(onehot_a, gate_b)

        def tok_group(tg, carry):
            ts = [tg * PEER_TOKEN_UNROLL + u for u in range(PEER_TOKEN_UNROLL)]
            ws = [gate_matrix(t) for t in ts]
            for t, w in zip(ts, ws):
                a_scr[pl.ds(t, NA, stride=pitch), :] = w * _gelu(a_scr[pl.ds(t, NA, stride=pitch), :])
            return carry

        lax.fori_loop(0, T // PEER_TOKEN_UNROLL, tok_group, 0)

    @pl.when(jj >= nch)
    def _():
        parts = []
        for al in range(ac):
            start = pl.multiple_of(((jj - nch) * ac + al) * pitch, 8)
            parts.append(_bf(a_scr[pl.ds(start, T), :]))
        o_ref[...] += _dot(jnp.concatenate(parts, axis=1), v_ref[...])


def _peer_mix_call(rows, h2b, ea, eb, g, ub_all, vb_all, l):
    T = rows.R
    D = D_MODEL
    NK = PEER_NKEYS
    n_half = 2
    ac = 8
    nch = NK // n_half // ac
    HK = PEER_HEADS * PEER_TOPK
    row = lambda i, j: (i, 0)

    def u_idx(i, j):
        return (l, (j // (2 * nch)) * nch + jnp.minimum(j % (2 * nch), nch - 1), 0)

    def v_idx(i, j):
        return (l, (j // (2 * nch)) * nch + jnp.maximum(j % (2 * nch) - nch, 0), 0)

    return pl.pallas_call(
        functools.partial(_peer_mix_kernel, T=T, nch=nch, ac=ac, n_half=n_half), name="peer_mix",
        grid=(rows.nb, n_half * 2 * nch),
        in_specs=[pl.BlockSpec((T, D), row),
                  pl.BlockSpec((T, HK), row), pl.BlockSpec((T, HK), row), pl.BlockSpec((T, HK), row),
                  pl.BlockSpec((None, ac * NK, D), u_idx),
                  pl.BlockSpec((None, ac * NK, D), v_idx)],
        out_specs=pl.BlockSpec((T, D), row),
        out_shape=jax.ShapeDtypeStruct((rows.N, D), f32),
        scratch_shapes=[pltpu.VMEM((NK // n_half * (T + PEER_PITCH_PAD), NK), f32)],
        compiler_params=_cparams("arbitrary", "arbitrary"),
    )(h2b, ea, eb, g, ub_all, vb_all)


def _final_kernel(x_ref, p_ref, g2s, g2p, g_ref, o_ref, *, rows):
    x = x_ref[...] + rows.pick(pl.program_id(0), g2s, g2p) * p_ref[...]
    o_ref[...] = _rms(x, g_ref[...])


def _final_call(rows, x, peer_out, gain, mod, l_last):
    R, D = rows.R, D_MODEL
    g2s, g2p = rows.mod_specs(l_last, 5)
    row = pl.BlockSpec((R, D), lambda i: (i, 0))
    return pl.pallas_call(
        functools.partial(_final_kernel, rows=rows), name="final_norm", grid=(rows.nb,),
        in_specs=[row, row, g2s, g2p, pl.BlockSpec((1, D), lambda i: (0, 0))],
        out_specs=row,
        out_shape=jax.ShapeDtypeStruct((rows.N, D), f32),
        compiler_params=_cparams("arbitrary"),
    )(x, peer_out, mod, mod, gain.reshape(1, D))


def _mla_constants():
    H, R = MLA_HEADS, MLA_ROPE
    half = R // 2
    rot = np.zeros((H * R, H * R), np.float32)
    for g in range(H):
        for jj in range(half):
            rot[g * R + half + jj, g * R + jj] = -1.0
            rot[g * R + jj, g * R + half + jj] = 1.0
    place = np.zeros((H * R, H * 256), np.float32)
    for g in range(H):
        for jj in range(R):
            place[g * R + jj, g * 256 + MLA_KV_RANK + jj] = 1.0
    return jnp.asarray(rot, bf16), jnp.asarray(place, bf16)


def kernel(x_prompt, x_sample, cache_mla_ckv, cache_mla_kr, state_gla, state_mlstm_C, state_mlstm_n, state_mlstm_m, page_table, c_prompt, c_sample, w_ada, b_ada, norm1, norm2, w_in, gla_wa2, gla_ba, gla_norm, mla_w_uq, mla_gq, mla_gkv, mla_w_uk, mla_w_uv, ml_bi, ml_bf, ml_norm, sg_gv, sg_ws, sg_bs, w_br, w_o, peer_wq, peer_k1, peer_k2, peer_u, peer_v, final_norm):
    D = D_MODEL
    B, T, _ = x_prompt.shape
    DB, TS, _ = x_sample.shape
    depth = w_ada.shape[0]
    n_pages = page_table.shape[1]
    page = cache_mla_ckv.shape[2]
    past_len = n_pages * page
    rows = _Rows(B, T, DB, TS)
    Np, Ns = rows.Np, rows.Ns
    assert Ns % 128 == 0 and T % 128 == 0 and TS == 4

    c_rows = jnp.concatenate([jnp.repeat(c_sample, TS, axis=0), c_prompt,
                              jnp.zeros((16 - B, D), f32)], axis=0)
    mod = _ada_call(c_rows, w_ada, b_ada)

    x = jnp.concatenate([x_prompt.reshape(Np, D), x_sample.reshape(Ns, D)], axis=0)

    inv = ROPE_THETA ** (-jnp.arange(MLA_ROPE // 2, dtype=f32) / (MLA_ROPE // 2))
    pos_p = jnp.arange(T, dtype=jnp.int32)
    pos_s = past_len + jnp.arange(TS, dtype=jnp.int32)

    def table(fn):
        tp = fn(pos_p.astype(f32)[:, None] * inv)
        ts = fn(pos_s.astype(f32)[:, None] * inv)
        rows_ = jnp.concatenate([jnp.tile(tp, (B, 1)), jnp.tile(ts, (DB, 1))], axis=0)
        return jnp.tile(rows_, (1, 2 * MLA_HEADS))

    cos_t, sin_t = table(jnp.cos), table(jnp.sin)
    rot, place = _mla_constants()
    eye_h = jnp.eye(MLA_HEADS, dtype=f32)
    Ls = math.gcd(TS, GLA_CHUNK)
    Lp_gla = math.gcd(T, GLA_CHUNK)
    Lp_ml = math.gcd(T, ML_CHUNK)
    Ls_ml = math.gcd(TS, ML_CHUNK)
    spb = SG_CHUNK // TS
    w_in_t = jnp.swapaxes(w_in, 1, 2)
    cache_krt = jnp.swapaxes(cache_mla_kr, 2, 3)
    w_mix_all = _bf(w_in_t[:, N_BRANCH * D:, :])
    ub_all = _bf(peer_u)
    vb_all = _bf(peer_v)
    rows_peer = _Rows(B, T, DB, TS, block_sizes=(512, 256, 128))

    new_p, new_s = [], []
    for l in range(depth):
        wl = w_mix_all[l]
        zr = lambda n: jnp.zeros((n, D), bf16)
        wA = jnp.concatenate([wl[0:1552], zr(112)], axis=0)
        wB = jnp.concatenate([wl[1552:2096], zr(96)], axis=0)
        wC = jnp.concatenate([wl[2096:3632], wl[3632:3636], zr(124), wl[3636:3640], zr(124)], axis=0)
        wD = wl[3640:4664]
        wa2p = jnp.concatenate([gla_wa2[l], jnp.zeros((128 - GLA_RANK, GLA_HEADS * GLA_DK), f32)], axis=0)
        uq = mla_w_uq[l].reshape(MLA_Q_RANK, MLA_HEADS, MLA_NOPE + MLA_ROPE)
        wqn = _bf(uq[:, :, :MLA_NOPE].reshape(MLA_Q_RANK, MLA_HEADS * MLA_NOPE))
        wqr = _bf(uq[:, :, MLA_NOPE:].reshape(MLA_Q_RANK, MLA_HEADS * MLA_ROPE))
        uk = jnp.transpose(mla_w_uk[l], (1, 2, 0))
        wa = jnp.einsum('hnr,hg->hngr', uk, eye_h)
        wa = _bf(jnp.pad(wa, ((0, 0), (0, 0), (0, 0), (0, 256 - MLA_KV_RANK))).reshape(MLA_HEADS * MLA_NOPE, MLA_HEADS * 256))
        wuv = _bf(jnp.einsum('rhv,hg->hrgv', mla_w_uv[l], eye_h).reshape(MLA_HEADS, MLA_KV_RANK, MLA_HEADS * MLA_V))
        pad128 = lambda a: jnp.pad(a, (0, 128 - a.shape[0])).reshape(1, 128)
        bcol_p = jnp.pad(sg_bs[l].T, ((0, 0), (0, 128 - SG_GROUPS)))
        ws_s = jnp.stack([jnp.kron(jnp.eye(spb, dtype=f32), sg_ws[l, g, :TS, :TS]) for g in range(SG_GROUPS)])
        bcol_s = jnp.pad(jnp.tile(sg_bs[l][:, :TS].T, (spb, 1)), ((0, 0), (0, 128 - SG_GROUPS)))
        wo_bf = _bf(w_o[l])
        wq = peer_wq[l]
        wqh = _bf(wq)
        wql = _bf(wq - wqh.astype(f32))
        half = PEER_DKEY // 2
        k1p = jnp.pad(peer_k1[l], ((0, 0), (0, 0), (0, half)))
        k2p = jnp.pad(peer_k2[l], ((0, 0), (0, 0), (half, 0)))

        if l == 0:
            h1 = _modnorm_call(rows, x, None, norm1[l], mod, l)
        else:
            x, h1 = _modnorm_call(rows, x, peer_out, norm1[l], mod, l)
        zA, zB, zC, zD = (_mm_call(h1, w) for w in (wA, wB, wC, wD))

        gn_a = gla_norm[l].reshape(1, -1)
        ba = gla_ba[l].reshape(1, -1)
        ya_p, S_p = _gla_call(zA, 0, B, T, Lp_gla, wa2p, ba, gn_a, None)
        ya_s, S_s = _gla_call(zA, Np, DB, TS, Ls, wa2p, ba, gn_a, state_gla[l])

        qcat, kcat = _mla_prep_call(rows, zB, cos_t, sin_t, mla_gq[l].reshape(1, -1), mla_gkv[l].reshape(1, -1),
                                    wqn, wqr, wa, place, rot)
        yb_p = _flash_call(qcat, kcat, wuv, B, T)
        q3 = qcat[Np:].reshape(DB, TS * MLA_HEADS, 256)
        kn3 = jnp.pad(kcat[Np:].reshape(DB, TS, 256), ((0, 0), (0, 8 - TS), (0, 0)))
        yb_s = _paged_call(page_table, q3, kn3, wuv, cache_mla_ckv, cache_krt, l).reshape(Ns, -1)

        bi, bfg, gn_c = pad128(ml_bi[l]), pad128(ml_bf[l]), ml_norm[l].reshape(1, -1)
        yc_p, C_p, n_p, m_p = _mlstm_call(zC, 0, B, T, Lp_ml, bi, bfg, gn_c, None)
        st = (state_mlstm_C[l], state_mlstm_n[l].reshape(DB, 1, -1),
              jnp.pad(state_mlstm_m[l], ((0, 0), (0, 128 - ML_HEADS))).reshape(DB, 1, 128))
        yc_s, C_s, n_s, m_s = _mlstm_call(zC, Np, DB, TS, Ls_ml, bi, bfg, gn_c, st)

        gv = sg_gv[l].reshape(1, -1)
        (yd_p,) = _sgu_call(zD, 0, Np, gv, sg_ws[l], bcol_p, False)
        yd_s, v_s = _sgu_call(zD, Np, Ns, gv, ws_s, bcol_s, True)

        ys = [jnp.concatenate([p_, s_], axis=0) for p_, s_ in ((ya_p, ya_s), (yb_p, yb_s), (yc_p, yc_s), (yd_p, yd_s))]
        merged = _merge_call(h1, ys, w_in_t, w_br, l)
        x, h2, h2b = _outproj_call(rows, x, merged, wo_bf, norm2[l], mod, l)

        ea, eb, g = _peer_score_call(rows, h2, wqh, wql, k1p, k2p)
        peer_out = _peer_mix_call(rows_peer, h2b, ea, eb, g, ub_all, vb_all, l)

        new_p.append(dict(ckv=kcat[:Np, :MLA_KV_RANK].reshape(B, T, -1),
                          kr=kcat[:Np, MLA_KV_RANK:MLA_KV_RANK + MLA_ROPE].reshape(B, T, -1),
                          S=S_p, C=C_p, n=n_p.reshape(B, ML_HEADS, ML_DK), m=m_p[:, 0, :ML_HEADS]))
        new_s.append(dict(ckv=kcat[Np:, :MLA_KV_RANK].reshape(DB, TS, -1),
                          kr=kcat[Np:, MLA_KV_RANK:MLA_KV_RANK + MLA_ROPE].reshape(DB, TS, -1),
                          S=S_s, C=C_s, n=n_s.reshape(DB, ML_HEADS, ML_DK), m=m_s[:, 0, :ML_HEADS],
                          v=v_s.reshape(DB, TS, -1)))

    y = _final_call(rows, x, peer_out, final_norm, mod, depth - 1)
    stk = lambda outs, name: jnp.stack([o[name] for o in outs], axis=0)
    return (y[:Np].reshape(B, T, D), y[Np:].reshape(DB, TS, D),
            stk(new_p, 'ckv'), stk(new_p, 'kr'), stk(new_p, 'S'), stk(new_p, 'C'), stk(new_p, 'n'), stk(new_p, 'm'),
            stk(new_s, 'ckv'), stk(new_s, 'kr'), stk(new_s, 'S'), stk(new_s, 'C'), stk(new_s, 'n'), stk(new_s, 'm'),
            stk(new_s, 'v'))
```

```python
import functools
import math

import numpy as np
import jax
import jax.numpy as jnp
from jax import lax
from jax.experimental import pallas as pl
from jax.experimental.pallas import tpu as pltpu

f32 = jnp.float32
bf16 = jnp.bfloat16

D_MODEL = 2048
EPS = 1e-6
ADA_CHUNKS = 6
GLA_HEADS, GLA_DK, GLA_DV, GLA_RANK, GLA_TAU, GLA_CHUNK = 4, 64, 128, 16, 16.0, 16
MLA_HEADS, MLA_Q_RANK, MLA_KV_RANK, MLA_NOPE, MLA_ROPE, MLA_V = 8, 384, 128, 64, 32, 64
MLA_SCALE = (MLA_NOPE + MLA_ROPE) ** -0.5
ROPE_THETA = 10000.0
ML_HEADS, ML_DK, ML_DV, ML_CHUNK = 4, 64, 128, 64
SG_GROUPS, SG_CHUNK, SG_WIDTH = 4, 128, 512
PEER_HEADS, PEER_NKEYS, PEER_DKEY, PEER_TOPK = 8, 128, 128, 16
PEER_EXPERTS = PEER_NKEYS * PEER_NKEYS
BRANCH_WIDTH = 512
N_BRANCH = 4

W_A = 1664
W_B = 640
W_C = 1792
W_D = 1024

NEG = -1e30
VMEM_LIMIT_V7X = 56 * 1024 * 1024
PEER_PITCH_PAD = 8
PEER_TOKEN_UNROLL = 32

def _cparams(*sem):
    return pltpu.CompilerParams(dimension_semantics=sem, vmem_limit_bytes=VMEM_LIMIT_V7X)


def _bf(x):
    return x.astype(bf16)


def _dot(a, b):
    return jnp.dot(a, b, preferred_element_type=f32)


def _dot_nt(a, b):
    return lax.dot_general(a, b, (((1,), (1,)), ((), ())), preferred_element_type=f32)


def _split2(x):
    hi = _bf(x)
    lo = _bf(x - hi.astype(f32))
    return hi, lo


def _dot3(a, b):
    ah, al = _split2(a)
    bh, bl = _split2(b)
    return _dot(ah, bh) + (_dot(ah, bl) + _dot(al, bh))


def _dot2b(a, b_bf):
    ah, al = _split2(a)
    return _dot(ah, b_bf) + _dot(al, b_bf)


def _sigmoid(x):
    return 1.0 / (1.0 + jnp.exp(-x))


def _log_sigmoid(x):
    return jnp.minimum(x, 0.0) - jnp.log1p(jnp.exp(-jnp.abs(x)))


def _gelu(x):
    return jax.nn.gelu(x)


def _rms(x, g):
    return x * lax.rsqrt(jnp.mean(x * x, axis=-1, keepdims=True) + EPS) * g


def _iota(shape, dim):
    return lax.broadcasted_iota(jnp.int32, shape, dim)


def _seg_cumsum(x, rowmod, L):
    s = 1
    while s < L:
        x = x + jnp.where(rowmod >= s, pltpu.roll(x, s, 0), 0.0)
        s *= 2
    return x


def _seg_last(x, rowmod, L):
    R = x.shape[0]
    s = 1
    while s < L:
        x = jnp.where(rowmod + s < L, pltpu.roll(x, R - s, 0), x)
        s *= 2
    return x


def _expand_heads(x, n_heads, width):
    R = x.shape[0]
    lane_head = _iota((R, n_heads * width), 1) // width
    out = jnp.zeros((R, n_heads * width), f32)
    for h in range(n_heads):
        out = jnp.where(lane_head == h, x[:, h:h + 1], out)
    return out


def _pick_block(n, cands):
    for c in cands:
        if n % c == 0:
            return c
    raise ValueError(f"no block size in {cands} divides {n}")


def _ada_kernel(c_ref, w_ref, b_ref, o_ref):
    c = c_ref[...]
    a = _bf(c * _sigmoid(c))
    o_ref[...] = _dot(a, _bf(w_ref[...])) + b_ref[...]


def _ada_call(c_rows, w_ada, b_ada):
    L, D, N6 = w_ada.shape
    Rc = c_rows.shape[0]
    tn = 1024
    return pl.pallas_call(
        _ada_kernel, name="ada_mod",
        grid=(L, N6 // tn),
        in_specs=[pl.BlockSpec((Rc, D), lambda l, j: (0, 0)),
                  pl.BlockSpec((None, D, tn), lambda l, j: (l, 0, j)),
                  pl.BlockSpec((None, 1, tn), lambda l, j: (l, 0, j))],
        out_specs=pl.BlockSpec((None, Rc, tn), lambda l, j: (l, 0, j)),
        out_shape=jax.ShapeDtypeStruct((L, Rc, N6), f32),
        compiler_params=_cparams("arbitrary", "arbitrary"),
    )(c_rows, w_ada, b_ada.reshape(L, 1, N6))


class _Rows:
    def __init__(self, B, T, DB, TS, block_sizes=(256, 128)):
        self.B, self.T, self.DB, self.TS = B, T, DB, TS
        self.Np, self.Ns = B * T, DB * TS
        self.N = self.Np + self.Ns
        self.R = _pick_block(math.gcd(T, self.Ns), block_sizes)
        self.nPb = self.Np // self.R
        self.nSb = self.Ns // self.R
        self.nb = self.nPb + self.nSb
        self.blocks_per_seq = T // self.R

    def mod_specs(self, l, chunk):
        R, nPb, nSb, Ns = self.R, self.nPb, self.nSb, self.Ns
        s = pl.BlockSpec((None, R, D_MODEL), lambda i: (l, jnp.clip(i - nPb, 0, nSb - 1), chunk))
        p = pl.BlockSpec((None, 8, D_MODEL), lambda i: (l, Ns // 8, chunk))
        return s, p

    def pick(self, i, s_ref, p_ref):
        b = jnp.minimum(i // self.blocks_per_seq, self.B - 1)
        return jnp.where(i >= self.nPb, s_ref[...], p_ref[pl.ds(b, 1), :])


def _modnorm_kernel(*refs, rows, with_peer):
    i = pl.program_id(0)
    if with_peer:
        x_ref, p_ref, g2s, g2p, g_ref, shs_ref, shp_ref, scs_ref, scp_ref, xo_ref, h_ref = refs
        x = x_ref[...] + rows.pick(i, g2s, g2p) * p_ref[...]
        xo_ref[...] = x
    else:
        x_ref, g_ref, shs_ref, shp_ref, scs_ref, scp_ref, h_ref = refs
        x = x_ref[...]
    sh = rows.pick(i, shs_ref, shp_ref)
    sc = rows.pick(i, scs_ref, scp_ref)
    h_ref[...] = _bf(_rms(x, g_ref[...]) * (1.0 + sc) + sh)


def _modnorm_call(rows, x, peer_out, gain, mod, l):
    R = rows.R
    shs, shp = rows.mod_specs(l, 0)
    scs, scp = rows.mod_specs(l, 1)
    row = pl.BlockSpec((R, D_MODEL), lambda i: (i, 0))
    vec = pl.BlockSpec((1, D_MODEL), lambda i: (0, 0))
    h_shape = jax.ShapeDtypeStruct((rows.N, D_MODEL), bf16)
    if peer_out is None:
        in_specs = [row, vec, shs, shp, scs, scp]
        args = (x, gain.reshape(1, D_MODEL), mod, mod, mod, mod)
        out_specs, out_shape = row, h_shape
    else:
        g2s, g2p = rows.mod_specs(l - 1, 5)
        in_specs = [row, row, g2s, g2p, vec, shs, shp, scs, scp]
        args = (x, peer_out, mod, mod, gain.reshape(1, D_MODEL), mod, mod, mod, mod)
        out_specs = [row, row]
        out_shape = [jax.ShapeDtypeStruct((rows.N, D_MODEL), f32), h_shape]
    return pl.pallas_call(
        functools.partial(_modnorm_kernel, rows=rows, with_peer=peer_out is not None), name="modnorm",
        grid=(rows.nb,), in_specs=in_specs, out_specs=out_specs, out_shape=out_shape,
        compiler_params=_cparams("arbitrary"),
    )(*args)


def _mm_kernel(x_ref, wt_ref, o_ref):
    o_ref[...] = _dot_nt(x_ref[...], wt_ref[...]).astype(o_ref.dtype)


def _mm_call(x, wt, out_dtype=f32):
    M, K = x.shape
    N = wt.shape[0]
    tm = _pick_block(M, (1088, 1024, 640, 512, 256, 128))
    return pl.pallas_call(
        _mm_kernel, name="mixer_proj",
        grid=(M // tm,),
        in_specs=[pl.BlockSpec((tm, K), lambda i: (i, 0)),
                  pl.BlockSpec((N, K), lambda i: (0, 0))],
        out_specs=pl.BlockSpec((tm, N), lambda i: (i, 0)),
        out_shape=jax.ShapeDtypeStruct((M, N), out_dtype),
        compiler_params=_cparams("arbitrary"),
    )(x, wt)


def _gla_kernel(*refs, R, L, carry):
    if carry:
        z_ref, wa2_ref, ba_ref, gn_ref, y_ref, so_ref, s_scr = refs
    else:
        z_ref, wa2_ref, ba_ref, gn_ref, s0_ref, y_ref, so_ref = refs
    H, DK, DV = GLA_HEADS, GLA_DK, GLA_DV
    HK, HV = H * DK, H * DV
    nchunk = R // L
    z = z_ref[...]
    q = z[:, 0:HK] * (DK ** -0.5)
    k = z[:, HK:2 * HK]
    v = z[:, 2 * HK:2 * HK + HV]
    r = z[:, 2 * HK + HV:2 * HK + 2 * HV]
    ag = z[:, 2 * HK + 2 * HV:2 * HK + 2 * HV + 128]
    la = _log_sigmoid(_dot3(ag, wa2_ref[...]) + ba_ref[...]) * (1.0 / GLA_TAU)
    rowmod = _iota((R, HK), 0) % L
    b = _seg_cumsum(la, rowmod, L)
    tot = _seg_last(b, rowmod, L)

    sel_row_head = _iota((HK, 128), 0) // DK
    sel_lane = _iota((HK, 128), 1)
    att = jnp.zeros((R, 128), f32)
    for d in range(L):
        ks = k if d == 0 else pltpu.roll(k, d, 0)
        bs = b if d == 0 else pltpu.roll(b, d, 0)
        p = q * ks * jnp.exp(jnp.where(rowmod >= d, b - bs, NEG))
        att = att + _dot2b(p, _bf(sel_lane == d * H + sel_row_head))
    ex_row = _iota((128, HV), 0)
    ex_lane_head = _iota((128, HV), 1) // DV
    o = jnp.zeros((R, HV), f32)
    att_b = _bf(att)
    for d in range(L):
        vs = v if d == 0 else pltpu.roll(v, d, 0)
        o = o + _dot(att_b, _bf(ex_row == d * H + ex_lane_head)) * vs

    qt = _bf(q * jnp.exp(b))
    kd = k * jnp.exp(tot - b)
    kdT = _bf(kd.T)
    totT = tot.T
    vb = _bf(v)
    blockdiag = (_iota((HK, HV), 0) // DK) == (_iota((HK, HV), 1) // DV)
    row_chunk = _iota((R, HK), 0) // L
    col_chunk = _iota((HK, R), 1) // L
    if carry:
        @pl.when(pl.program_id(1) == 0)
        def _():
            s_scr[...] = jnp.zeros((HK, HV), f32)
        S = s_scr[...]
    for c in range(nchunk):
        if not carry:
            rows_ = [jnp.concatenate([s0_ref[c, h] if g == h else jnp.zeros((DK, DV), f32) for g in range(H)], axis=1)
                     for h in range(H)]
            S = jnp.concatenate(rows_, axis=0)
        o = o + _dot(jnp.where(row_chunk == c, qt, jnp.zeros_like(qt)), _bf(S))
        U = _dot(jnp.where(col_chunk == c, kdT, jnp.zeros_like(kdT)), vb)
        dcol = jnp.exp(totT[:, c * L:c * L + 1])
        S = dcol * S + jnp.where(blockdiag, U, 0.0)
        if not carry:
            for h in range(H):
                so_ref[c, h] = S[h * DK:(h + 1) * DK, h * DV:(h + 1) * DV]
    if carry:
        s_scr[...] = S

        @pl.when(pl.program_id(1) == pl.num_programs(1) - 1)
        def _():
            for h in range(H):
                so_ref[h] = S[h * DK:(h + 1) * DK, h * DV:(h + 1) * DV]

    gn = gn_ref[...]
    outs = []
    for h in range(H):
        oh = o[:, h * DV:(h + 1) * DV]
        outs.append(_rms(oh, gn[:, h * DV:(h + 1) * DV]))
    y_ref[...] = _bf(jnp.concatenate(outs, axis=1) * (r * _sigmoid(r)))


def _gla_call(zA, row0, B, T, L, wa2p, ba, gn, s0):
    H, DK, DV = GLA_HEADS, GLA_DK, GLA_DV
    carry = s0 is None
    if carry:
        R = _pick_block(T, (256, 128))
        nt = T // R
        grid = (B, nt)
        rb0 = row0 // R
        in_specs = [pl.BlockSpec((R, W_A), lambda b, t: (rb0 + b * nt + t, 0)),
                    pl.BlockSpec((128, H * DK), lambda b, t: (0, 0)),
                    pl.BlockSpec((1, H * DK), lambda b, t: (0, 0)),
                    pl.BlockSpec((1, H * DV), lambda b, t: (0, 0))]
        out_specs = [pl.BlockSpec((R, H * DV), lambda b, t: (b * nt + t, 0)),
                     pl.BlockSpec((None, H, DK, DV), lambda b, t: (b, 0, 0, 0))]
        scratch = [pltpu.VMEM((H * DK, H * DV), f32)]
        args = (zA, wa2p, ba, gn)
        sem = ("arbitrary", "arbitrary")
    else:
        R = 128
        nb = (B * T) // R
        cpb = R // L
        grid = (nb,)
        rb0 = row0 // R
        in_specs = [pl.BlockSpec((R, W_A), lambda i: (rb0 + i, 0)),
                    pl.BlockSpec((128, H * DK), lambda i: (0, 0)),
                    pl.BlockSpec((1, H * DK), lambda i: (0, 0)),
                    pl.BlockSpec((1, H * DV), lambda i: (0, 0)),
                    pl.BlockSpec((cpb, H, DK, DV), lambda i: (i, 0, 0, 0))]
        out_specs = [pl.BlockSpec((R, H * DV), lambda i: (i, 0)),
                     pl.BlockSpec((cpb, H, DK, DV), lambda i: (i, 0, 0, 0))]
        scratch = []
        args = (zA, wa2p, ba, gn, s0)
        sem = ("arbitrary",)
    return pl.pallas_call(
        functools.partial(_gla_kernel, R=R, L=L, carry=carry), name="gla_prompt" if carry else "gla_sample",
        grid=grid, in_specs=in_specs, out_specs=out_specs,
        out_shape=[jax.ShapeDtypeStruct((B * T, H * DV), bf16),
                   jax.ShapeDtypeStruct((B, H, DK, DV), f32)],
        scratch_shapes=scratch,
        compiler_params=_cparams(*sem),
    )(*args)


def _mlstm_kernel(*refs, R, L, carry):
    if carry:
        z_ref, bi_ref, bf_ref, gn_ref, y_ref, co_ref, no_ref, mo_ref, c_scr, n_scr, m_scr = refs
    else:
        z_ref, bi_ref, bf_ref, gn_ref, c0_ref, n0_ref, m0_ref, y_ref, co_ref, no_ref, mo_ref = refs
    H, DK, DV = ML_HEADS, ML_DK, ML_DV
    HK, HV = H * DK, H * DV
    nchunk = R // L
    z = z_ref[...]
    q = z[:, 0:HK]
    k = z[:, HK:2 * HK] * (DK ** -0.5)
    v = z[:, 2 * HK:2 * HK + HV]
    og = z[:, 2 * HK + HV:2 * HK + 2 * HV]
    ig = z[:, 2 * HK + 2 * HV:2 * HK + 2 * HV + 128] + bi_ref[...]
    fg = z[:, 2 * HK + 2 * HV + 128:2 * HK + 2 * HV + 256] + bf_ref[...]
    rowmod = _iota((R, 128), 0) % L
    row_chunk128 = _iota((R, 128), 0) // L
    F = _seg_cumsum(_log_sigmoid(fg), rowmod, L)
    FT = F.T
    IT = ig.T

    if carry:
        @pl.when(pl.program_id(1) == 0)
        def _():
            c_scr[...] = jnp.zeros((HK, HV), f32)
            n_scr[...] = jnp.zeros((1, HK), f32)
            m_scr[...] = jnp.zeros((1, 128), f32)

    tpos = _iota((R, R), 0)
    spos = _iota((R, R), 1)
    causal = (tpos // L == spos // L) & (spos <= tpos)
    qb = _bf(q)
    kb = _bf(k)
    vb = _bf(v)
    lane_head_k = _iota((R, HK), 1) // DK

    Dms, qks, MX = [], [], jnp.zeros((R, 128), f32)
    lane128 = _iota((R, 128), 1)
    for h in range(H):
        Dm = jnp.where(causal, F[:, h:h + 1] - FT[h:h + 1, :] + IT[h:h + 1, :], NEG)
        Dms.append(Dm)
        MX = jnp.where(lane128 == h, jnp.max(Dm, axis=1, keepdims=True), MX)
        qks.append(_dot_nt(jnp.where(lane_head_k == h, qb, jnp.zeros_like(qb)), kb))

    if carry:
        Mprev = jnp.zeros((R, 128), f32)
        m_run = m_scr[...]
        for c in range(nchunk):
            Mprev = jnp.where(row_chunk128 == c, m_run, Mprev)
            last = c * L + L - 1
            m_run = jnp.maximum(F[last:last + 1, :] + m_run, MX[last:last + 1, :])
    else:
        Mprev = jnp.zeros((R, 128), f32)
        for c in range(nchunk):
            Mprev = jnp.where(row_chunk128 == c, m0_ref[c], Mprev)
    Mt = jnp.maximum(F + Mprev, MX)
    Mnew = _seg_last(Mt, rowmod, L)
    Fend = _seg_last(F, rowmod, L)
    Wprev = jnp.exp(F + Mprev - Mt)
    Wsrc = jnp.exp(Fend - F + ig - Mnew)
    Wold = jnp.exp(Fend + Mprev - Mnew)
    Emt = jnp.exp(-Mt)

    kw = k * _expand_heads(Wsrc, H, DK)
    kwT = _bf(kw.T)
    Wold_k = _expand_heads(Wold, H, DK)
    Wold_v = _expand_heads(Wold, H, DV)
    blockdiag = (_iota((HK, HV), 0) // DK) == (_iota((HK, HV), 1) // DV)
    row_chunk = _iota((R, HK), 0) // L
    col_chunk = _iota((HK, R), 1) // L

    qC = jnp.zeros((R, HV), f32)
    Nrows = jnp.zeros((R, HK), f32)
    if carry:
        C = c_scr[...]
        n = n_scr[...]
    for c in range(nchunk):
        if not carry:
            rows_ = [jnp.concatenate([c0_ref[c, h] if g == h else jnp.zeros((DK, DV), f32) for g in range(H)], axis=1)
                     for h in range(H)]
            C = jnp.concatenate(rows_, axis=0)
            n = n0_ref[c]
        in_chunk = row_chunk == c
        qC = qC + _dot(jnp.where(in_chunk, qb, jnp.zeros_like(qb)), _bf(C))
        Nrows = jnp.where(in_chunk, n, Nrows)
        last = c * L + L - 1
        U = _dot(jnp.where(col_chunk == c, kwT, jnp.zeros_like(kwT)), vb)
        C = Wold_v[last:last + 1, :] * C + jnp.where(blockdiag, U, 0.0)
        n = Wold_k[last:last + 1, :] * n + jnp.sum(jnp.where(in_chunk, kw, 0.0), axis=0, keepdims=True)
        if not carry:
            for h in range(H):
                co_ref[c, h] = C[h * DK:(h + 1) * DK, h * DV:(h + 1) * DV]
            no_ref[c] = n
            mo_ref[c] = Mnew[last:last + 1, :]
    if carry:
        c_scr[...] = C
        n_scr[...] = n
        m_scr[...] = Mnew[R - 1:R, :]

        @pl.when(pl.program_id(1) == pl.num_programs(1) - 1)
        def _():
            for h in range(H):
                co_ref[h] = C[h * DK:(h + 1) * DK, h * DV:(h + 1) * DV]
            no_ref[...] = n
            mo_ref[...] = Mnew[R - 1:R, :]

    qn = _dot2b(q * Nrows, _bf(_iota((HK, 128), 1) == _iota((HK, 128), 0) // DK))
    gn = gn_ref[...]
    outs = []
    for h in range(H):
        A = jnp.exp(Dms[h] - Mt[:, h:h + 1]) * qks[h]
        num = Wprev[:, h:h + 1] * qC[:, h * DV:(h + 1) * DV] + _dot(_bf(A), vb[:, h * DV:(h + 1) * DV])
        nq = Wprev[:, h:h + 1] * qn[:, h:h + 1] + jnp.sum(A, axis=1, keepdims=True)
        hh = num / jnp.maximum(jnp.abs(nq), Emt[:, h:h + 1])
        outs.append(_rms(hh, gn[:, h * DV:(h + 1) * DV]))
    y_ref[...] = _bf(_sigmoid(og) * jnp.concatenate(outs, axis=1))


def _mlstm_call(zC, row0, B, T, L, bi, bfg, gn, st):
    H, DK, DV = ML_HEADS, ML_DK, ML_DV
    carry = st is None
    R = 128
    out_shape = [jax.ShapeDtypeStruct((B * T, H * DV), bf16),
                 jax.ShapeDtypeStruct((B, H, DK, DV), f32),
                 jax.ShapeDtypeStruct((B, 1, H * DK), f32),
                 jax.ShapeDtypeStruct((B, 1, 128), f32)]
    rb0 = row0 // R
    if carry:
        nt = T // R
        grid = (B, nt)
        c2 = lambda b, t: (0, 0)
        in_specs = [pl.BlockSpec((R, W_C), lambda b, t: (rb0 + b * nt + t, 0)),
                    pl.BlockSpec((1, 128), c2), pl.BlockSpec((1, 128), c2), pl.BlockSpec((1, H * DV), c2)]
        out_specs = [pl.BlockSpec((R, H * DV), lambda b, t: (b * nt + t, 0)),
                     pl.BlockSpec((None, H, DK, DV), lambda b, t: (b, 0, 0, 0)),
                     pl.BlockSpec((None, 1, H * DK), lambda b, t: (b, 0, 0)),
                     pl.BlockSpec((None, 1, 128), lambda b, t: (b, 0, 0))]
        scratch = [pltpu.VMEM((H * DK, H * DV), f32), pltpu.VMEM((1, H * DK), f32), pltpu.VMEM((1, 128), f32)]
        args = (zC, bi, bfg, gn)
        sem = ("arbitrary", "arbitrary")
    else:
        c0, n0, m0 = st
        nb = (B * T) // R
        cpb = R // L
        grid = (nb,)
        c2 = lambda i: (0, 0)
        in_specs = [pl.BlockSpec((R, W_C), lambda i: (rb0 + i, 0)),
                    pl.BlockSpec((1, 128), c2), pl.BlockSpec((1, 128), c2), pl.BlockSpec((1, H * DV), c2),
                    pl.BlockSpec((cpb, H, DK, DV), lambda i: (i, 0, 0, 0)),
                    pl.BlockSpec((cpb, 1, H * DK), lambda i: (i, 0, 0)),
                    pl.BlockSpec((cpb, 1, 128), lambda i: (i, 0, 0))]
        out_specs = [pl.BlockSpec((R, H * DV), lambda i: (i, 0)),
                     pl.BlockSpec((cpb, H, DK, DV), lambda i: (i, 0, 0, 0)),
                     pl.BlockSpec((cpb, 1, H * DK), lambda i: (i, 0, 0)),
                     pl.BlockSpec((cpb, 1, 128), lambda i: (i, 0, 0))]
        scratch = []
        args = (zC, bi, bfg, gn, c0, n0, m0)
        sem = ("arbitrary",)
    return pl.pallas_call(
        functools.partial(_mlstm_kernel, R=R, L=L, carry=carry), name="mlstm_prompt" if carry else "mlstm_sample",
        grid=grid, in_specs=in_specs, out_specs=out_specs, out_shape=out_shape,
        scratch_shapes=scratch, compiler_params=_cparams(*sem),
    )(*args)


def _sgu_kernel(z_ref, gv_ref, ws_ref, bcol_ref, y_ref, *v_out, nch):
    G = SG_GROUPS
    W = SG_WIDTH
    cw = W // G
    tril = _iota((SG_CHUNK, SG_CHUNK), 1) <= _iota((SG_CHUNK, SG_CHUNK), 0)
    for c in range(nch):
        rs = slice(c * SG_CHUNK, (c + 1) * SG_CHUNK)
        u = z_ref[rs, 0:W]
        vn = _rms(_gelu(z_ref[rs, W:2 * W]), gv_ref[...])
        if v_out:
            v_out[0][rs, :] = vn
        vb = _bf(vn)
        bcol = bcol_ref[...]
        outs = []
        for g in range(G):
            wc = _bf(jnp.where(tril, ws_ref[g], 0.0))
            outs.append(_dot(wc, vb[:, g * cw:(g + 1) * cw]) + bcol[:, g:g + 1])
        y_ref[rs, :] = _bf(_gelu(u) * jnp.concatenate(outs, axis=1))


def _sgu_call(zD, row0, n_rows, gv, ws, bcol, want_v):
    nch = _pick_block(n_rows // SG_CHUNK, (4, 2, 1))
    Rb = nch * SG_CHUNK
    rb0 = row0 // Rb
    out_shape = [jax.ShapeDtypeStruct((n_rows, SG_WIDTH), bf16)]
    out_specs = [pl.BlockSpec((Rb, SG_WIDTH), lambda i: (i, 0))]
    if want_v:
        out_shape.append(jax.ShapeDtypeStruct((n_rows, SG_WIDTH), f32))
        out_specs.append(pl.BlockSpec((Rb, SG_WIDTH), lambda i: (i, 0)))
    return pl.pallas_call(
        functools.partial(_sgu_kernel, nch=nch), name="sgu",
        grid=(n_rows // Rb,),
        in_specs=[pl.BlockSpec((Rb, W_D), lambda i: (rb0 + i, 0)),
                  pl.BlockSpec((1, SG_WIDTH), lambda i: (0, 0)),
                  pl.BlockSpec((SG_GROUPS, SG_CHUNK, SG_CHUNK), lambda i: (0, 0, 0)),
                  pl.BlockSpec((SG_CHUNK, 128), lambda i: (0, 0))],
        out_specs=out_specs, out_shape=out_shape,
        compiler_params=_cparams("arbitrary"),
    )(zD, gv, ws, bcol)


def _mla_prep_kernel(z_ref, cos_ref, sin_ref, gq_ref, gkv_ref, wqn_ref, wqr_ref, wa_ref, pb_ref, rot_ref,
                     q_ref, k_ref):
    z = z_ref[...]
    dq = z[:, 0:MLA_Q_RANK]
    dkv = z[:, MLA_Q_RANK:MLA_Q_RANK + MLA_KV_RANK]
    kr = z[:, MLA_Q_RANK + MLA_KV_RANK:MLA_Q_RANK + MLA_KV_RANK + 128]
    cos = cos_ref[...]
    sin = sin_ref[...]
    rot = rot_ref[...]
    dqn = _bf(_rms(dq, gq_ref[...]))
    qn = _dot(dqn, wqn_ref[...])
    qr = _dot(dqn, wqr_ref[...])
    qrope = qr * cos + _dot2b(qr, rot) * sin
    q_ref[...] = _bf(_dot(_bf(qn), wa_ref[...]) + _dot(_bf(qrope), pb_ref[...]))
    ckv = _rms(dkv, gkv_ref[...])
    krope = kr * cos[:, 0:128] + _dot2b(kr, rot[0:128, 0:128]) * sin[:, 0:128]
    k_ref[...] = jnp.concatenate([ckv, krope], axis=1)


def _mla_prep_call(rows, zB, cos, sin, gq, gkv, wqn, wqr, wa, pb, rot):
    R = rows.R
    HC = MLA_HEADS * 256
    c2 = lambda i: (0, 0)
    return pl.pallas_call(
        _mla_prep_kernel, name="mla_prep",
        grid=(rows.nb,),
        in_specs=[pl.BlockSpec((R, W_B), lambda i: (i, 0)),
                  pl.BlockSpec((R, 256), lambda i: (i, 0)),
                  pl.BlockSpec((R, 256), lambda i: (i, 0)),
                  pl.BlockSpec((1, MLA_Q_RANK), c2), pl.BlockSpec((1, MLA_KV_RANK), c2),
                  pl.BlockSpec((MLA_Q_RANK, MLA_HEADS * MLA_NOPE), c2),
                  pl.BlockSpec((MLA_Q_RANK, MLA_HEADS * MLA_ROPE), c2),
                  pl.BlockSpec((MLA_HEADS * MLA_NOPE, HC), c2),
                  pl.BlockSpec((MLA_HEADS * MLA_ROPE, HC), c2),
                  pl.BlockSpec((256, 256), c2)],
        out_specs=[pl.BlockSpec((R, HC), lambda i: (i, 0)), pl.BlockSpec((R, 256), lambda i: (i, 0))],
        out_shape=[jax.ShapeDtypeStruct((rows.N, HC), bf16), jax.ShapeDtypeStruct((rows.N, 256), f32)],
        compiler_params=_cparams("arbitrary"),
    )(zB, cos, sin, gq, gkv, wqn, wqr, wa, pb, rot)


def _flash_kernel(qtab_ref, ktab_ref, q_ref, k_ref, wuv_ref, y_ref, qs_scr, m_scr, l_scr, acc_scr, *, tq, tk):
    qi = qtab_ref[pl.program_id(1)]
    ki = ktab_ref[pl.program_id(1)]
    H = MLA_HEADS
    NR = H * tq
    diag = (qi * tq + tq - 1) // tk

    @pl.when(ki == 0)
    def _():
        for h in range(H):
            qs_scr[h * tq:(h + 1) * tq, :] = q_ref[:, h * 256:(h + 1) * 256]
        m_scr[...] = jnp.full(m_scr.shape, NEG, f32)
        l_scr[...] = jnp.zeros(l_scr.shape, f32)
        acc_scr[...] = jnp.zeros(acc_scr.shape, f32)

    def step(masked):
        kb = _bf(k_ref[...])
        s = _dot_nt(qs_scr[...], kb) * MLA_SCALE
        if masked:
            qpos = qi * tq + _iota((NR, tk), 0) % tq
            s = jnp.where(ki * tk + _iota((NR, tk), 1) <= qpos, s, NEG)
        m_prev = m_scr[...]
        m_next = jnp.maximum(m_prev, jnp.max(s, axis=1, keepdims=True))
        alpha = jnp.exp(m_prev - m_next)
        p = jnp.exp(s - jnp.concatenate([m_next] * (tk // 128), axis=1))
        l_scr[...] = alpha * l_scr[...] + jnp.sum(p, axis=1, keepdims=True)
        acc_scr[...] = alpha * acc_scr[...] + _dot(_bf(p), kb[:, 0:MLA_KV_RANK])
        m_scr[...] = m_next

    pl.when(ki < diag)(functools.partial(step, False))
    pl.when(ki == diag)(functools.partial(step, True))

    @pl.when(ki == diag)
    def _():
        o = _bf(acc_scr[...] / l_scr[...])
        y = jnp.zeros((tq, MLA_HEADS * MLA_V), f32)
        for h in range(H):
            y = y + _dot(o[h * tq:(h + 1) * tq, :], wuv_ref[h])
        y_ref[...] = _bf(y)


def _flash_call(qcat, kcat, wuv, B, T):
    tq = 128
    tk = _pick_block(T, (512, 256, 128))
    nq, nk = T // tq, T // tk
    H = MLA_HEADS
    pairs = [(i, j) for i in range(nq) for j in range((i * tq + tq - 1) // tk + 1)]
    qtab = jnp.asarray([p[0] for p in pairs], jnp.int32)
    ktab = jnp.asarray([p[1] for p in pairs], jnp.int32)
    grid_spec = pltpu.PrefetchScalarGridSpec(
        num_scalar_prefetch=2,
        grid=(B, len(pairs)),
        in_specs=[pl.BlockSpec((tq, H * 256), lambda b, p, qt, kt: (b * nq + qt[p], 0)),
                  pl.BlockSpec((tk, 256), lambda b, p, qt, kt: (b * nk + kt[p], 0)),
                  pl.BlockSpec((H, MLA_KV_RANK, H * MLA_V), lambda b, p, qt, kt: (0, 0, 0))],
        out_specs=pl.BlockSpec((tq, H * MLA_V), lambda b, p, qt, kt: (b * nq + qt[p], 0)),
        scratch_shapes=[pltpu.VMEM((H * tq, 256), bf16), pltpu.VMEM((H * tq, 128), f32),
                        pltpu.VMEM((H * tq, 128), f32), pltpu.VMEM((H * tq, MLA_KV_RANK), f32)],
    )
    return pl.pallas_call(
        functools.partial(_flash_kernel, tq=tq, tk=tk), name="mla_flash",
        grid_spec=grid_spec,
        out_shape=jax.ShapeDtypeStruct((B * T, H * MLA_V), bf16),
        compiler_params=_cparams("arbitrary", "arbitrary"),
    )(qtab, ktab, qcat, kcat, wuv)


def _paged_kernel(pt_ref, q_ref, kn_ref, wuv_ref, ck_hbm, krt_hbm, y_ref,
                  ck_buf, krt_buf, kb_scr, sem, *, l, n_pages, page, TS, chunk):
    b = pl.program_id(0)
    nb = pl.num_programs(0)
    H = MLA_HEADS
    NR = TS * H
    n_keys = n_pages * page
    slot = b % 2

    def page_copies(bb, sl, p):
        pg = pt_ref[bb * n_pages + p]
        keys_ = pl.ds(p * page, page)
        return (pltpu.make_async_copy(ck_hbm.at[l, pg], ck_buf.at[sl, keys_, :], sem.at[0, sl]),
                pltpu.make_async_copy(krt_hbm.at[l, pg], krt_buf.at[sl, :, keys_], sem.at[1, sl]))

    def start_all(bb, sl):
        for p in range(n_pages):
            for cp in page_copies(bb, sl, p):
                cp.start()

    def wait_all(bb, sl):
        for p in range(n_pages):
            for cp in page_copies(bb, sl, p):
                cp.wait()

    @pl.when(b == 0)
    def _():
        start_all(0, 0)

    @pl.when(b + 1 < nb)
    def _():
        start_all(b + 1, 1 - slot)

    wait_all(b, slot)

    q = q_ref[...]
    ql = q[:, 0:MLA_KV_RANK]
    qr = q[:, MLA_KV_RANK:MLA_KV_RANK + MLA_ROPE]
    kn = _bf(kn_ref[...])
    s_new = _dot_nt(q, kn) * MLA_SCALE
    ok = (_iota((NR, 8), 1) <= _iota((NR, 8), 0) // H) & (_iota((NR, 8), 1) < TS)
    s_new = jnp.where(ok, s_new, NEG)
    m = jnp.max(s_new, axis=1, keepdims=True)
    ss = []
    for c in range(n_keys // chunk):
        rows_ = pl.ds(c * chunk, chunk)
        kb_scr[rows_, :] = _bf(ck_buf[slot, rows_, :])
        s = (_dot_nt(ql, kb_scr[rows_, :]) + _dot(qr, _bf(krt_buf[slot, :, rows_]))) * MLA_SCALE
        m = jnp.maximum(m, jnp.max(s, axis=1, keepdims=True))
        ss.append(s)
    e = jnp.exp(s_new - m)
    lsum = jnp.sum(e, axis=1, keepdims=True)
    acc = _dot(_bf(e), kn[:, 0:MLA_KV_RANK])
    for c in range(n_keys // chunk):
        e = jnp.exp(ss[c] - m)
        lsum = lsum + jnp.sum(e, axis=1, keepdims=True)
        acc = acc + _dot(_bf(e), kb_scr[pl.ds(c * chunk, chunk), :])
    o = acc / lsum
    row_head = _iota((NR, MLA_KV_RANK), 0) % H
    y = jnp.zeros((NR, H * MLA_V), f32)
    for h in range(H):
        y = y + _dot(_bf(jnp.where(row_head == h, o, 0.0)), wuv_ref[h])
    y_ref[...] = _bf(jnp.sum(y.reshape(TS, H, H * MLA_V), axis=1))


def _paged_call(page_table, q3, kn3, wuv, cache_ckv, cache_krt, l):
    DB, n_pages = page_table.shape
    TS = q3.shape[1] // MLA_HEADS
    H = MLA_HEADS
    page = cache_ckv.shape[2]
    n_keys = n_pages * page
    chunk = _pick_block(n_keys, (2048, 1024, 512, 256, 128))
    grid_spec = pltpu.PrefetchScalarGridSpec(
        num_scalar_prefetch=1,
        grid=(DB,),
        in_specs=[pl.BlockSpec((None, TS * H, 256), lambda b, pt: (b, 0, 0)),
                  pl.BlockSpec((None, 8, 256), lambda b, pt: (b, 0, 0)),
                  pl.BlockSpec((H, MLA_KV_RANK, H * MLA_V), lambda b, pt: (0, 0, 0)),
                  pl.BlockSpec(memory_space=pl.ANY), pl.BlockSpec(memory_space=pl.ANY)],
        out_specs=pl.BlockSpec((None, TS, H * MLA_V), lambda b, pt: (b, 0, 0)),
        scratch_shapes=[pltpu.VMEM((2, n_keys, MLA_KV_RANK), f32),
                        pltpu.VMEM((2, MLA_ROPE, n_keys), f32),
                        pltpu.VMEM((n_keys, MLA_KV_RANK), bf16),
                        pltpu.SemaphoreType.DMA((2, 2))],
    )
    return pl.pallas_call(
        functools.partial(_paged_kernel, l=l, n_pages=n_pages, page=page, TS=TS, chunk=chunk), name="mla_paged",
        grid_spec=grid_spec,
        out_shape=jax.ShapeDtypeStruct((DB, TS, H * MLA_V), bf16),
        compiler_params=_cparams("arbitrary"),
    )(page_table.reshape(-1), q3, kn3, wuv, cache_ckv, cache_krt)


def _merge_kernel(h_ref, ya_ref, yb_ref, yc_ref, yd_ref, g0, g1, g2, g3, b0, b1, b2, b3, o_ref):
    h = h_ref[...]
    acc = None
    for y_ref, g_ref, b_ref in ((ya_ref, g0, b0), (yb_ref, g1, b1), (yc_ref, g2, b2), (yd_ref, g3, b3)):
        gate = _sigmoid(_dot_nt(h, _bf(g_ref[...])))
        term = gate * _dot(y_ref[...], _bf(b_ref[...]))
        acc = term if acc is None else acc + term
    o_ref[...] = _bf(acc)


def _merge_call(h1, ys, w_in_t, w_br, l):
    N = h1.shape[0]
    D = D_MODEL
    tm = _pick_block(N, (1088, 640, 512, 256, 128))
    tn = 256
    ncol = D // tn
    gate_specs = [pl.BlockSpec((None, tn, D), (lambda i, j, br=br: (l, br * ncol + j, 0))) for br in range(N_BRANCH)]
    br_specs = [pl.BlockSpec((None, None, BRANCH_WIDTH, tn), (lambda i, j, br=br: (l, br, 0, j))) for br in range(N_BRANCH)]
    y_specs = [pl.BlockSpec((tm, BRANCH_WIDTH), lambda i, j: (i, 0)) for _ in range(N_BRANCH)]
    return pl.pallas_call(
        _merge_kernel, name="merge",
        grid=(N // tm, ncol),
        in_specs=[pl.BlockSpec((tm, D), lambda i, j: (i, 0))] + y_specs + gate_specs + br_specs,
        out_specs=pl.BlockSpec((tm, tn), lambda i, j: (i, j)),
        out_shape=jax.ShapeDtypeStruct((N, D), bf16),
        compiler_params=_cparams("arbitrary", "arbitrary"),
    )(h1, *ys, w_in_t, w_in_t, w_in_t, w_in_t, w_br, w_br, w_br, w_br)


def _outproj_kernel(x_ref, mg_ref, wo_ref, g_ref, g1s, g1p, shs, shp, scs, scp, xo_ref, h_ref, hb_ref, *, rows):
    i = pl.program_id(0)
    x = x_ref[...] + rows.pick(i, g1s, g1p) * _dot(mg_ref[...], wo_ref[...])
    xo_ref[...] = x
    h = _rms(x, g_ref[...]) * (1.0 + rows.pick(i, scs, scp)) + rows.pick(i, shs, shp)
    h_ref[...] = h
    hb_ref[...] = _bf(h)


def _outproj_call(rows, x, merged, wo_bf, gain2, mod, l):
    R = rows.R
    D = D_MODEL
    g1s, g1p = rows.mod_specs(l, 2)
    shs, shp = rows.mod_specs(l, 3)
    scs, scp = rows.mod_specs(l, 4)
    return pl.pallas_call(
        functools.partial(_outproj_kernel, rows=rows), name="outproj",
        grid=(rows.nb,),
        in_specs=[pl.BlockSpec((R, D), lambda i: (i, 0)), pl.BlockSpec((R, D), lambda i: (i, 0)),
                  pl.BlockSpec((D, D), lambda i: (0, 0)), pl.BlockSpec((1, D), lambda i: (0, 0)),
                  g1s, g1p, shs, shp, scs, scp],
        out_specs=[pl.BlockSpec((R, D), lambda i: (i, 0))] * 3,
        out_shape=[jax.ShapeDtypeStruct((rows.N, D), f32), jax.ShapeDtypeStruct((rows.N, D), f32),
                   jax.ShapeDtypeStruct((rows.N, D), bf16)],
        compiler_params=_cparams("arbitrary"),
    )(x, merged, wo_bf, gain2.reshape(1, D), mod, mod, mod, mod, mod, mod)


def _peer_candidates():
    K = PEER_TOPK
    pairs = [(k1, k2) for k1 in range(K) for k2 in range(K) if (k1 + 1) * (k2 + 1) <= K]
    n = -(-len(pairs) // 8) * 8
    g1 = np.zeros((n, K), np.float32)
    g2 = np.zeros((n, K), np.float32)
    for r, (k1, k2) in enumerate(pairs):
        g1[r, k1] = 1.0
        g2[r, k2] = 1.0
    return len(pairs), jnp.asarray(g1, bf16), jnp.asarray(g2, bf16)


def _select_rows3(sel_bf, v):
    hi = _bf(v)
    r1 = v - hi.astype(f32)
    mid = _bf(r1)
    lo = _bf(r1 - mid.astype(f32))
    return (_dot(sel_bf, hi) + _dot(sel_bf, mid)) + _dot(sel_bf, lo)


def _peer_score_kernel(h_ref, wqh_ref, wql_ref, k1_ref, k2_ref, g1_ref, g2_ref, a_ref, b_ref, g_ref,
                       v1_scr, i1_scr, v2_scr, i2_scr, top_scr, ea_scr, eb_scr, *, n_cand):
    T = h_ref.shape[0]
    NK, K = PEER_NKEYS, PEER_TOPK
    LT = 128
    NC = g1_ref.shape[0]
    hh, hl = _split2(h_ref[...])
    q = _dot(hh, wqh_ref[...]) + (_dot(hh, wql_ref[...]) + _dot(hl, wqh_ref[...]))
    key_idx = _iota((NK, LT), 0).astype(f32)
    cand_idx = _iota((NC, LT), 0).astype(f32)
    neg_inf = -jnp.inf
    g1 = g1_ref[...]
    g2 = g2_ref[...]

    def topk_into(s, val_scr, idx_scr):
        for kk in range(K):
            m = jnp.max(s, axis=0, keepdims=True)
            pos = jnp.min(jnp.where(s == m, key_idx, float(NK)), axis=0, keepdims=True)
            val_scr[pl.ds(kk, 1), :] = m
            idx_scr[pl.ds(kk, 1), :] = pos
            s = jnp.where(key_idx == pos, neg_inf, s)

    for h in range(PEER_HEADS):
        qh, ql = _split2(q[:, h * PEER_DKEY:(h + 1) * PEER_DKEY])
        sTs = []
        for k_ref in (k1_ref, k2_ref):
            kh, kl = _split2(k_ref[h])
            sTs.append(_dot_nt(kh, qh) + (_dot_nt(kh, ql) + _dot_nt(kl, qh)))
        for lt in range(T // LT):
            cols = slice(lt * LT, (lt + 1) * LT)
            topk_into(sTs[0][:, cols], v1_scr, i1_scr)
            topk_into(sTs[1][:, cols], v2_scr, i2_scr)
            cand = _select_rows3(g1, v1_scr[...]) + _select_rows3(g2, v2_scr[...])
            cand = jnp.where(cand_idx < float(n_cand), cand, neg_inf)
            ca = _dot(g1, _bf(i1_scr[...]))
            cb = _dot(g2, _bf(i2_scr[...]))
            for kk in range(K):
                m = jnp.max(cand, axis=0, keepdims=True)
                pos = jnp.min(jnp.where(cand == m, cand_idx, float(NC)), axis=0, keepdims=True)
                sel = cand_idx == pos
                r = h * K + kk
                top_scr[pl.ds(r, 1), cols] = m
                ea_scr[pl.ds(r, 1), cols] = jnp.sum(jnp.where(sel, ca, 0.0), axis=0, keepdims=True)
                eb_scr[pl.ds(r, 1), cols] = jnp.sum(jnp.where(sel, cb, 0.0), axis=0, keepdims=True)
                cand = jnp.where(sel, neg_inf, cand)
        tops = top_scr[pl.ds(h * K, K), :]
        e = jnp.exp(tops - jnp.max(tops, axis=0, keepdims=True))
        top_scr[pl.ds(h * K, K), :] = e / jnp.sum(e, axis=0, keepdims=True)
    g_ref[...] = top_scr[...].T
    a_ref[...] = ea_scr[...].T
    b_ref[...] = eb_scr[...].T


def _peer_score_call(rows, h2, wqh, wql, k1p, k2p):
    R = rows.R
    D = D_MODEL
    HK = PEER_HEADS * PEER_TOPK
    o_spec = pl.BlockSpec((R, HK), lambda i: (i, 0))
    o_shape = jax.ShapeDtypeStruct((rows.N, HK), f32)
    n_cand, g1, g2 = _peer_candidates()
    return pl.pallas_call(
        functools.partial(_peer_score_kernel, n_cand=n_cand), name="peer_score",
        grid=(rows.nb,),
        in_specs=[pl.BlockSpec((R, D), lambda i: (i, 0)),
                  pl.BlockSpec((D, PEER_HEADS * PEER_DKEY), lambda i: (0, 0)),
                  pl.BlockSpec((D, PEER_HEADS * PEER_DKEY), lambda i: (0, 0)),
                  pl.BlockSpec((PEER_HEADS, PEER_NKEYS, PEER_DKEY), lambda i: (0, 0, 0)),
                  pl.BlockSpec((PEER_HEADS, PEER_NKEYS, PEER_DKEY), lambda i: (0, 0, 0)),
                  pl.BlockSpec(g1.shape, lambda i: (0, 0)), pl.BlockSpec(g2.shape, lambda i: (0, 0))],
        out_specs=[o_spec, o_spec, o_spec],
        out_shape=[o_shape, o_shape, o_shape],
        scratch_shapes=[pltpu.VMEM((PEER_TOPK, 128), f32)] * 4 + [pltpu.VMEM((HK, R), f32)] * 3,
        compiler_params=_cparams("arbitrary"),
    )(h2, wqh, wql, k1p, k2p, g1, g2)


def _peer_mix_kernel(h_ref, ea_ref, eb_ref, g_ref, u_ref, v_ref, o_ref, a_scr, *, T, nch, ac, n_half):
    j = pl.program_id(1)
    NK = PEER_NKEYS
    NA = NK // n_half
    pitch = T + PEER_PITCH_PAD
    half = j // (2 * nch)
    jj = j % (2 * nch)

    @pl.when(j == 0)
    def _():
        o_ref[...] = jnp.zeros(o_ref.shape, f32)

    @pl.when(jj < nch)
    def _():
        res = _dot_nt(h_ref[...], u_ref[...])
        for al in range(ac):
            start = pl.multiple_of((jj * ac + al) * pitch, 8)
            a_scr[pl.ds(start, T), :] = res[:, al * NK:(al + 1) * NK]

    @pl.when(jj == nch)
    def _():
        sub_a = (_iota((NA, NK), 0) + half * NA).astype(f32).astype(bf16)
        sub_b = _iota((NK, NK), 0).astype(f32).astype(bf16)
        one = jnp.ones((NA, NK), bf16)
        zero_a = jnp.zeros((NA, NK), bf16)
        zero_b = jnp.zeros((NK, NK), bf16)

        def gate_matrix(t):
            ia = jnp.broadcast_to(_bf(ea_ref[pl.ds(t, 1), :]), (NA, NK))
            ib = jnp.broadcast_to(_bf(eb_ref[pl.ds(t, 1), :]), (NK, NK))
            gg = jnp.broadcast_to(_bf(g_ref[pl.ds(t, 1), :]), (NK, NK))
            onehot_a = jnp.where(sub_a == ia, one, zero_a)
            gate_b = jnp.where(sub_b == ib, gg, zero_b)
            return _dot_nt(onehot_a, gate_b)

        def tok_group(tg, carry):
            ts = [tg * PEER_TOKEN_UNROLL + u for u in range(PEER_TOKEN_UNROLL)]
            ws = [gate_matrix(t) for t in ts]
            for t, w in zip(ts, ws):
                a_scr[pl.ds(t, NA, stride=pitch), :] = w * _gelu(a_scr[pl.ds(t, NA, stride=pitch), :])
            return carry

        lax.fori_loop(0, T // PEER_TOKEN_UNROLL, tok_group, 0)

    @pl.when(jj >= nch)
    def _():
        parts = []
        for al in range(ac):
            start = pl.multiple_of(((jj - nch) * ac + al) * pitch, 8)
            parts.append(_bf(a_scr[pl.ds(start, T), :]))
        o_ref[...] += _dot(jnp.concatenate(parts, axis=1), v_ref[...])


def _peer_mix_call(rows, h2b, ea, eb, g, ub_all, vb_all, l):
    T = rows.R
    D = D_MODEL
    NK = PEER_NKEYS
    n_half = 2
    ac = 8
    nch = NK // n_half // ac
    HK = PEER_HEADS * PEER_TOPK
    row = lambda i, j: (i, 0)

    def u_idx(i, j):
        return (l, (j // (2 * nch)) * nch + jnp.minimum(j % (2 * nch), nch - 1), 0)

    def v_idx(i, j):
        return (l, (j // (2 * nch)) * nch + jnp.maximum(j % (2 * nch) - nch, 0), 0)

    return pl.pallas_call(
        functools.partial(_peer_mix_kernel, T=T, nch=nch, ac=ac, n_half=n_half), name="peer_mix",
        grid=(rows.nb, n_half * 2 * nch),
        in_specs=[pl.BlockSpec((T, D), row),
                  pl.BlockSpec((T, HK), row), pl.BlockSpec((T, HK), row), pl.BlockSpec((T, HK), row),
                  pl.BlockSpec((None, ac * NK, D), u_idx),
                  pl.BlockSpec((None, ac * NK, D), v_idx)],
        out_specs=pl.BlockSpec((T, D), row),
        out_shape=jax.ShapeDtypeStruct((rows.N, D), f32),
        scratch_shapes=[pltpu.VMEM((NK // n_half * (T + PEER_PITCH_PAD), NK), f32)],
        compiler_params=_cparams("arbitrary", "arbitrary"),
    )(h2b, ea, eb, g, ub_all, vb_all)


def _final_kernel(x_ref, p_ref, g2s, g2p, g_ref, o_ref, *, rows):
    x = x_ref[...] + rows.pick(pl.program_id(0), g2s, g2p) * p_ref[...]
    o_ref[...] = _rms(x, g_ref[...])


def _final_call(rows, x, peer_out, gain, mod, l_last):
    R, D = rows.R, D_MODEL
    g2s, g2p = rows.mod_specs(l_last, 5)
    row = pl.BlockSpec((R, D), lambda i: (i, 0))
    return pl.pallas_call(
        functools.partial(_final_kernel, rows=rows), name="final_norm", grid=(rows.nb,),
        in_specs=[row, row, g2s, g2p, pl.BlockSpec((1, D), lambda i: (0, 0))],
        out_specs=row,
        out_shape=jax.ShapeDtypeStruct((rows.N, D), f32),
        compiler_params=_cparams("arbitrary"),
    )(x, peer_out, mod, mod, gain.reshape(1, D))


def _mla_constants():
    H, R = MLA_HEADS, MLA_ROPE
    half = R // 2
    rot = np.zeros((H * R, H * R), np.float32)
    for g in range(H):
        for jj in range(half):
            rot[g * R + half + jj, g * R + jj] = -1.0
            rot[g * R + jj, g * R + half + jj] = 1.0
    place = np.zeros((H * R, H * 256), np.float32)
    for g in range(H):
        for jj in range(R):
            place[g * R + jj, g * 256 + MLA_KV_RANK + jj] = 1.0
    return jnp.asarray(rot, bf16), jnp.asarray(place, bf16)


def kernel(x_prompt, x_sample, cache_mla_ckv, cache_mla_kr, state_gla, state_mlstm_C, state_mlstm_n, state_mlstm_m, page_table, c_prompt, c_sample, w_ada, b_ada, norm1, norm2, w_in, gla_wa2, gla_ba, gla_norm, mla_w_uq, mla_gq, mla_gkv, mla_w_uk, mla_w_uv, ml_bi, ml_bf, ml_norm, sg_gv, sg_ws, sg_bs, w_br, w_o, peer_wq, peer_k1, peer_k2, peer_u, peer_v, final_norm):
    D = D_MODEL
    B, T, _ = x_prompt.shape
    DB, TS, _ = x_sample.shape
    depth = w_ada.shape[0]
    n_pages = page_table.shape[1]
    page = cache_mla_ckv.shape[2]
    past_len = n_pages * page
    rows = _Rows(B, T, DB, TS)
    Np, Ns = rows.Np, rows.Ns
    assert Ns % 128 == 0 and T % 128 == 0 and TS == 4

    c_rows = jnp.concatenate([jnp.repeat(c_sample, TS, axis=0), c_prompt,
                              jnp.zeros((16 - B, D), f32)], axis=0)
    mod = _ada_call(c_rows, w_ada, b_ada)

    x = jnp.concatenate([x_prompt.reshape(Np, D), x_sample.reshape(Ns, D)], axis=0)

    inv = ROPE_THETA ** (-jnp.arange(MLA_ROPE // 2, dtype=f32) / (MLA_ROPE // 2))
    pos_p = jnp.arange(T, dtype=jnp.int32)
    pos_s = past_len + jnp.arange(TS, dtype=jnp.int32)

    def table(fn):
        tp = fn(pos_p.astype(f32)[:, None] * inv)
        ts = fn(pos_s.astype(f32)[:, None] * inv)
        rows_ = jnp.concatenate([jnp.tile(tp, (B, 1)), jnp.tile(ts, (DB, 1))], axis=0)
        return jnp.tile(rows_, (1, 2 * MLA_HEADS))

    cos_t, sin_t = table(jnp.cos), table(jnp.sin)
    rot, place = _mla_constants()
    eye_h = jnp.eye(MLA_HEADS, dtype=f32)
    Ls = math.gcd(TS, GLA_CHUNK)
    Lp_gla = math.gcd(T, GLA_CHUNK)
    Lp_ml = math.gcd(T, ML_CHUNK)
    Ls_ml = math.gcd(TS, ML_CHUNK)
    spb = SG_CHUNK // TS
    w_in_t = jnp.swapaxes(w_in, 1, 2)
    cache_krt = jnp.swapaxes(cache_mla_kr, 2, 3)
    w_mix_all = _bf(w_in_t[:, N_BRANCH * D:, :])
    ub_all = _bf(peer_u)
    vb_all = _bf(peer_v)
    rows_peer = _Rows(B, T, DB, TS, block_sizes=(512, 256, 128))

    new_p, new_s = [], []
    for l in range(depth):
        wl = w_mix_all[l]
        zr = lambda n: jnp.zeros((n, D), bf16)
        wA = jnp.concatenate([wl[0:1552], zr(112)], axis=0)
        wB = jnp.concatenate([wl[1552:2096], zr(96)], axis=0)
        wC = jnp.concatenate([wl[2096:3632], wl[3632:3636], zr(124), wl[3636:3640], zr(124)], axis=0)
        wD = wl[3640:4664]
        wa2p = jnp.concatenate([gla_wa2[l], jnp.zeros((128 - GLA_RANK, GLA_HEADS * GLA_DK), f32)], axis=0)
        uq = mla_w_uq[l].reshape(MLA_Q_RANK, MLA_HEADS, MLA_NOPE + MLA_ROPE)
        wqn = _bf(uq[:, :, :MLA_NOPE].reshape(MLA_Q_RANK, MLA_HEADS * MLA_NOPE))
        wqr = _bf(uq[:, :, MLA_NOPE:].reshape(MLA_Q_RANK, MLA_HEADS * MLA_ROPE))
        uk = jnp.transpose(mla_w_uk[l], (1, 2, 0))
        wa = jnp.einsum('hnr,hg->hngr', uk, eye_h)
        wa = _bf(jnp.pad(wa, ((0, 0), (0, 0), (0, 0), (0, 256 - MLA_KV_RANK))).reshape(MLA_HEADS * MLA_NOPE, MLA_HEADS * 256))
        wuv = _bf(jnp.einsum('rhv,hg->hrgv', mla_w_uv[l], eye_h).reshape(MLA_HEADS, MLA_KV_RANK, MLA_HEADS * MLA_V))
        pad128 = lambda a: jnp.pad(a, (0, 128 - a.shape[0])).reshape(1, 128)
        bcol_p = jnp.pad(sg_bs[l].T, ((0, 0), (0, 128 - SG_GROUPS)))
        ws_s = jnp.stack([jnp.kron(jnp.eye(spb, dtype=f32), sg_ws[l, g, :TS, :TS]) for g in range(SG_GROUPS)])
        bcol_s = jnp.pad(jnp.tile(sg_bs[l][:, :TS].T, (spb, 1)), ((0, 0), (0, 128 - SG_GROUPS)))
        wo_bf = _bf(w_o[l])
        wq = peer_wq[l]
        wqh = _bf(wq)
        wql = _bf(wq - wqh.astype(f32))
        half = PEER_DKEY // 2
        k1p = jnp.pad(peer_k1[l], ((0, 0), (0, 0), (0, half)))
        k2p = jnp.pad(peer_k2[l], ((0, 0), (0, 0), (half, 0)))

        if l == 0:
            h1 = _modnorm_call(rows, x, None, norm1[l], mod, l)
        else:
            x, h1 = _modnorm_call(rows, x, peer_out, norm1[l], mod, l)
        zA, zB, zC, zD = (_mm_call(h1, w) for w in (wA, wB, wC, wD))

        gn_a = gla_norm[l].reshape(1, -1)
        ba = gla_ba[l].reshape(1, -1)
        ya_p, S_p = _gla_call(zA, 0, B, T, Lp_gla, wa2p, ba, gn_a, None)
        ya_s, S_s = _gla_call(zA, Np, DB, TS, Ls, wa2p, ba, gn_a, state_gla[l])

        qcat, kcat = _mla_prep_call(rows, zB, cos_t, sin_t, mla_gq[l].reshape(1, -1), mla_gkv[l].reshape(1, -1),
                                    wqn, wqr, wa, place, rot)
        yb_p = _flash_call(qcat, kcat, wuv, B, T)
        q3 = qcat[Np:].reshape(DB, TS * MLA_HEADS, 256)
        kn3 = jnp.pad(kcat[Np:].reshape(DB, TS, 256), ((0, 0), (0, 8 - TS), (0, 0)))
        yb_s = _paged_call(page_table, q3, kn3, wuv, cache_mla_ckv, cache_krt, l).reshape(Ns, -1)

        bi, bfg, gn_c = pad128(ml_bi[l]), pad128(ml_bf[l]), ml_norm[l].reshape(1, -1)
        yc_p, C_p, n_p, m_p = _mlstm_call(zC, 0, B, T, Lp_ml, bi, bfg, gn_c, None)
        st = (state_mlstm_C[l], state_mlstm_n[l].reshape(DB, 1, -1),
              jnp.pad(state_mlstm_m[l], ((0, 0), (0, 128 - ML_HEADS))).reshape(DB, 1, 128))
        yc_s, C_s, n_s, m_s = _mlstm_call(zC, Np, DB, TS, Ls_ml, bi, bfg, gn_c, st)

        gv = sg_gv[l].reshape(1, -1)
        (yd_p,) = _sgu_call(zD, 0, Np, gv, sg_ws[l], bcol_p, False)
        yd_s, v_s = _sgu_call(zD, Np, Ns, gv, ws_s, bcol_s, True)

        ys = [jnp.concatenate([p_, s_], axis=0) for p_, s_ in ((ya_p, ya_s), (yb_p, yb_s), (yc_p, yc_s), (yd_p, yd_s))]
        merged = _merge_call(h1, ys, w_in_t, w_br, l)
        x, h2, h2b = _outproj_call(rows, x, merged, wo_bf, norm2[l], mod, l)

        ea, eb, g = _peer_score_call(rows, h2, wqh, wql, k1p, k2p)
        peer_out = _peer_mix_call(rows_peer, h2b, ea, eb, g, ub_all, vb_all, l)

        new_p.append(dict(ckv=kcat[:Np, :MLA_KV_RANK].reshape(B, T, -1),
                          kr=kcat[:Np, MLA_KV_RANK:MLA_KV_RANK + MLA_ROPE].reshape(B, T, -1),
                          S=S_p, C=C_p, n=n_p.reshape(B, ML_HEADS, ML_DK), m=m_p[:, 0, :ML_HEADS]))
        new_s.append(dict(ckv=kcat[Np:, :MLA_KV_RANK].reshape(DB, TS, -1),
                          kr=kcat[Np:, MLA_KV_RANK:MLA_KV_RANK + MLA_ROPE].reshape(DB, TS, -1),
                          S=S_s, C=C_s, n=n_s.reshape(DB, ML_HEADS, ML_DK), m=m_s[:, 0, :ML_HEADS],
                          v=v_s.reshape(DB, TS, -1)))

    y = _final_call(rows, x, peer_out, final_norm, mod, depth - 1)
    stk = lambda outs, name: jnp.stack([o[name] for o in outs], axis=0)
    return (y[:Np].reshape(B, T, D), y[Np:].reshape(DB, TS, D),
            stk(new_p, 'ckv'), stk(new_p, 'kr'), stk(new_p, 'S'), stk(new_p, 'C'), stk(new_p, 'n'), stk(new_p, 'm'),
            stk(new_s, 'ckv'), stk(new_s, 'kr'), stk(new_s, 'S'), stk(new_s, 'C'), stk(new_s, 'n'), stk(new_s, 'm'),
            stk(new_s, 'v'))
```
